```python
import jax, jax.numpy as jnp
from jax import lax
import numpy as np


D_MODEL = 1024
BATCH = 4
SEQ = 8192
DEPTH = 1

CHUNK = 64
Q_BLOCK = 128
SB_HEADS = 8
SB_HEAD_DIM = 64
SB_WIDTH = SB_HEADS * SB_HEAD_DIM
CONV_WIDTH = D_MODEL // 2
CONV_K = 3
N_GROUPS = 4
EXPERTS_PER_GROUP = 8
N_EXPERTS = N_GROUPS * EXPERTS_PER_GROUP
TOP_K = 2
D_EXPERT = D_MODEL // 2
EXPERT_BLOCK = 128
PROJ_WIDTH = 3 * SB_WIDTH + 3 * CONV_WIDTH + 2 * D_MODEL
DEEPNORM_ALPHA = (2.0 * DEPTH) ** 0.25
DEEPNORM_BETA = (8.0 * DEPTH) ** -0.25
LN_EPS = 1e-5

kernel_name = 'hybrid_stickbreak_shortconv_hmoe'


def layer_norm(x, g, b):
    xf = x.astype(jnp.float32)
    mu = jnp.mean(xf, axis=-1, keepdims=True)
    var = jnp.mean(jnp.square(xf - mu), axis=-1, keepdims=True)
    return ((xf - mu) * lax.rsqrt(var + LN_EPS) * g + b).astype(x.dtype)


def stick_breaking_attention(q, k, v):
    bsz, h, s, dh = q.shape
    nq = s // Q_BLOCK
    scale = dh ** -0.5
    q_blocks = q.reshape(bsz, h, nq, Q_BLOCK, dh).transpose(2, 0, 1, 3, 4)
    kf = k.astype(jnp.float32)
    vf = v.astype(jnp.float32)
    k_pos = jnp.arange(s)

    def one_block(args):
        qb, blk = args
        z = jnp.einsum('bhqd,bhkd->bhqk', qb.astype(jnp.float32), kf) * scale
        q_pos = blk * Q_BLOCK + jnp.arange(Q_BLOCK)
        mask = k_pos[None, :] < q_pos[:, None]
        log_stay = jnp.where(mask, jax.nn.log_sigmoid(-z), 0.0)
        tail = lax.cumsum(log_stay, axis=3, reverse=True)
        tail = jnp.concatenate([tail[..., 1:], jnp.zeros_like(tail[..., :1])], axis=-1)
        weights = jnp.where(mask, jnp.exp(jax.nn.log_sigmoid(z) + tail), 0.0)
        return jnp.einsum('bhqk,bhkd->bhqd', weights, vf)

    out = lax.map(one_block, (q_blocks, jnp.arange(nq)))
    return out.transpose(1, 2, 0, 3, 4).reshape(bsz, h, s, dh).astype(q.dtype)


def short_gated_conv(cb, cc, ch, conv_w):
    u = cc * ch
    s = u.shape[1]
    u_pad = jnp.pad(u, ((0, 0), (CONV_K - 1, 0), (0, 0)))
    y = sum(conv_w[i] * u_pad[:, i:i + s] for i in range(CONV_K))
    return cb * y


def token_mixer(x, w_in, gate_bias, conv_w, w_branch_a, w_branch_b, w_out):
    bsz, s, _ = x.shape
    proj = x @ w_in
    splits = [SB_WIDTH, 2 * SB_WIDTH, 3 * SB_WIDTH,
              3 * SB_WIDTH + CONV_WIDTH, 3 * SB_WIDTH + 2 * CONV_WIDTH,
              3 * SB_WIDTH + 3 * CONV_WIDTH]
    q, k, v, cb, cc, ch, gate_logits = jnp.split(proj, splits, axis=-1)

    def heads(t):
        return t.reshape(bsz, s, SB_HEADS, SB_HEAD_DIM).transpose(0, 2, 1, 3)

    attn = stick_breaking_attention(heads(q), heads(k), heads(v))
    attn = attn.transpose(0, 2, 1, 3).reshape(bsz, s, SB_WIDTH)
    branch_a = attn @ w_branch_a
    branch_b = short_gated_conv(cb, cc, ch, conv_w) @ w_branch_b
    gates = jax.nn.sigmoid((gate_logits + gate_bias).astype(jnp.float32)).astype(x.dtype)
    gate_a, gate_b = jnp.split(gates, 2, axis=-1)
    return (gate_a * branch_a + gate_b * branch_b) @ w_out


def hierarchical_moe(x, w_router_g, b_router_g, w_router_e, b_router_e, w_gate, w_up, w_down):
    bsz, s, d = x.shape
    t = bsz * s
    xt = x.reshape(t, d)
    group_logits = (xt @ w_router_g).astype(jnp.float32) + b_router_g
    group_prob, group_idx = lax.top_k(jax.nn.softmax(group_logits, axis=-1), 1)
    expert_logits = (xt @ w_router_e).astype(jnp.float32).reshape(t, N_GROUPS, EXPERTS_PER_GROUP) + b_router_e
    in_group = jnp.take_along_axis(expert_logits, group_idx[:, :, None], axis=1)[:, 0]
    exp_prob, exp_idx = lax.top_k(jax.nn.softmax(in_group, axis=-1), TOP_K)
    exp_prob = exp_prob / jnp.sum(exp_prob, axis=-1, keepdims=True)
    route_w = (group_prob * exp_prob).reshape(-1)
    route_e = (group_idx * EXPERTS_PER_GROUP + exp_idx).reshape(-1)
    route_tok = jnp.repeat(jnp.arange(t, dtype=jnp.int32), TOP_K)
    n = t * TOP_K
    order = jnp.argsort(route_e)
    e_s = route_e[order]
    tok_s = route_tok[order]
    w_s = route_w[order]
    counts = jnp.bincount(route_e, length=N_EXPERTS)
    start = jnp.cumsum(counts) - counts
    padded = (counts + EXPERT_BLOCK - 1) // EXPERT_BLOCK * EXPERT_BLOCK
    pad_end = jnp.cumsum(padded)
    dest = pad_end[e_s] - padded[e_s] + jnp.arange(n, dtype=jnp.int32) - start[e_s]
    n_blocks = -(-n // EXPERT_BLOCK) + N_EXPERTS
    slot_tok = jnp.full((n_blocks * EXPERT_BLOCK,), t, jnp.int32).at[dest].set(tok_s)
    block_expert = jnp.minimum(
        jnp.searchsorted(pad_end, jnp.arange(n_blocks) * EXPERT_BLOCK, side='right'), N_EXPERTS - 1)
    x_pad = jnp.concatenate([xt, jnp.zeros((1, d), xt.dtype)], axis=0)
    x_blocks = x_pad[slot_tok].reshape(n_blocks, EXPERT_BLOCK, d)

    def expert_block(args):
        xb, e = args
        hidden = jax.nn.silu(xb @ w_gate[e]) * (xb @ w_up[e])
        return hidden @ w_down[e]

    y_slots = lax.map(expert_block, (x_blocks, block_expert)).reshape(-1, d)
    y_assign = y_slots[dest] * w_s[:, None].astype(x.dtype)
    y = jax.ops.segment_sum(y_assign, tok_s, num_segments=t)
    return y.reshape(bsz, s, d)


def setup_inputs(seed: int = 0) -> dict:
    key = jax.random.key(seed)
    ks = jax.random.split(key, 18)
    L, D = DEPTH, D_MODEL
    beta = DEEPNORM_BETA

    def nrm(k, shape, scale):
        return jax.random.normal(k, shape, jnp.float32) * scale

    col_scale = jnp.concatenate([
        jnp.ones((2 * SB_WIDTH,), jnp.float32), jnp.full((SB_WIDTH,), beta, jnp.float32),
        jnp.ones((2 * CONV_WIDTH,), jnp.float32), jnp.full((CONV_WIDTH,), beta, jnp.float32),
        jnp.ones((2 * D,), jnp.float32)])
    return {
        'x': nrm(ks[0], (BATCH, SEQ, D), 1.0),
        'w_in': nrm(ks[1], (L, D, PROJ_WIDTH), D ** -0.5) * col_scale,
        'gate_bias': nrm(ks[2], (L, 2 * D), 0.02),
        'conv_w': nrm(ks[3], (L, CONV_K, CONV_WIDTH), CONV_K ** -0.5),
        'w_branch_a': nrm(ks[4], (L, SB_WIDTH, D), SB_WIDTH ** -0.5 * beta),
        'w_branch_b': nrm(ks[5], (L, CONV_WIDTH, D), CONV_WIDTH ** -0.5 * beta),
        'w_out': nrm(ks[6], (L, D, D), D ** -0.5 * beta),
        'ln1_g': 1.0 + nrm(ks[7], (L, D), 0.02),
        'ln1_b': nrm(ks[8], (L, D), 0.02),
        'w_router_g': nrm(ks[9], (L, D, N_GROUPS), D ** -0.5),
        'b_router_g': nrm(ks[10], (L, N_GROUPS), 0.01),
        'w_router_e': nrm(ks[11], (L, D, N_EXPERTS), D ** -0.5),
        'b_router_e': nrm(ks[12], (L, N_GROUPS, EXPERTS_PER_GROUP), 0.01),
        'w_gate': nrm(ks[13], (L, N_EXPERTS, D, D_EXPERT), D ** -0.5 * beta),
        'w_up': nrm(ks[14], (L, N_EXPERTS, D, D_EXPERT), D ** -0.5 * beta),
        'w_down': nrm(ks[15], (L, N_EXPERTS, D_EXPERT, D), D_EXPERT ** -0.5 * beta),
        'ln2_g': 1.0 + nrm(ks[16], (L, D), 0.02),
        'ln2_b': nrm(ks[17], (L, D), 0.02),
    }


def reference(x, w_in, gate_bias, conv_w, w_branch_a, w_branch_b, w_out, ln1_g, ln1_b,
              w_router_g, b_router_g, w_router_e, b_router_e, w_gate, w_up, w_down,
              ln2_g, ln2_b):
    h = x
    for l in range(DEPTH):
        mixed = token_mixer(h, w_in[l], gate_bias[l], conv_w[l], w_branch_a[l], w_branch_b[l], w_out[l])
        h = layer_norm(DEEPNORM_ALPHA * h + mixed, ln1_g[l], ln1_b[l])
        ffn = hierarchical_moe(h, w_router_g[l], b_router_g[l], w_router_e[l], b_router_e[l],
                               w_gate[l], w_up[l], w_down[l])
        h = layer_norm(DEEPNORM_ALPHA * h + ffn, ln2_g[l], ln2_b[l])
    return h
```

```python
import functools

import jax
import jax.numpy as jnp
from jax import lax
from jax.experimental import pallas as pl
from jax.experimental.pallas import tpu as pltpu

F32 = jnp.float32
BF16 = jnp.bfloat16

SB_HEADS = 8
SB_HEAD_DIM = 64
SB_WIDTH = SB_HEADS * SB_HEAD_DIM
N_GROUPS = 4
EXPERTS_PER_GROUP = 8
N_EXPERTS = N_GROUPS * EXPERTS_PER_GROUP
LN_EPS = 1e-5

LANES = 128
HEAD_PAIR = LANES
PROJ_TILE = 256
ATTN_TILE = 256
ROW_TILE = 256
EXPERT_BLOCK = 256
ROUTER_ROWS = 8 + N_EXPERTS
TAIL_CUTOFF = 110.0
VMEM_LIMIT = 56 * 1024 * 1024


def _dot(a, b):
    return jnp.dot(a, b, preferred_element_type=F32)


def _dot_nt(a, b):
    return lax.dot_general(a, b, (((1,), (1,)), ((), ())), preferred_element_type=F32)


def _layer_norm(y, g, b):
    mu = jnp.mean(y, axis=-1, keepdims=True)
    d = y - mu
    var = jnp.mean(d * d, axis=-1, keepdims=True)
    return d * lax.rsqrt(var + LN_EPS) * g + b


def _proj_kernel(x_ref, w_ref, gbias_ref, convw_ref, wbb_ref, qkv_ref, ga_ref, gbb_ref, ubuf_ref,
                 *, tm, tiles_per_seq, d_model):
    i = pl.program_id(0)
    cw = SB_WIDTH
    xb = x_ref[...].astype(BF16)

    qkv = _dot(xb, w_ref[:, 0:3 * SB_WIDTH])
    qkv_ref[:, 0:SB_WIDTH] = (qkv[:, 0:SB_WIDTH] * (SB_HEAD_DIM ** -0.5)).astype(BF16)
    qkv_ref[:, SB_WIDTH:3 * SB_WIDTH] = qkv[:, SB_WIDTH:3 * SB_WIDTH].astype(BF16)

    c0 = 3 * SB_WIDTH
    conv_in = _dot(xb, w_ref[:, c0:c0 + 3 * cw])
    cb = conv_in[:, 0:cw]
    u = conv_in[:, cw:2 * cw] * conv_in[:, 2 * cw:3 * cw]

    @pl.when(i % tiles_per_seq == 0)
    def _():
        ubuf_ref[0:8, :] = jnp.zeros((8, cw), F32)

    ubuf_ref[8:tm + 8, :] = u
    cwt = convw_ref[...]
    y = (cwt[0:1, :] * ubuf_ref[pl.ds(6, tm), :] + cwt[1:2, :] * ubuf_ref[pl.ds(7, tm), :]
         + cwt[2:3, :] * u)
    ubuf_ref[0:8, :] = u[tm - 8:tm, :]
    branch_b = _dot((cb * y).astype(BF16), wbb_ref[...])

    g0 = c0 + 3 * cw
    gates = jax.nn.sigmoid(_dot(xb, w_ref[:, g0:g0 + 2 * d_model]) + gbias_ref[...])
    ga_ref[...] = gates[:, 0:d_model]
    gbb_ref[...] = gates[:, d_model:2 * d_model] * branch_b


def _proj_call(x2, w_in_b, gbias, conv_w, wbb_b, seq):
    t, d = x2.shape
    pw = w_in_b.shape[1]
    tm = PROJ_TILE
    cw = conv_w.shape[1]
    kern = functools.partial(_proj_kernel, tm=tm, tiles_per_seq=seq // tm, d_model=d)
    const = lambda i: (0, 0)
    return pl.pallas_call(
        kern,
        grid=(t // tm,),
        in_specs=[
            pl.BlockSpec((tm, d), lambda i: (i, 0)),
            pl.BlockSpec((d, pw), const),
            pl.BlockSpec((1, 2 * d), const),
            pl.BlockSpec((3, cw), const),
            pl.BlockSpec((cw, d), const),
        ],
        out_specs=[
            pl.BlockSpec((tm, 3 * SB_WIDTH), lambda i: (i, 0)),
            pl.BlockSpec((tm, d), lambda i: (i, 0)),
            pl.BlockSpec((tm, d), lambda i: (i, 0)),
        ],
        out_shape=[
            jax.ShapeDtypeStruct((t, 3 * SB_WIDTH), BF16),
            jax.ShapeDtypeStruct((t, d), F32),
            jax.ShapeDtypeStruct((t, d), F32),
        ],
        scratch_shapes=[pltpu.VMEM((tm + 8, cw), F32)],
        compiler_params=pltpu.CompilerParams(
            dimension_semantics=("arbitrary",), vmem_limit_bytes=VMEM_LIMIT),
        name="proj",
    )(x2, w_in_b, gbias, conv_w, wbb_b)


def _attn_kernel(q_ref, k_ref, v_ref, o_ref, acc_ref, c_ref, *, ts):
    qi = pl.program_id(2)
    q = q_ref[...]
    lane = lax.broadcasted_iota(jnp.int32, (ts, HEAD_PAIR), 1)
    row = lax.broadcasted_iota(jnp.int32, (ts, ts), 0)
    col = lax.broadcasted_iota(jnp.int32, (ts, ts), 1)
    later = (row > col).astype(BF16)
    causal = col < row
    reps = ts // LANES
    zero_b = jnp.zeros((ts, HEAD_PAIR), BF16)

    acc_ref[...] = jnp.zeros((ts, HEAD_PAIR), F32)

    for half in range(HEAD_PAIR // SB_HEAD_DIM):
        in_head = (lane >= half * SB_HEAD_DIM) & (lane < (half + 1) * SB_HEAD_DIM)

        def tile(j, diag):
            start = pl.multiple_of(j * ts, ts)
            kt = jnp.where(in_head, k_ref[pl.ds(start, ts), :], zero_b)
            vt = jnp.where(in_head, v_ref[pl.ds(start, ts), :], zero_b)
            z = _dot_nt(q, kt)
            sp = jnp.maximum(z, 0.0) + jnp.log(1.0 + jnp.exp(-jnp.abs(z)))
            if diag:
                sp = jnp.where(causal, sp, 0.0)
            hi = sp.astype(BF16)
            lo = (sp - hi.astype(F32)).astype(BF16)
            inner = _dot(hi, later) + _dot(lo, later)
            cvec = c_ref[...]
            carry = jnp.concatenate([cvec] * reps, axis=1)
            w = jnp.exp(z - sp - inner - carry)
            if diag:
                w = jnp.where(causal, w, 0.0)
            acc_ref[...] += _dot(w.astype(BF16), vt)
            total = inner[:, 0:1] + sp[:, 0:1]
            c_ref[...] = cvec + jnp.broadcast_to(total, (ts, LANES))

        c_ref[...] = jnp.zeros((ts, LANES), F32)
        tile(qi, True)

        def cond(carry):
            j, cmin = carry
            return jnp.logical_and(j >= 0, cmin < TAIL_CUTOFF)

        def body(carry):
            j, _ = carry
            tile(j, False)
            return j - 1, jnp.min(c_ref[...])

        lax.while_loop(cond, body, (qi - 1, jnp.min(c_ref[...])))

    o_ref[...] = acc_ref[...].astype(BF16)


def _attn_call(qkv3):
    b, s, _ = qkv3.shape
    ts = ATTN_TILE
    npair = SB_WIDTH // HEAD_PAIR
    kern = functools.partial(_attn_kernel, ts=ts)
    return pl.pallas_call(
        kern,
        grid=(b, npair, s // ts),
        in_specs=[
            pl.BlockSpec((None, ts, HEAD_PAIR), lambda bi, p, qi: (bi, qi, p)),
            pl.BlockSpec((None, s, HEAD_PAIR), lambda bi, p, qi: (bi, 0, npair + p)),
            pl.BlockSpec((None, s, HEAD_PAIR), lambda bi, p, qi: (bi, 0, 2 * npair + p)),
        ],
        out_specs=pl.BlockSpec((None, ts, HEAD_PAIR), lambda bi, p, qi: (bi, qi, p)),
        out_shape=jax.ShapeDtypeStruct((b, s, SB_WIDTH), BF16),
        scratch_shapes=[pltpu.VMEM((ts, HEAD_PAIR), F32), pltpu.VMEM((ts, LANES), F32)],
        compiler_params=pltpu.CompilerParams(
            dimension_semantics=("arbitrary", "arbitrary", "arbitrary"), vmem_limit_bytes=VMEM_LIMIT),
        name="attn",
    )(qkv3, qkv3, qkv3)


def _mix_kernel(attn_ref, ga_ref, gbb_ref, x_ref, wa_ref, wo_ref, g1_ref, b1_ref, wrt_ref, brt_ref,
                h1_ref, meta_ref, cnt_ref, *, tm, alpha):
    i = pl.program_id(0)
    reps = tm // LANES
    branch_a = _dot(attn_ref[...], wa_ref[...])
    merged = ga_ref[...] * branch_a + gbb_ref[...]
    mixed = _dot(merged.astype(BF16), wo_ref[...])
    h1 = _layer_norm(alpha * x_ref[...] + mixed, g1_ref[...], b1_ref[...])
    h1_ref[...] = h1

    lt = _dot_nt(wrt_ref[...], h1.astype(BF16)) + jnp.concatenate([brt_ref[...]] * reps, axis=1)
    r = [lt[k:k + 1, :] for k in range(N_GROUPS)]
    gmax = jnp.maximum(jnp.maximum(r[0], r[1]), jnp.maximum(r[2], r[3]))
    gidx = jnp.where(r[0] == gmax, 0, jnp.where(r[1] == gmax, 1, jnp.where(r[2] == gmax, 2, 3)))
    gsum = (jnp.exp(r[0] - gmax) + jnp.exp(r[1] - gmax)) + (jnp.exp(r[2] - gmax) + jnp.exp(r[3] - gmax))
    gprob = 1.0 / gsum
    epg = EXPERTS_PER_GROUP
    slabs = [lt[8 + g * epg:8 + (g + 1) * epg, :] for g in range(N_GROUPS)]
    el = jnp.where(gidx == 0, slabs[0], jnp.where(gidx == 1, slabs[1], jnp.where(gidx == 2, slabs[2], slabs[3])))
    r8 = lax.broadcasted_iota(jnp.int32, (epg, tm), 0)
    m1 = jnp.max(el, axis=0, keepdims=True)
    i1 = jnp.min(jnp.where(el == m1, r8, epg), axis=0, keepdims=True)
    el2 = jnp.where(r8 == i1, -jnp.inf, el)
    m2 = jnp.max(el2, axis=0, keepdims=True)
    i2 = jnp.min(jnp.where(el2 == m2, r8, epg), axis=0, keepdims=True)
    dlt = jnp.exp(m2 - m1)
    w1 = gprob / (1.0 + dlt)
    w2 = gprob * dlt / (1.0 + dlt)
    e1 = gidx * epg + i1
    e2 = gidx * epg + i2

    r32 = lax.broadcasted_iota(jnp.int32, (N_EXPERTS, tm), 0)
    is1 = r32 == e1
    is2 = r32 == e2
    onehot = jnp.where(is1 | is2, 1.0, 0.0).astype(BF16)
    trow = lax.broadcasted_iota(jnp.int32, (tm, tm), 0)
    tcol = lax.broadcasted_iota(jnp.int32, (tm, tm), 1)
    earlier = (trow < tcol).astype(BF16)

    @pl.when(i == 0)
    def _():
        cnt_ref[...] = jnp.zeros((N_EXPERTS, LANES), F32)

    before = _dot(onehot, earlier) + jnp.concatenate([cnt_ref[...]] * reps, axis=1)
    rank1 = jnp.sum(jnp.where(is1, before, 0.0), axis=0, keepdims=True)
    rank2 = jnp.sum(jnp.where(is2, before, 0.0), axis=0, keepdims=True)
    cnt_ref[...] += _dot(onehot, jnp.ones((tm, LANES), BF16))

    zrow = jnp.zeros((1, tm), F32)
    meta_ref[...] = jnp.concatenate(
        [e1.astype(F32), e2.astype(F32), w1, w2, rank1, rank2, zrow, zrow], axis=0)


def _mix_call(attn2, gate_a, gbb, x2, wa_b, wo_b, g1, b1, wrt_b, brt, alpha):
    t, d = x2.shape
    tm = PROJ_TILE
    kern = functools.partial(_mix_kernel, tm=tm, alpha=alpha)
    const = lambda i: (0, 0)
    rowblk = lambda i: (i, 0)
    return pl.pallas_call(
        kern,
        grid=(t // tm,),
        in_specs=[
            pl.BlockSpec((tm, SB_WIDTH), rowblk),
            pl.BlockSpec((tm, d), rowblk),
            pl.BlockSpec((tm, d), rowblk),
            pl.BlockSpec((tm, d), rowblk),
            pl.BlockSpec((SB_WIDTH, d), const),
            pl.BlockSpec((d, d), const),
            pl.BlockSpec((1, d), const),
            pl.BlockSpec((1, d), const),
            pl.BlockSpec((ROUTER_ROWS, d), const),
            pl.BlockSpec((ROUTER_ROWS, LANES), const),
        ],
        out_specs=[
            pl.BlockSpec((tm, d), rowblk),
            pl.BlockSpec((8, tm), lambda i: (0, i)),
            pl.BlockSpec((N_EXPERTS, LANES), const),
        ],
        out_shape=[
            jax.ShapeDtypeStruct((t, d), F32),
            jax.ShapeDtypeStruct((8, t), F32),
            jax.ShapeDtypeStruct((N_EXPERTS, LANES), F32),
        ],
        compiler_params=pltpu.CompilerParams(
            dimension_semantics=("arbitrary",), vmem_limit_bytes=VMEM_LIMIT),
        name="mix",
    )(attn2, gate_a, gbb, x2, wa_b, wo_b, g1, b1, wrt_b, brt)


def _row_copy(src_ref, src_row, dst_ref, dst_row, sem):
    return pltpu.make_async_copy(src_ref.at[pl.ds(src_row, 1), :], dst_ref.at[pl.ds(dst_row, 1), :], sem)


def _scatter_kernel(dest_ref, h1_ref, xs_in_ref, xs_ref, sem, *, tm):
    del xs_in_ref

    def start(r, c):
        for k in range(2):
            _row_copy(h1_ref, r, xs_ref, dest_ref[0, 0, k * tm + r], sem).start()
        return c

    lax.fori_loop(0, tm, start, 0, unroll=8)

    def wait(r, c):
        for k in range(2):
            _row_copy(h1_ref, r, xs_ref, dest_ref[0, 0, k * tm + r], sem).wait()
        return c

    lax.fori_loop(0, tm, wait, 0, unroll=8)


def _scatter_call(dest_blk, h1, xs_zero):
    t, d = h1.shape
    tm = ROW_TILE
    kern = functools.partial(_scatter_kernel, tm=tm)
    return pl.pallas_call(
        kern,
        grid=(t // tm,),
        in_specs=[
            pl.BlockSpec((1, 1, 2 * tm), lambda i: (i, 0, 0), memory_space=pltpu.SMEM),
            pl.BlockSpec((tm, d), lambda i: (i, 0)),
            pl.BlockSpec(memory_space=pl.ANY),
        ],
        out_specs=pl.BlockSpec(memory_space=pl.ANY),
        out_shape=jax.ShapeDtypeStruct(xs_zero.shape, F32),
        scratch_shapes=[pltpu.SemaphoreType.DMA(())],
        input_output_aliases={2: 0},
        compiler_params=pltpu.CompilerParams(
            dimension_semantics=("arbitrary",), vmem_limit_bytes=VMEM_LIMIT),
        name="scatter",
    )(dest_blk, h1, xs_zero)


def _moe_kernel(be_ref, nu_ref, x_ref, wg_ref, wu_ref, wd_ref, y_ref, wgb_ref, wub_ref, wdb_ref, prev_ref):
    i = pl.program_id(0)
    e = be_ref[i]

    @pl.when(i == 0)
    def _():
        prev_ref[0] = -1

    @pl.when(i < nu_ref[0])
    def _():
        @pl.when(e != prev_ref[0])
        def _():
            wgb_ref[...] = wg_ref[...].astype(BF16)
            wub_ref[...] = wu_ref[...].astype(BF16)
            wdb_ref[...] = wd_ref[...].astype(BF16)
            prev_ref[0] = e

        xb = x_ref[...].astype(BF16)
        g = _dot(xb, wgb_ref[...])
        u = _dot(xb, wub_ref[...])
        hidden = g * jax.nn.sigmoid(g) * u
        y_ref[...] = _dot(hidden.astype(BF16), wdb_ref[...])

    @pl.when(i >= nu_ref[0])
    def _():
        y_ref[...] = jnp.zeros(y_ref.shape, F32)


def _moe_call(block_expert, n_used, xs, w_gate, w_up, w_down):
    n_slots, d = xs.shape
    de = w_gate.shape[2]
    blk = EXPERT_BLOCK
    grid_spec = pltpu.PrefetchScalarGridSpec(
        num_scalar_prefetch=2,
        grid=(n_slots // blk,),
        in_specs=[
            pl.BlockSpec((blk, d), lambda i, be, nu: (i, 0)),
            pl.BlockSpec((None, d, de), lambda i, be, nu: (be[i], 0, 0)),
            pl.BlockSpec((None, d, de), lambda i, be, nu: (be[i], 0, 0)),
            pl.BlockSpec((None, de, d), lambda i, be, nu: (be[i], 0, 0)),
        ],
        out_specs=pl.BlockSpec((blk, d), lambda i, be, nu: (i, 0)),
        scratch_shapes=[
            pltpu.VMEM((d, de), BF16), pltpu.VMEM((d, de), BF16), pltpu.VMEM((de, d), BF16),
            pltpu.SMEM((1,), jnp.int32),
        ],
    )
    return pl.pallas_call(
        _moe_kernel,
        grid_spec=grid_spec,
        out_shape=jax.ShapeDtypeStruct((n_slots, d), F32),
        compiler_params=pltpu.CompilerParams(
            dimension_semantics=("arbitrary",), vmem_limit_bytes=VMEM_LIMIT),
        name="moe",
    )(block_expert, n_used, xs, w_gate, w_up, w_down)


def _combine_kernel(dest_ref, h1_ref, rw_ref, y_ref, g2_ref, b2_ref, o_ref, buf_ref, sem, *, tm, alpha):
    def start(r, c):
        for k in range(2):
            _row_copy(y_ref, dest_ref[0, 0, k * tm + r], buf_ref.at[k], r, sem).start()
        return c

    lax.fori_loop(0, tm, start, 0, unroll=8)

    def wait(r, c):
        for k in range(2):
            _row_copy(y_ref, dest_ref[0, 0, k * tm + r], buf_ref.at[k], r, sem).wait()
        return c

    lax.fori_loop(0, tm, wait, 0, unroll=8)

    rw = rw_ref[...]
    ffn = rw[:, 0:1] * buf_ref[0] + rw[:, 1:2] * buf_ref[1]
    o_ref[...] = _layer_norm(alpha * h1_ref[...] + ffn, g2_ref[...], b2_ref[...])


def _combine_call(dest_blk, h1, route_w, y_slots, g2, b2, alpha):
    t, d = h1.shape
    tm = ROW_TILE
    kern = functools.partial(_combine_kernel, tm=tm, alpha=alpha)
    const = lambda i: (0, 0)
    return pl.pallas_call(
        kern,
        grid=(t // tm,),
        in_specs=[
            pl.BlockSpec((1, 1, 2 * tm), lambda i: (i, 0, 0), memory_space=pltpu.SMEM),
            pl.BlockSpec((tm, d), lambda i: (i, 0)),
            pl.BlockSpec((tm, 2), lambda i: (i, 0)),
            pl.BlockSpec(memory_space=pl.ANY),
            pl.BlockSpec((1, d), const),
            pl.BlockSpec((1, d), const),
        ],
        out_specs=pl.BlockSpec((tm, d), lambda i: (i, 0)),
        out_shape=jax.ShapeDtypeStruct((t, d), F32),
        scratch_shapes=[pltpu.VMEM((2, tm, d), F32), pltpu.SemaphoreType.DMA(())],
        compiler_params=pltpu.CompilerParams(
            dimension_semantics=("arbitrary",), vmem_limit_bytes=VMEM_LIMIT),
        name="combine",
    )(dest_blk, h1, route_w, y_slots, g2, b2)


def _layer(h, w_in, gate_bias, conv_w, w_branch_a, w_branch_b, w_out, ln1_g, ln1_b,
           w_router_g, b_router_g, w_router_e, b_router_e, w_gate, w_up, w_down, ln2_g, ln2_b, alpha):
    bsz, seq, d = h.shape
    t = bsz * seq
    x2 = h.reshape(t, d)

    qkv, gate_a, gbb = _proj_call(x2, w_in.astype(BF16), gate_bias.reshape(1, 2 * d), conv_w,
                                  w_branch_b.astype(BF16), seq)
    attn = _attn_call(qkv.reshape(bsz, seq, 3 * SB_WIDTH)).reshape(t, SB_WIDTH)

    wrt = jnp.zeros((ROUTER_ROWS, d), F32).at[0:N_GROUPS].set(w_router_g.T).at[8:].set(w_router_e.T)
    brt = jnp.zeros((ROUTER_ROWS,), F32).at[0:N_GROUPS].set(b_router_g).at[8:].set(b_router_e.reshape(-1))
    brt = jnp.broadcast_to(brt[:, None], (ROUTER_ROWS, LANES))
    h1, meta, cnt = _mix_call(attn, gate_a, gbb, x2, w_branch_a.astype(BF16), w_out.astype(BF16),
                              ln1_g.reshape(1, d), ln1_b.reshape(1, d), wrt.astype(BF16), brt, alpha)

    blk = EXPERT_BLOCK
    counts = cnt[:, 0].astype(jnp.int32)
    padded = (counts + blk - 1) // blk * blk
    pad_end = jnp.cumsum(padded)
    pad_start = pad_end - padded
    expert = meta[0:2].astype(jnp.int32)
    rank = meta[4:6].astype(jnp.int32)
    onehot = expert[:, :, None] == jnp.arange(N_EXPERTS, dtype=jnp.int32)
    dest = jnp.sum(jnp.where(onehot, pad_start, 0), axis=-1) + rank
    n_blocks = (2 * t) // blk + N_EXPERTS
    block_start = jnp.arange(n_blocks, dtype=jnp.int32) * blk
    block_expert = jnp.minimum(jnp.sum(pad_end[None, :] <= block_start[:, None], axis=1), N_EXPERTS - 1)
    n_used = (pad_end[-1] // blk).reshape(1)
    tm = ROW_TILE
    dest_blk = dest.reshape(2, t // tm, tm).transpose(1, 0, 2).reshape(t // tm, 1, 2 * tm)

    xs = _scatter_call(dest_blk, h1, jnp.zeros((n_blocks * blk, d), F32))
    y_slots = _moe_call(block_expert.astype(jnp.int32), n_used.astype(jnp.int32), xs, w_gate, w_up, w_down)
    out = _combine_call(dest_blk, h1, meta[2:4].T, y_slots, ln2_g.reshape(1, d), ln2_b.reshape(1, d), alpha)
    return out.reshape(bsz, seq, d)


def kernel(x, w_in, gate_bias, conv_w, w_branch_a, w_branch_b, w_out, ln1_g, ln1_b, w_router_g, b_router_g,
           w_router_e, b_router_e, w_gate, w_up, w_down, ln2_g, ln2_b):
    depth = w_in.shape[0]
    alpha = (2.0 * depth) ** 0.25
    h = x
    for l in range(depth):
        h = _layer(h, w_in[l], gate_bias[l], conv_w[l], w_branch_a[l], w_branch_b[l], w_out[l], ln1_g[l],
                   ln1_b[l], w_router_g[l], b_router_g[l], w_router_e[l], b_router_e[l], w_gate[l], w_up[l],
                   w_down[l], ln2_g[l], ln2_b[l], alpha)
    return h
```

```python
import functools

import jax
import jax.numpy as jnp
from jax import lax
from jax.experimental import pallas as pl
from jax.experimental.pallas import tpu as pltpu

F32 = jnp.float32
BF16 = jnp.bfloat16

SB_HEADS = 8
SB_HEAD_DIM = 64
SB_WIDTH = SB_HEADS * SB_HEAD_DIM
N_GROUPS = 4
EXPERTS_PER_GROUP = 8
N_EXPERTS = N_GROUPS * EXPERTS_PER_GROUP
LN_EPS = 1e-5

LANES = 128
HEAD_PAIR = LANES
PROJ_TILE = 256
ATTN_TILE = 256
ROW_TILE = 256
EXPERT_BLOCK = 256
ROUTER_ROWS = 8 + N_EXPERTS
TAIL_CUTOFF = 110.0
MASKED_LOGIT = -1e30
LOG2E = 1.4426950408889634
VMEM_LIMIT = 56 * 1024 * 1024


def _dot(a, b):
    return jnp.dot(a, b, preferred_element_type=F32)


def _dot_nt(a, b):
    return lax.dot_general(a, b, (((1,), (1,)), ((), ())), preferred_element_type=F32)


def _layer_norm(y, g, b):
    mu = jnp.mean(y, axis=-1, keepdims=True)
    d = y - mu
    var = jnp.mean(d * d, axis=-1, keepdims=True)
    return d * lax.rsqrt(var + LN_EPS) * g + b


def _proj_kernel(x_ref, w_ref, gbias_ref, convw_ref, wbb_ref, qkv_ref, ga_ref, gbb_ref, ubuf_ref,
                 *, tm, tiles_per_seq, d_model):
    i = pl.program_id(0)
    cw = SB_WIDTH
    xb = x_ref[...].astype(BF16)

    qkv = _dot(xb, w_ref[:, 0:3 * SB_WIDTH])
    qkv_ref[:, 0:SB_WIDTH] = (qkv[:, 0:SB_WIDTH] * (SB_HEAD_DIM ** -0.5)).astype(BF16)
    qkv_ref[:, SB_WIDTH:3 * SB_WIDTH] = qkv[:, SB_WIDTH:3 * SB_WIDTH].astype(BF16)

    c0 = 3 * SB_WIDTH
    conv_in = _dot(xb, w_ref[:, c0:c0 + 3 * cw])
    cb = conv_in[:, 0:cw]
    u = conv_in[:, cw:2 * cw] * conv_in[:, 2 * cw:3 * cw]

    @pl.when(i % tiles_per_seq == 0)
    def _():
        ubuf_ref[0:8, :] = jnp.zeros((8, cw), F32)

    ubuf_ref[8:tm + 8, :] = u
    cwt = convw_ref[...]
    y = (cwt[0:1, :] * ubuf_ref[pl.ds(6, tm), :] + cwt[1:2, :] * ubuf_ref[pl.ds(7, tm), :]
         + cwt[2:3, :] * u)
    ubuf_ref[0:8, :] = u[tm - 8:tm, :]
    branch_b = _dot((cb * y).astype(BF16), wbb_ref[...])

    g0 = c0 + 3 * cw
    gates = jax.nn.sigmoid(_dot(xb, w_ref[:, g0:g0 + 2 * d_model]) + gbias_ref[...])
    ga_ref[...] = gates[:, 0:d_model]
    gbb_ref[...] = gates[:, d_model:2 * d_model] * branch_b


def _proj_call(x2, w_in_b, gbias, conv_w, wbb_b, seq):
    t, d = x2.shape
    pw = w_in_b.shape[1]
    tm = PROJ_TILE
    cw = conv_w.shape[1]
    kern = functools.partial(_proj_kernel, tm=tm, tiles_per_seq=seq // tm, d_model=d)
    const = lambda i: (0, 0)
    return pl.pallas_call(
        kern,
        grid=(t // tm,),
        in_specs=[
            pl.BlockSpec((tm, d), lambda i: (i, 0)),
            pl.BlockSpec((d, pw), const),
            pl.BlockSpec((1, 2 * d), const),
            pl.BlockSpec((3, cw), const),
            pl.BlockSpec((cw, d), const),
        ],
        out_specs=[
            pl.BlockSpec((tm, 3 * SB_WIDTH), lambda i: (i, 0)),
            pl.BlockSpec((tm, d), lambda i: (i, 0)),
            pl.BlockSpec((tm, d), lambda i: (i, 0)),
        ],
        out_shape=[
            jax.ShapeDtypeStruct((t, 3 * SB_WIDTH), BF16),
            jax.ShapeDtypeStruct((t, d), F32),
            jax.ShapeDtypeStruct((t, d), F32),
        ],
        scratch_shapes=[pltpu.VMEM((tm + 8, cw), F32)],
        compiler_params=pltpu.CompilerParams(
            dimension_semantics=("arbitrary",), vmem_limit_bytes=VMEM_LIMIT),
        name="proj",
    )(x2, w_in_b, gbias, conv_w, wbb_b)


def _attn_kernel(q_ref, k_ref, v_ref, o_ref, acc_ref, c_ref, *, ts):
    qi = pl.program_id(1)
    lane = lax.broadcasted_iota(jnp.int32, (ts, HEAD_PAIR), 1)
    row = lax.broadcasted_iota(jnp.int32, (ts, ts), 0)
    col = lax.broadcasted_iota(jnp.int32, (ts, ts), 1)
    later = (row > col).astype(BF16)
    later2 = jnp.concatenate([later, later], axis=0)
    ahead = col - row
    reps = ts // LANES
    zero_b = jnp.zeros((ts, HEAD_PAIR), BF16)
    halves = HEAD_PAIR // SB_HEAD_DIM
    in_head = [(lane >= h * SB_HEAD_DIM) & (lane < (h + 1) * SB_HEAD_DIM) for h in range(halves)]

    acc_ref[...] = jnp.zeros(acc_ref.shape, F32)
    c_ref[...] = jnp.zeros(c_ref.shape, F32)

    def tile(j):
        start = pl.multiple_of(j * ts, ts)
        bias = jnp.where(ahead < (qi - j) * ts, 0.0, MASKED_LOGIT)
        for p in range(SB_WIDTH // HEAD_PAIR):
            cols = slice(p * HEAD_PAIR, (p + 1) * HEAD_PAIR)
            q = q_ref[:, cols]
            kp = k_ref[pl.ds(start, ts), cols]
            vp = v_ref[pl.ds(start, ts), cols]
            out = None
            for h in range(halves):
                head = p * halves + h
                z = _dot_nt(q, jnp.where(in_head[h], kp, zero_b)) + bias
                sp = jnp.maximum(z, 0.0) + jnp.log(1.0 + jnp.exp2(jnp.abs(z) * (-LOG2E)))
                hi = sp.astype(BF16)
                lo = (sp - hi.astype(F32)).astype(BF16)
                inner = _dot(jnp.concatenate([hi, lo], axis=1), later2)
                cvec = c_ref[head]
                carry = jnp.concatenate([cvec] * reps, axis=1)
                w = jnp.exp(z - sp - inner - carry)
                pv = _dot(w.astype(BF16), jnp.where(in_head[h], vp, zero_b))
                out = pv if out is None else out + pv
                total = inner[:, 0:1] + sp[:, 0:1]
                c_ref[head] = cvec + jnp.broadcast_to(total, (ts, LANES))
            acc_ref[:, cols] += out

    def cond(carry):
        j, cmin = carry
        return jnp.logical_and(j >= 0, cmin < TAIL_CUTOFF)

    def body(carry):
        j, _ = carry
        tile(j)
        return j - 1, jnp.min(c_ref[...])

    lax.while_loop(cond, body, (qi, jnp.float32(0.0)))
    o_ref[...] = acc_ref[...].astype(BF16)


def _attn_call(qkv3):
    b, s, _ = qkv3.shape
    ts = ATTN_TILE
    kern = functools.partial(_attn_kernel, ts=ts)
    return pl.pallas_call(
        kern,
        grid=(b, s // ts),
        in_specs=[
            pl.BlockSpec((None, ts, SB_WIDTH), lambda bi, qi: (bi, qi, 0)),
            pl.BlockSpec((None, s, SB_WIDTH), lambda bi, qi: (bi, 0, 1)),
            pl.BlockSpec((None, s, SB_WIDTH), lambda bi, qi: (bi, 0, 2)),
        ],
        out_specs=pl.BlockSpec((None, ts, SB_WIDTH), lambda bi, qi: (bi, qi, 0)),
        out_shape=jax.ShapeDtypeStruct((b, s, SB_WIDTH), BF16),
        scratch_shapes=[pltpu.VMEM((ts, SB_WIDTH), F32), pltpu.VMEM((SB_HEADS, ts, LANES), F32)],
        compiler_params=pltpu.CompilerParams(
            dimension_semantics=("arbitrary", "arbitrary"), vmem_limit_bytes=VMEM_LIMIT),
        name="attn",
    )(qkv3, qkv3, qkv3)


def _mix_kernel(attn_ref, ga_ref, gbb_ref, x_ref, wa_ref, wo_ref, g1_ref, b1_ref, wrt_ref, brt_ref,
                h1_ref, meta_ref, cnt_ref, *, tm, alpha):
    i = pl.program_id(0)
    reps = tm // LANES
    branch_a = _dot(attn_ref[...], wa_ref[...])
    merged = ga_ref[...] * branch_a + gbb_ref[...]
    mixed = _dot(merged.astype(BF16), wo_ref[...])
    h1 = _layer_norm(alpha * x_ref[...] + mixed, g1_ref[...], b1_ref[...])
    h1_ref[...] = h1

    lt = _dot_nt(wrt_ref[...], h1.astype(BF16)) + jnp.concatenate([brt_ref[...]] * reps, axis=1)
    r = [lt[k:k + 1, :] for k in range(N_GROUPS)]
    gmax = jnp.maximum(jnp.maximum(r[0], r[1]), jnp.maximum(r[2], r[3]))
    gidx = jnp.where(r[0] == gmax, 0, jnp.where(r[1] == gmax, 1, jnp.where(r[2] == gmax, 2, 3)))
    gsum = (jnp.exp(r[0] - gmax) + jnp.exp(r[1] - gmax)) + (jnp.exp(r[2] - gmax) + jnp.exp(r[3] - gmax))
    gprob = 1.0 / gsum
    epg = EXPERTS_PER_GROUP
    slabs = [lt[8 + g * epg:8 + (g + 1) * epg, :] for g in range(N_GROUPS)]
    el = jnp.where(gidx == 0, slabs[0], jnp.where(gidx == 1, slabs[1], jnp.where(gidx == 2, slabs[2], slabs[3])))
    r8 = lax.broadcasted_iota(jnp.int32, (epg, tm), 0)
    m1 = jnp.max(el, axis=0, keepdims=True)
    i1 = jnp.min(jnp.where(el == m1, r8, epg), axis=0, keepdims=True)
    el2 = jnp.where(r8 == i1, -jnp.inf, el)
    m2 = jnp.max(el2, axis=0, keepdims=True)
    i2 = jnp.min(jnp.where(el2 == m2, r8, epg), axis=0, keepdims=True)
    dlt = jnp.exp(m2 - m1)
    w1 = gprob / (1.0 + dlt)
    w2 = gprob * dlt / (1.0 + dlt)
    e1 = gidx * epg + i1
    e2 = gidx * epg + i2

    r32 = lax.broadcasted_iota(jnp.int32, (N_EXPERTS, tm), 0)
    is1 = r32 == e1
    is2 = r32 == e2
    onehot = jnp.where(is1 | is2, 1.0, 0.0).astype(BF16)
    trow = lax.broadcasted_iota(jnp.int32, (tm, tm), 0)
    tcol = lax.broadcasted_iota(jnp.int32, (tm, tm), 1)
    earlier = (trow < tcol).astype(BF16)

    @pl.when(i == 0)
    def _():
        cnt_ref[...] = jnp.zeros((N_EXPERTS, LANES), F32)

    before = _dot(onehot, earlier) + jnp.concatenate([cnt_ref[...]] * reps, axis=1)
    rank1 = jnp.sum(jnp.where(is1, before, 0.0), axis=0, keepdims=True)
    rank2 = jnp.sum(jnp.where(is2, before, 0.0), axis=0, keepdims=True)
    cnt_ref[...] += _dot(onehot, jnp.ones((tm, LANES), BF16))

    zrow = jnp.zeros((1, tm), F32)
    meta_ref[...] = jnp.concatenate(
        [e1.astype(F32), e2.astype(F32), w1, w2, rank1, rank2, zrow, zrow], axis=0)


def _mix_call(attn2, gate_a, gbb, x2, wa_b, wo_b, g1, b1, wrt_b, brt, alpha):
    t, d = x2.shape
    tm = PROJ_TILE
    kern = functools.partial(_mix_kernel, tm=tm, alpha=alpha)
    const = lambda i: (0, 0)
    rowblk = lambda i: (i, 0)
    return pl.pallas_call(
        kern,
        grid=(t // tm,),
        in_specs=[
            pl.BlockSpec((tm, SB_WIDTH), rowblk),
            pl.BlockSpec((tm, d), rowblk),
            pl.BlockSpec((tm, d), rowblk),
            pl.BlockSpec((tm, d), rowblk),
            pl.BlockSpec((SB_WIDTH, d), const),
            pl.BlockSpec((d, d), const),
            pl.BlockSpec((1, d), const),
            pl.BlockSpec((1, d), const),
            pl.BlockSpec((ROUTER_ROWS, d), const),
            pl.BlockSpec((ROUTER_ROWS, LANES), const),
        ],
        out_specs=[
            pl.BlockSpec((tm, d), rowblk),
            pl.BlockSpec((8, tm), lambda i: (0, i)),
            pl.BlockSpec((N_EXPERTS, LANES), const),
        ],
        out_shape=[
            jax.ShapeDtypeStruct((t, d), F32),
            jax.ShapeDtypeStruct((8, t), F32),
            jax.ShapeDtypeStruct((N_EXPERTS, LANES), F32),
        ],
        compiler_params=pltpu.CompilerParams(
            dimension_semantics=("arbitrary",), vmem_limit_bytes=VMEM_LIMIT),
        name="mix",
    )(attn2, gate_a, gbb, x2, wa_b, wo_b, g1, b1, wrt_b, brt)


def _row_copy(src_ref, src_row, dst_ref, dst_row, sem):
    return pltpu.make_async_copy(src_ref.at[pl.ds(src_row, 1), :], dst_ref.at[pl.ds(dst_row, 1), :], sem)


def _scatter_kernel(dest_ref, h1_ref, xs_in_ref, xs_ref, sem, *, tm):
    del xs_in_ref

    def start(r, c):
        for k in range(2):
            _row_copy(h1_ref, r, xs_ref, dest_ref[0, 0, k * tm + r], sem).start()
        return c

    lax.fori_loop(0, tm, start, 0, unroll=8)

    def wait(r, c):
        for k in range(2):
            _row_copy(h1_ref, r, xs_ref, dest_ref[0, 0, k * tm + r], sem).wait()
        return c

    lax.fori_loop(0, tm, wait, 0, unroll=8)


def _scatter_call(dest_blk, h1, xs_zero):
    t, d = h1.shape
    tm = ROW_TILE
    kern = functools.partial(_scatter_kernel, tm=tm)
    return pl.pallas_call(
        kern,
        grid=(t // tm,),
        in_specs=[
            pl.BlockSpec((1, 1, 2 * tm), lambda i: (i, 0, 0), memory_space=pltpu.SMEM),
            pl.BlockSpec((tm, d), lambda i: (i, 0)),
            pl.BlockSpec(memory_space=pl.ANY),
        ],
        out_specs=pl.BlockSpec(memory_space=pl.ANY),
        out_shape=jax.ShapeDtypeStruct(xs_zero.shape, F32),
        scratch_shapes=[pltpu.SemaphoreType.DMA(())],
        input_output_aliases={2: 0},
        compiler_params=pltpu.CompilerParams(
            dimension_semantics=("arbitrary",), vmem_limit_bytes=VMEM_LIMIT),
        name="scatter",
    )(dest_blk, h1, xs_zero)


def _moe_kernel(be_ref, nu_ref, x_ref, wg_ref, wu_ref, wd_ref, y_ref, wgb_ref, wub_ref, wdb_ref, prev_ref):
    i = pl.program_id(0)
    e = be_ref[i]

    @pl.when(i == 0)
    def _():
        prev_ref[0] = -1

    @pl.when(i < nu_ref[0])
    def _():
        @pl.when(e != prev_ref[0])
        def _():
            wgb_ref[...] = wg_ref[...].astype(BF16)
            wub_ref[...] = wu_ref[...].astype(BF16)
            wdb_ref[...] = wd_ref[...].astype(BF16)
            prev_ref[0] = e

        xb = x_ref[...].astype(BF16)
        g = _dot(xb, wgb_ref[...])
        u = _dot(xb, wub_ref[...])
        hidden = g * jax.nn.sigmoid(g) * u
        y_ref[...] = _dot(hidden.astype(BF16), wdb_ref[...])

    @pl.when(i >= nu_ref[0])
    def _():
        y_ref[...] = jnp.zeros(y_ref.shape, F32)


def _moe_call(block_expert, n_used, xs, w_gate, w_up, w_down):
    n_slots, d = xs.shape
    de = w_gate.shape[2]
    blk = EXPERT_BLOCK
    grid_spec = pltpu.PrefetchScalarGridSpec(
        num_scalar_prefetch=2,
        grid=(n_slots // blk,),
        in_specs=[
            pl.BlockSpec((blk, d), lambda i, be, nu: (i, 0)),
            pl.BlockSpec((None, d, de), lambda i, be, nu: (be[i], 0, 0)),
            pl.BlockSpec((None, d, de), lambda i, be, nu: (be[i], 0, 0)),
            pl.BlockSpec((None, de, d), lambda i, be, nu: (be[i], 0, 0)),
        ],
        out_specs=pl.BlockSpec((blk, d), lambda i, be, nu: (i, 0)),
        scratch_shapes=[
            pltpu.VMEM((d, de), BF16), pltpu.VMEM((d, de), BF16), pltpu.VMEM((de, d), BF16),
            pltpu.SMEM((1,), jnp.int32),
        ],
    )
    return pl.pallas_call(
        _moe_kernel,
        grid_spec=grid_spec,
        out_shape=jax.ShapeDtypeStruct((n_slots, d), F32),
        compiler_params=pltpu.CompilerParams(
            dimension_semantics=("arbitrary",), vmem_limit_bytes=VMEM_LIMIT),
        name="moe",
    )(block_expert, n_used, xs, w_gate, w_up, w_down)


def _combine_kernel(dest_ref, h1_ref, rw_ref, y_ref, g2_ref, b2_ref, o_ref, buf_ref, sem, *, tm, alpha):
    def start(r, c):
        for k in range(2):
            _row_copy(y_ref, dest_ref[0, 0, k * tm + r], buf_ref.at[k], r, sem).start()
        return c

    lax.fori_loop(0, tm, start, 0, unroll=8)

    def wait(r, c):
        for k in range(2):
            _row_copy(y_ref, dest_ref[0, 0, k * tm + r], buf_ref.at[k], r, sem).wait()
        return c

    lax.fori_loop(0, tm, wait, 0, unroll=8)

    rw = rw_ref[...]
    ffn = rw[:, 0:1] * buf_ref[0] + rw[:, 1:2] * buf_ref[1]
    o_ref[...] = _layer_norm(alpha * h1_ref[...] + ffn, g2_ref[...], b2_ref[...])


def _combine_call(dest_blk, h1, route_w, y_slots, g2, b2, alpha):
    t, d = h1.shape
    tm = ROW_TILE
    kern = functools.partial(_combine_kernel, tm=tm, alpha=alpha)
    const = lambda i: (0, 0)
    return pl.pallas_call(
        kern,
        grid=(t // tm,),
        in_specs=[
            pl.BlockSpec((1, 1, 2 * tm), lambda i: (i, 0, 0), memory_space=pltpu.SMEM),
            pl.BlockSpec((tm, d), lambda i: (i, 0)),
            pl.BlockSpec((tm, 2), lambda i: (i, 0)),
            pl.BlockSpec(memory_space=pl.ANY),
            pl.BlockSpec((1, d), const),
            pl.BlockSpec((1, d), const),
        ],
        out_specs=pl.BlockSpec((tm, d), lambda i: (i, 0)),
        out_shape=jax.ShapeDtypeStruct((t, d), F32),
        scratch_shapes=[pltpu.VMEM((2, tm, d), F32), pltpu.SemaphoreType.DMA(())],
        compiler_params=pltpu.CompilerParams(
            dimension_semantics=("arbitrary",), vmem_limit_bytes=VMEM_LIMIT),
        name="combine",
    )(dest_blk, h1, route_w, y_slots, g2, b2)


def _layer(h, w_in, gate_bias, conv_w, w_branch_a, w_branch_b, w_out, ln1_g, ln1_b,
           w_router_g, b_router_g, w_router_e, b_router_e, w_gate, w_up, w_down, ln2_g, ln2_b, alpha):
    bsz, seq, d = h.shape
    t = bsz * seq
    x2 = h.reshape(t, d)

    qkv, gate_a, gbb = _proj_call(x2, w_in.astype(BF16), gate_bias.reshape(1, 2 * d), conv_w,
                                  w_branch_b.astype(BF16), seq)
    attn = _attn_call(qkv.reshape(bsz, seq, 3 * SB_WIDTH)).reshape(t, SB_WIDTH)

    wrt = jnp.zeros((ROUTER_ROWS, d), F32).at[0:N_GROUPS].set(w_router_g.T).at[8:].set(w_router_e.T)
    brt = jnp.zeros((ROUTER_ROWS,), F32).at[0:N_GROUPS].set(b_router_g).at[8:].set(b_router_e.reshape(-1))
    brt = jnp.broadcast_to(brt[:, None], (ROUTER_ROWS, LANES))
    h1, meta, cnt = _mix_call(attn, gate_a, gbb, x2, w_branch_a.astype(BF16), w_out.astype(BF16),
                              ln1_g.reshape(1, d), ln1_b.reshape(1, d), wrt.astype(BF16), brt, alpha)

    blk = EXPERT_BLOCK
    counts = cnt[:, 0].astype(jnp.int32)
    padded = (counts + blk - 1) // blk * blk
    pad_end = jnp.cumsum(padded)
    pad_start = pad_end - padded
    expert = meta[0:2].astype(jnp.int32)
    rank = meta[4:6].astype(jnp.int32)
    onehot = expert[:, :, None] == jnp.arange(N_EXPERTS, dtype=jnp.int32)
    dest = jnp.sum(jnp.where(onehot, pad_start, 0), axis=-1) + rank
    n_blocks = (2 * t) // blk + N_EXPERTS
    block_start = jnp.arange(n_blocks, dtype=jnp.int32) * blk
    block_expert = jnp.minimum(jnp.sum(pad_end[None, :] <= block_start[:, None], axis=1), N_EXPERTS - 1)
    n_used = (pad_end[-1] // blk).reshape(1)
    tm = ROW_TILE
    dest_blk = dest.reshape(2, t // tm, tm).transpose(1, 0, 2).reshape(t // tm, 1, 2 * tm)

    xs = _scatter_call(dest_blk, h1, jnp.zeros((n_blocks * blk, d), F32))
    y_slots = _moe_call(block_expert.astype(jnp.int32), n_used.astype(jnp.int32), xs, w_gate, w_up, w_down)
    out = _combine_call(dest_blk, h1, meta[2:4].T, y_slots, ln2_g.reshape(1, d), ln2_b.reshape(1, d), alpha)
    return out.reshape(bsz, seq, d)


def kernel(x, w_in, gate_bias, conv_w, w_branch_a, w_branch_b, w_out, ln1_g, ln1_b, w_router_g, b_router_g,
           w_router_e, b_router_e, w_gate, w_up, w_down, ln2_g, ln2_b):
    depth = w_in.shape[0]
    alpha = (2.0 * depth) ** 0.25
    h = x
    for l in range(depth):
        h = _layer(h, w_in[l], gate_bias[l], conv_w[l], w_branch_a[l], w_branch_b[l], w_out[l], ln1_g[l],
                   ln1_b[l], w_router_g[l], b_router_g[l], w_router_e[l], b_router_e[l], w_gate[l], w_up[l],
                   w_down[l], ln2_g[l], ln2_b[l], alpha)
    return h
```

```python
import functools

import jax
import jax.numpy as jnp
from jax import lax
from jax.experimental import pallas as pl
from jax.experimental.pallas import tpu as pltpu

F32 = jnp.float32
BF16 = jnp.bfloat16

SB_HEADS = 8
SB_HEAD_DIM = 64
SB_WIDTH = SB_HEADS * SB_HEAD_DIM
N_GROUPS = 4
EXPERTS_PER_GROUP = 8
N_EXPERTS = N_GROUPS * EXPERTS_PER_GROUP
LN_EPS = 1e-5

LANES = 128
HEAD_PAIR = LANES
PROJ_TILE = 256
ATTN_TILE = 256
ROW_TILE = 256
EXPERT_BLOCK = 256
SLOT_CHUNK = 4096
ROUTER_ROWS = 8 + N_EXPERTS
TAIL_CUTOFF = 110.0
MASKED_LOGIT = -1e30
LOG2E = 1.4426950408889634
VMEM_LIMIT = 56 * 1024 * 1024


def _dot(a, b):
    return jnp.dot(a, b, preferred_element_type=F32)


def _dot_nt(a, b):
    return lax.dot_general(a, b, (((1,), (1,)), ((), ())), preferred_element_type=F32)


def _layer_norm(y, g, b):
    mu = jnp.mean(y, axis=-1, keepdims=True)
    d = y - mu
    var = jnp.mean(d * d, axis=-1, keepdims=True)
    return d * lax.rsqrt(var + LN_EPS) * g + b


def _proj_kernel(x_ref, w_ref, gbias_ref, convw_ref, wbb_ref, qkv_ref, ga_ref, gbb_ref, ubuf_ref,
                 *, tm, tiles_per_seq, d_model):
    i = pl.program_id(0)
    cw = SB_WIDTH
    xb = x_ref[...].astype(BF16)

    qkv = _dot(xb, w_ref[:, 0:3 * SB_WIDTH])
    qkv_ref[:, 0:SB_WIDTH] = (qkv[:, 0:SB_WIDTH] * (SB_HEAD_DIM ** -0.5)).astype(BF16)
    qkv_ref[:, SB_WIDTH:3 * SB_WIDTH] = qkv[:, SB_WIDTH:3 * SB_WIDTH].astype(BF16)

    c0 = 3 * SB_WIDTH
    conv_in = _dot(xb, w_ref[:, c0:c0 + 3 * cw])
    cb = conv_in[:, 0:cw]
    u = conv_in[:, cw:2 * cw] * conv_in[:, 2 * cw:3 * cw]

    @pl.when(i % tiles_per_seq == 0)
    def _():
        ubuf_ref[0:8, :] = jnp.zeros((8, cw), F32)

    ubuf_ref[8:tm + 8, :] = u
    cwt = convw_ref[...]
    y = (cwt[0:1, :] * ubuf_ref[pl.ds(6, tm), :] + cwt[1:2, :] * ubuf_ref[pl.ds(7, tm), :]
         + cwt[2:3, :] * u)
    ubuf_ref[0:8, :] = u[tm - 8:tm, :]
    branch_b = _dot((cb * y).astype(BF16), wbb_ref[...])

    g0 = c0 + 3 * cw
    gates = jax.nn.sigmoid(_dot(xb, w_ref[:, g0:g0 + 2 * d_model]) + gbias_ref[...])
    ga_ref[...] = gates[:, 0:d_model]
    gbb_ref[...] = gates[:, d_model:2 * d_model] * branch_b


def _proj_call(x2, w_in_b, gbias, conv_w, wbb_b, seq):
    t, d = x2.shape
    pw = w_in_b.shape[1]
    tm = PROJ_TILE
    cw = conv_w.shape[1]
    kern = functools.partial(_proj_kernel, tm=tm, tiles_per_seq=seq // tm, d_model=d)
    const = lambda i: (0, 0)
    return pl.pallas_call(
        kern,
        grid=(t // tm,),
        in_specs=[
            pl.BlockSpec((tm, d), lambda i: (i, 0)),
            pl.BlockSpec((d, pw), const),
            pl.BlockSpec((1, 2 * d), const),
            pl.BlockSpec((3, cw), const),
            pl.BlockSpec((cw, d), const),
        ],
        out_specs=[
            pl.BlockSpec((tm, 3 * SB_WIDTH), lambda i: (i, 0)),
            pl.BlockSpec((tm, d), lambda i: (i, 0)),
            pl.BlockSpec((tm, d), lambda i: (i, 0)),
        ],
        out_shape=[
            jax.ShapeDtypeStruct((t, 3 * SB_WIDTH), BF16),
            jax.ShapeDtypeStruct((t, d), F32),
            jax.ShapeDtypeStruct((t, d), F32),
        ],
        scratch_shapes=[pltpu.VMEM((tm + 8, cw), F32)],
        compiler_params=pltpu.CompilerParams(
            dimension_semantics=("arbitrary",), vmem_limit_bytes=VMEM_LIMIT),
        name="proj",
    )(x2, w_in_b, gbias, conv_w, wbb_b)


def _attn_kernel(q_ref, k_ref, v_ref, o_ref, acc_ref, c_ref, *, ts):
    qi = pl.program_id(1)
    lane = lax.broadcasted_iota(jnp.int32, (ts, HEAD_PAIR), 1)
    row = lax.broadcasted_iota(jnp.int32, (ts, ts), 0)
    col = lax.broadcasted_iota(jnp.int32, (ts, ts), 1)
    later = (row > col).astype(BF16)
    later2 = jnp.concatenate([later, later], axis=0)
    ahead = col - row
    reps = ts // LANES
    zero_b = jnp.zeros((ts, HEAD_PAIR), BF16)
    halves = HEAD_PAIR // SB_HEAD_DIM
    in_head = [(lane >= h * SB_HEAD_DIM) & (lane < (h + 1) * SB_HEAD_DIM) for h in range(halves)]

    acc_ref[...] = jnp.zeros(acc_ref.shape, F32)
    c_ref[...] = jnp.zeros(c_ref.shape, F32)

    def tile(j):
        start = pl.multiple_of(j * ts, ts)
        bias = jnp.where(ahead < (qi - j) * ts, 0.0, MASKED_LOGIT)
        for p in range(SB_WIDTH // HEAD_PAIR):
            cols = slice(p * HEAD_PAIR, (p + 1) * HEAD_PAIR)
            q = q_ref[:, cols]
            kp = k_ref[pl.ds(start, ts), cols]
            vp = v_ref[pl.ds(start, ts), cols]
            out = None
            for h in range(halves):
                head = p * halves + h
                z = _dot_nt(q, jnp.where(in_head[h], kp, zero_b)) + bias
                sp = jnp.maximum(z, 0.0) + jnp.log(1.0 + jnp.exp2(jnp.abs(z) * (-LOG2E)))
                hi = sp.astype(BF16)
                lo = (sp - hi.astype(F32)).astype(BF16)
                inner = _dot(jnp.concatenate([hi, lo], axis=1), later2)
                cvec = c_ref[head]
                carry = jnp.concatenate([cvec] * reps, axis=1)
                w = jnp.exp(z - sp - inner - carry)
                pv = _dot(w.astype(BF16), jnp.where(in_head[h], vp, zero_b))
                out = pv if out is None else out + pv
                total = inner[:, 0:1] + sp[:, 0:1]
                c_ref[head] = cvec + jnp.broadcast_to(total, (ts, LANES))
            acc_ref[:, cols] += out

    def cond(carry):
        j, cmin = carry
        return jnp.logical_and(j >= 0, cmin < TAIL_CUTOFF)

    def body(carry):
        j, _ = carry
        tile(j)
        return j - 1, jnp.min(c_ref[...])

    lax.while_loop(cond, body, (qi, jnp.float32(0.0)))
    o_ref[...] = acc_ref[...].astype(BF16)


def _attn_call(qkv3):
    b, s, _ = qkv3.shape
    ts = ATTN_TILE
    kern = functools.partial(_attn_kernel, ts=ts)
    return pl.pallas_call(
        kern,
        grid=(b, s // ts),
        in_specs=[
            pl.BlockSpec((None, ts, SB_WIDTH), lambda bi, qi: (bi, qi, 0)),
            pl.BlockSpec((None, s, SB_WIDTH), lambda bi, qi: (bi, 0, 1)),
            pl.BlockSpec((None, s, SB_WIDTH), lambda bi, qi: (bi, 0, 2)),
        ],
        out_specs=pl.BlockSpec((None, ts, SB_WIDTH), lambda bi, qi: (bi, qi, 0)),
        out_shape=jax.ShapeDtypeStruct((b, s, SB_WIDTH), BF16),
        scratch_shapes=[pltpu.VMEM((ts, SB_WIDTH), F32), pltpu.VMEM((SB_HEADS, ts, LANES), F32)],
        compiler_params=pltpu.CompilerParams(
            dimension_semantics=("arbitrary", "arbitrary"), vmem_limit_bytes=VMEM_LIMIT),
        name="attn",
    )(qkv3, qkv3, qkv3)


def _mix_kernel(attn_ref, ga_ref, gbb_ref, x_ref, wa_ref, wo_ref, g1_ref, b1_ref, wrt_ref, brt_ref,
                h1_ref, meta_ref, cnt_ref, *, tm, alpha):
    i = pl.program_id(0)
    reps = tm // LANES
    branch_a = _dot(attn_ref[...], wa_ref[...])
    merged = ga_ref[...] * branch_a + gbb_ref[...]
    mixed = _dot(merged.astype(BF16), wo_ref[...])
    h1 = _layer_norm(alpha * x_ref[...] + mixed, g1_ref[...], b1_ref[...])
    h1_ref[...] = h1

    lt = _dot_nt(wrt_ref[...], h1.astype(BF16)) + jnp.concatenate([brt_ref[...]] * reps, axis=1)
    r = [lt[k:k + 1, :] for k in range(N_GROUPS)]
    gmax = jnp.maximum(jnp.maximum(r[0], r[1]), jnp.maximum(r[2], r[3]))
    gidx = jnp.where(r[0] == gmax, 0, jnp.where(r[1] == gmax, 1, jnp.where(r[2] == gmax, 2, 3)))
    gsum = (jnp.exp(r[0] - gmax) + jnp.exp(r[1] - gmax)) + (jnp.exp(r[2] - gmax) + jnp.exp(r[3] - gmax))
    gprob = 1.0 / gsum
    epg = EXPERTS_PER_GROUP
    slabs = [lt[8 + g * epg:8 + (g + 1) * epg, :] for g in range(N_GROUPS)]
    el = jnp.where(gidx == 0, slabs[0], jnp.where(gidx == 1, slabs[1], jnp.where(gidx == 2, slabs[2], slabs[3])))
    r8 = lax.broadcasted_iota(jnp.int32, (epg, tm), 0)
    m1 = jnp.max(el, axis=0, keepdims=True)
    i1 = jnp.min(jnp.where(el == m1, r8, epg), axis=0, keepdims=True)
    el2 = jnp.where(r8 == i1, -jnp.inf, el)
    m2 = jnp.max(el2, axis=0, keepdims=True)
    i2 = jnp.min(jnp.where(el2 == m2, r8, epg), axis=0, keepdims=True)
    dlt = jnp.exp(m2 - m1)
    w1 = gprob / (1.0 + dlt)
    w2 = gprob * dlt / (1.0 + dlt)
    e1 = gidx * epg + i1
    e2 = gidx * epg + i2

    r32 = lax.broadcasted_iota(jnp.int32, (N_EXPERTS, tm), 0)
    is1 = r32 == e1
    is2 = r32 == e2
    onehot = jnp.where(is1 | is2, 1.0, 0.0).astype(BF16)
    trow = lax.broadcasted_iota(jnp.int32, (tm, tm), 0)
    tcol = lax.broadcasted_iota(jnp.int32, (tm, tm), 1)
    earlier = (trow < tcol).astype(BF16)

    @pl.when(i == 0)
    def _():
        cnt_ref[...] = jnp.zeros((N_EXPERTS, LANES), F32)

    before = _dot(onehot, earlier) + jnp.concatenate([cnt_ref[...]] * reps, axis=1)
    rank1 = jnp.sum(jnp.where(is1, before, 0.0), axis=0, keepdims=True)
    rank2 = jnp.sum(jnp.where(is2, before, 0.0), axis=0, keepdims=True)
    cnt_ref[...] += _dot(onehot, jnp.ones((tm, LANES), BF16))

    zrow = jnp.zeros((1, tm), F32)
    meta_ref[...] = jnp.concatenate(
        [e1.astype(F32), e2.astype(F32), w1, w2, rank1, rank2, zrow, zrow], axis=0)


def _mix_call(attn2, gate_a, gbb, x2, wa_b, wo_b, g1, b1, wrt_b, brt, alpha):
    t, d = x2.shape
    tm = PROJ_TILE
    kern = functools.partial(_mix_kernel, tm=tm, alpha=alpha)
    const = lambda i: (0, 0)
    rowblk = lambda i: (i, 0)
    return pl.pallas_call(
        kern,
        grid=(t // tm,),
        in_specs=[
            pl.BlockSpec((tm, SB_WIDTH), rowblk),
            pl.BlockSpec((tm, d), rowblk),
            pl.BlockSpec((tm, d), rowblk),
            pl.BlockSpec((tm, d), rowblk),
            pl.BlockSpec((SB_WIDTH, d), const),
            pl.BlockSpec((d, d), const),
            pl.BlockSpec((1, d), const),
            pl.BlockSpec((1, d), const),
            pl.BlockSpec((ROUTER_ROWS, d), const),
            pl.BlockSpec((ROUTER_ROWS, LANES), const),
        ],
        out_specs=[
            pl.BlockSpec((tm, d), rowblk),
            pl.BlockSpec((8, tm), lambda i: (0, i)),
            pl.BlockSpec((N_EXPERTS, LANES), const),
        ],
        out_shape=[
            jax.ShapeDtypeStruct((t, d), F32),
            jax.ShapeDtypeStruct((8, t), F32),
            jax.ShapeDtypeStruct((N_EXPERTS, LANES), F32),
        ],
        compiler_params=pltpu.CompilerParams(
            dimension_semantics=("arbitrary",), vmem_limit_bytes=VMEM_LIMIT),
        name="mix",
    )(attn2, gate_a, gbb, x2, wa_b, wo_b, g1, b1, wrt_b, brt)


def _row_copy(src_ref, src_row, dst_ref, dst_row, sem):
    return pltpu.make_async_copy(src_ref.at[pl.ds(src_row, 1), :], dst_ref.at[pl.ds(dst_row, 1), :], sem)


def _slots_kernel(dest_ref, fill_ref, tok_ref, *, chunk, chunks_per_k, blk):
    i = pl.program_id(0)
    shift = blk.bit_length() - 1

    @pl.when(i == 0)
    def _():
        for e in range(N_EXPERTS):
            def fill(s, c):
                tok_ref[s >> shift, s & (blk - 1)] = 0
                return c
            lax.fori_loop(fill_ref[0, e], fill_ref[1, e], fill, 0)

    base = (i % chunks_per_k) * chunk

    def put(r, c):
        d = dest_ref[0, 0, r]
        tok_ref[d >> shift, d & (blk - 1)] = base + r
        return c

    lax.fori_loop(0, chunk, put, 0, unroll=8)


def _slots_call(dest, fill, n_blocks):
    two, t = dest.shape
    chunk = min(t, SLOT_CHUNK)
    blk = EXPERT_BLOCK
    kern = functools.partial(_slots_kernel, chunk=chunk, chunks_per_k=t // chunk, blk=blk)
    return pl.pallas_call(
        kern,
        grid=(two * t // chunk,),
        in_specs=[
            pl.BlockSpec((1, 1, chunk), lambda i: (i, 0, 0), memory_space=pltpu.SMEM),
            pl.BlockSpec(memory_space=pltpu.SMEM),
        ],
        out_specs=pl.BlockSpec(memory_space=pltpu.SMEM),
        out_shape=jax.ShapeDtypeStruct((n_blocks, blk), jnp.int32),
        compiler_params=pltpu.CompilerParams(dimension_semantics=("arbitrary",)),
        name="slots",
    )(dest.reshape(two * t // chunk, 1, chunk), fill)


def _moe_kernel(be_ref, nu_ref, idx0_ref, idxn_ref, h1_ref, wg_ref, wu_ref, wd_ref, y_ref,
                xbuf_ref, sem, wgb_ref, wub_ref, wdb_ref, prev_ref, *, blk):
    i = pl.program_id(0)
    n_used = nu_ref[0]
    e = be_ref[i]
    cur = i % 2

    def gather(idx_ref, slot):
        for r in range(blk):
            _row_copy(h1_ref, idx_ref[0, 0, r], xbuf_ref.at[slot], r, sem.at[slot]).start()

    def drain(slot):
        for r in range(blk):
            _row_copy(h1_ref, 0, xbuf_ref.at[slot], r, sem.at[slot]).wait()

    @pl.when(i == 0)
    def _():
        prev_ref[0] = -1

    @pl.when(jnp.logical_and(i == 0, n_used > 0))
    def _():
        gather(idx0_ref, 0)

    @pl.when(i < n_used)
    def _():
        @pl.when(e != prev_ref[0])
        def _():
            wgb_ref[...] = wg_ref[...].astype(BF16)
            wub_ref[...] = wu_ref[...].astype(BF16)
            wdb_ref[...] = wd_ref[...].astype(BF16)
            prev_ref[0] = e

        drain(cur)
        xb = xbuf_ref[cur].astype(BF16)
        gather(idxn_ref, 1 - cur)
        g = _dot(xb, wgb_ref[...])
        u = _dot(xb, wub_ref[...])
        hidden = g * jax.nn.sigmoid(g) * u
        y_ref[...] = _dot(hidden.astype(BF16), wdb_ref[...])

    @pl.when(i == n_used - 1)
    def _():
        drain(1 - cur)

    @pl.when(i >= n_used)
    def _():
        y_ref[...] = jnp.zeros(y_ref.shape, F32)


def _moe_call(block_expert, n_used, slot_tok, h1, w_gate, w_up, w_down):
    n_blocks, blk = slot_tok.shape
    t, d = h1.shape
    de = w_gate.shape[2]
    idx = slot_tok.reshape(n_blocks, 1, blk)
    grid_spec = pltpu.PrefetchScalarGridSpec(
        num_scalar_prefetch=2,
        grid=(n_blocks,),
        in_specs=[
            pl.BlockSpec((1, 1, blk), lambda i, be, nu: (i, 0, 0), memory_space=pltpu.SMEM),
            pl.BlockSpec((1, 1, blk), lambda i, be, nu: (jnp.maximum(jnp.minimum(i + 1, nu[0] - 1), 0), 0, 0),
                         memory_space=pltpu.SMEM),
            pl.BlockSpec(memory_space=pl.ANY),
            pl.BlockSpec((None, d, de), lambda i, be, nu: (be[i], 0, 0)),
            pl.BlockSpec((None, d, de), lambda i, be, nu: (be[i], 0, 0)),
            pl.BlockSpec((None, de, d), lambda i, be, nu: (be[i], 0, 0)),
        ],
        out_specs=pl.BlockSpec((blk, d), lambda i, be, nu: (i, 0)),
        scratch_shapes=[
            pltpu.VMEM((2, blk, d), F32), pltpu.SemaphoreType.DMA((2,)),
            pltpu.VMEM((d, de), BF16), pltpu.VMEM((d, de), BF16), pltpu.VMEM((de, d), BF16),
            pltpu.SMEM((1,), jnp.int32),
        ],
    )
    return pl.pallas_call(
        functools.partial(_moe_kernel, blk=blk),
        grid_spec=grid_spec,
        out_shape=jax.ShapeDtypeStruct((n_blocks * blk, d), F32),
        compiler_params=pltpu.CompilerParams(
            dimension_semantics=("arbitrary",), vmem_limit_bytes=VMEM_LIMIT),
        name="moe",
    )(block_expert, n_used, idx, idx, h1, w_gate, w_up, w_down)


def _combine_kernel(dest0_ref, destn_ref, h1_ref, rw_ref, y_ref, g2_ref, b2_ref, o_ref, buf_ref, sem,
                    *, tm, alpha):
    i = pl.program_id(0)
    last = pl.num_programs(0) - 1
    cur = i % 2

    def gather(dest_ref, slot):
        for r in range(tm):
            for k in range(2):
                _row_copy(y_ref, dest_ref[0, 0, k * tm + r], buf_ref.at[slot, k], r, sem.at[slot]).start(priority=k)

    def drain(slot):
        for r in range(tm):
            for k in range(2):
                _row_copy(y_ref, 0, buf_ref.at[slot, k], r, sem.at[slot]).wait()

    @pl.when(i == 0)
    def _():
        gather(dest0_ref, 0)

    gather(destn_ref, 1 - cur)
    drain(cur)
    rw = rw_ref[...]
    ffn = rw[:, 0:1] * buf_ref[cur, 0] + rw[:, 1:2] * buf_ref[cur, 1]
    o_ref[...] = _layer_norm(alpha * h1_ref[...] + ffn, g2_ref[...], b2_ref[...])

    @pl.when(i == last)
    def _():
        drain(1 - cur)


def _combine_call(dest_blk, h1, route_w, y_slots, g2, b2, alpha):
    t, d = h1.shape
    tm = ROW_TILE
    nt = t // tm
    kern = functools.partial(_combine_kernel, tm=tm, alpha=alpha)
    const = lambda i: (0, 0)
    return pl.pallas_call(
        kern,
        grid=(nt,),
        in_specs=[
            pl.BlockSpec((1, 1, 2 * tm), lambda i: (i, 0, 0), memory_space=pltpu.SMEM),
            pl.BlockSpec((1, 1, 2 * tm), lambda i: (jnp.minimum(i + 1, nt - 1), 0, 0), memory_space=pltpu.SMEM),
            pl.BlockSpec((tm, d), lambda i: (i, 0)),
            pl.BlockSpec((tm, 2), lambda i: (i, 0)),
            pl.BlockSpec(memory_space=pl.ANY),
            pl.BlockSpec((1, d), const),
            pl.BlockSpec((1, d), const),
        ],
        out_specs=pl.BlockSpec((tm, d), lambda i: (i, 0)),
        out_shape=jax.ShapeDtypeStruct((t, d), F32),
        scratch_shapes=[pltpu.VMEM((2, 2, tm, d), F32), pltpu.SemaphoreType.DMA((2,))],
        compiler_params=pltpu.CompilerParams(
            dimension_semantics=("arbitrary",), vmem_limit_bytes=VMEM_LIMIT),
        name="combine",
    )(dest_blk, dest_blk, h1, route_w, y_slots, g2, b2)


def _layer(h, w_in, gate_bias, conv_w, w_branch_a, w_branch_b, w_out, ln1_g, ln1_b,
           w_router_g, b_router_g, w_router_e, b_router_e, w_gate, w_up, w_down, ln2_g, ln2_b, alpha):
    bsz, seq, d = h.shape
    t = bsz * seq
    x2 = h.reshape(t, d)

    qkv, gate_a, gbb = _proj_call(x2, w_in.astype(BF16), gate_bias.reshape(1, 2 * d), conv_w,
                                  w_branch_b.astype(BF16), seq)
    attn = _attn_call(qkv.reshape(bsz, seq, 3 * SB_WIDTH)).reshape(t, SB_WIDTH)

    wrt = jnp.zeros((ROUTER_ROWS, d), F32).at[0:N_GROUPS].set(w_router_g.T).at[8:].set(w_router_e.T)
    brt = jnp.zeros((ROUTER_ROWS,), F32).at[0:N_GROUPS].set(b_router_g).at[8:].set(b_router_e.reshape(-1))
    brt = jnp.broadcast_to(brt[:, None], (ROUTER_ROWS, LANES))
    h1, meta, cnt = _mix_call(attn, gate_a, gbb, x2, w_branch_a.astype(BF16), w_out.astype(BF16),
                              ln1_g.reshape(1, d), ln1_b.reshape(1, d), wrt.astype(BF16), brt, alpha)

    blk = EXPERT_BLOCK
    counts = cnt[:, 0].astype(jnp.int32)
    padded = (counts + blk - 1) // blk * blk
    pad_end = jnp.cumsum(padded)
    pad_start = pad_end - padded
    expert = meta[0:2].astype(jnp.int32)
    rank = meta[4:6].astype(jnp.int32)
    onehot = expert[:, :, None] == jnp.arange(N_EXPERTS, dtype=jnp.int32)
    dest = jnp.sum(jnp.where(onehot, pad_start, 0), axis=-1) + rank
    n_blocks = (2 * t) // blk + N_EXPERTS
    block_start = jnp.arange(n_blocks, dtype=jnp.int32) * blk
    block_expert = jnp.minimum(jnp.sum(pad_end[None, :] <= block_start[:, None], axis=1), N_EXPERTS - 1)
    n_used = (pad_end[-1] // blk).reshape(1)
    tm = ROW_TILE
    dest_blk = dest.reshape(2, t // tm, tm).transpose(1, 0, 2).reshape(t // tm, 1, 2 * tm)

    fill_hi = pad_end.at[N_EXPERTS - 1].set(n_blocks * blk)
    slot_tok = _slots_call(dest, jnp.stack([pad_start + counts, fill_hi]).astype(jnp.int32), n_blocks)
    y_slots = _moe_call(block_expert.astype(jnp.int32), n_used.astype(jnp.int32), slot_tok, h1,
                        w_gate, w_up, w_down)
    out = _combine_call(dest_blk, h1, meta[2:4].T, y_slots, ln2_g.reshape(1, d), ln2_b.reshape(1, d), alpha)
    return out.reshape(bsz, seq, d)


def kernel(x, w_in, gate_bias, conv_w, w_branch_a, w_branch_b, w_out, ln1_g, ln1_b, w_router_g, b_router_g,
           w_router_e, b_router_e, w_gate, w_up, w_down, ln2_g, ln2_b):
    depth = w_in.shape[0]
    alpha = (2.0 * depth) ** 0.25
    h = x
    for l in range(depth):
        h = _layer(h, w_in[l], gate_bias[l], conv_w[l], w_branch_a[l], w_branch_b[l], w_out[l], ln1_g[l],
                   ln1_b[l], w_router_g[l], b_router_g[l], w_router_e[l], b_router_e[l], w_gate[l], w_up[l],
                   w_down[l], ln2_g[l], ln2_b[l], alpha)
    return h
```

```python
import functools

import jax
import jax.numpy as jnp
from jax import lax
from jax.experimental import pallas as pl
from jax.experimental.pallas import tpu as pltpu

F32 = jnp.float32
BF16 = jnp.bfloat16

SB_HEADS = 8
SB_HEAD_DIM = 64
SB_WIDTH = SB_HEADS * SB_HEAD_DIM
N_GROUPS = 4
EXPERTS_PER_GROUP = 8
N_EXPERTS = N_GROUPS * EXPERTS_PER_GROUP
LN_EPS = 1e-5

LANES = 128
SUBLANES = 8
HEAD_PAIR = LANES
PROJ_TILE = 256
ATTN_TILE = 256
ROW_TILE = 256
EXPERT_BLOCK = 256
SLOT_CHUNK = 4096
ROUTER_ROWS = 8 + N_EXPERTS
TAIL_CUTOFF = 110.0
MASKED_LOGIT = -1e30
LOG2E = 1.4426950408889634
VMEM_LIMIT = 56 * 1024 * 1024


def _dot(a, b):
    return jnp.dot(a, b, preferred_element_type=F32)


def _dot_nt(a, b):
    return lax.dot_general(a, b, (((1,), (1,)), ((), ())), preferred_element_type=F32)


def _layer_norm(y, g, b):
    mu = jnp.mean(y, axis=-1, keepdims=True)
    d = y - mu
    var = jnp.mean(d * d, axis=-1, keepdims=True)
    return d * lax.rsqrt(var + LN_EPS) * g + b


def _proj_kernel(x_ref, w_ref, gbias_ref, convw_ref, wbb_ref, qkv_ref, ga_ref, gbb_ref, ubuf_ref,
                 *, tm, tiles_per_seq, d_model):
    i = pl.program_id(0)
    cw = SB_WIDTH
    xb = x_ref[...].astype(BF16)

    qkv = _dot(xb, w_ref[:, 0:3 * SB_WIDTH])
    qkv_ref[:, 0:SB_WIDTH] = (qkv[:, 0:SB_WIDTH] * (SB_HEAD_DIM ** -0.5)).astype(BF16)
    qkv_ref[:, SB_WIDTH:3 * SB_WIDTH] = qkv[:, SB_WIDTH:3 * SB_WIDTH].astype(BF16)

    c0 = 3 * SB_WIDTH
    conv_in = _dot(xb, w_ref[:, c0:c0 + 3 * cw])
    cb = conv_in[:, 0:cw]
    u = conv_in[:, cw:2 * cw] * conv_in[:, 2 * cw:3 * cw]

    @pl.when(i % tiles_per_seq == 0)
    def _():
        ubuf_ref[0:8, :] = jnp.zeros((8, cw), F32)

    ubuf_ref[8:tm + 8, :] = u
    cwt = convw_ref[...]
    y = (cwt[0:1, :] * ubuf_ref[pl.ds(6, tm), :] + cwt[1:2, :] * ubuf_ref[pl.ds(7, tm), :]
         + cwt[2:3, :] * u)
    ubuf_ref[0:8, :] = u[tm - 8:tm, :]
    branch_b = _dot((cb * y).astype(BF16), wbb_ref[...])

    g0 = c0 + 3 * cw
    gates = jax.nn.sigmoid(_dot(xb, w_ref[:, g0:g0 + 2 * d_model]) + gbias_ref[...])
    ga_ref[...] = gates[:, 0:d_model]
    gbb_ref[...] = gates[:, d_model:2 * d_model] * branch_b


def _proj_call(x2, w_in_b, gbias, conv_w, wbb_b, seq):
    t, d = x2.shape
    pw = w_in_b.shape[1]
    tm = PROJ_TILE
    cw = conv_w.shape[1]
    kern = functools.partial(_proj_kernel, tm=tm, tiles_per_seq=seq // tm, d_model=d)
    const = lambda i: (0, 0)
    return pl.pallas_call(
        kern,
        grid=(t // tm,),
        in_specs=[
            pl.BlockSpec((tm, d), lambda i: (i, 0)),
            pl.BlockSpec((d, pw), const),
            pl.BlockSpec((1, 2 * d), const),
            pl.BlockSpec((3, cw), const),
            pl.BlockSpec((cw, d), const),
        ],
        out_specs=[
            pl.BlockSpec((tm, 3 * SB_WIDTH), lambda i: (i, 0)),
            pl.BlockSpec((tm, d), lambda i: (i, 0)),
            pl.BlockSpec((tm, d), lambda i: (i, 0)),
        ],
        out_shape=[
            jax.ShapeDtypeStruct((t, 3 * SB_WIDTH), BF16),
            jax.ShapeDtypeStruct((t, d), F32),
            jax.ShapeDtypeStruct((t, d), F32),
        ],
        scratch_shapes=[pltpu.VMEM((tm + 8, cw), F32)],
        compiler_params=pltpu.CompilerParams(
            dimension_semantics=("arbitrary",), vmem_limit_bytes=VMEM_LIMIT),
        name="proj",
    )(x2, w_in_b, gbias, conv_w, wbb_b)


def _attn_kernel(q_ref, k_ref, v_ref, o_ref, acc_ref, c_ref, *, ts):
    qi = pl.program_id(1)
    lane = lax.broadcasted_iota(jnp.int32, (ts, HEAD_PAIR), 1)
    row = lax.broadcasted_iota(jnp.int32, (ts, ts), 0)
    col = lax.broadcasted_iota(jnp.int32, (ts, ts), 1)
    later = (row > col).astype(BF16)
    later2 = jnp.concatenate([later, later], axis=0)
    ahead = col - row
    reps = ts // LANES
    zero_b = jnp.zeros((ts, HEAD_PAIR), BF16)
    halves = HEAD_PAIR // SB_HEAD_DIM
    in_head = [(lane >= h * SB_HEAD_DIM) & (lane < (h + 1) * SB_HEAD_DIM) for h in range(halves)]

    acc_ref[...] = jnp.zeros(acc_ref.shape, F32)
    c_ref[...] = jnp.zeros(c_ref.shape, F32)

    def tile(j):
        start = pl.multiple_of(j * ts, ts)
        bias = jnp.where(ahead < (qi - j) * ts, 0.0, MASKED_LOGIT)
        for p in range(SB_WIDTH // HEAD_PAIR):
            cols = slice(p * HEAD_PAIR, (p + 1) * HEAD_PAIR)
            q = q_ref[:, cols]
            kp = k_ref[pl.ds(start, ts), cols]
            vp = v_ref[pl.ds(start, ts), cols]
            out = None
            for h in range(halves):
                head = p * halves + h
                z = _dot_nt(q, jnp.where(in_head[h], kp, zero_b)) + bias
                sp = jnp.maximum(z, 0.0) + jnp.log(1.0 + jnp.exp2(jnp.abs(z) * (-LOG2E)))
                hi = sp.astype(BF16)
                lo = (sp - hi.astype(F32)).astype(BF16)
                inner = _dot(jnp.concatenate([hi, lo], axis=1), later2)
                cvec = c_ref[head]
                carry = jnp.concatenate([cvec] * reps, axis=1)
                w = jnp.exp(z - sp - inner - carry)
                pv = _dot(w.astype(BF16), jnp.where(in_head[h], vp, zero_b))
                out = pv if out is None else out + pv
                total = inner[:, 0:1] + sp[:, 0:1]
                c_ref[head] = cvec + jnp.broadcast_to(total, (ts, LANES))
            acc_ref[:, cols] += out

    def cond(carry):
        j, cmin = carry
        return jnp.logical_and(j >= 0, cmin < TAIL_CUTOFF)

    def body(carry):
        j, _ = carry
        tile(j)
        return j - 1, jnp.min(c_ref[...])

    lax.while_loop(cond, body, (qi, jnp.float32(0.0)))
    o_ref[...] = acc_ref[...].astype(BF16)


def _attn_call(qkv3):
    b, s, _ = qkv3.shape
    ts = ATTN_TILE
    kern = functools.partial(_attn_kernel, ts=ts)
    return pl.pallas_call(
        kern,
        grid=(b, s // ts),
        in_specs=[
            pl.BlockSpec((None, ts, SB_WIDTH), lambda bi, qi: (bi, qi, 0)),
            pl.BlockSpec((None, s, SB_WIDTH), lambda bi, qi: (bi, 0, 1)),
            pl.BlockSpec((None, s, SB_WIDTH), lambda bi, qi: (bi, 0, 2)),
        ],
        out_specs=pl.BlockSpec((None, ts, SB_WIDTH), lambda bi, qi: (bi, qi, 0)),
        out_shape=jax.ShapeDtypeStruct((b, s, SB_WIDTH), BF16),
        scratch_shapes=[pltpu.VMEM((ts, SB_WIDTH), F32), pltpu.VMEM((SB_HEADS, ts, LANES), F32)],
        compiler_params=pltpu.CompilerParams(
            dimension_semantics=("arbitrary", "arbitrary"), vmem_limit_bytes=VMEM_LIMIT),
        name="attn",
    )(qkv3, qkv3, qkv3)


def _mix_kernel(attn_ref, ga_ref, gbb_ref, x_ref, wa_ref, wo_ref, g1_ref, b1_ref, wrt_ref, brt_ref,
                h1_ref, meta_ref, cnt_ref, xls_ref, tcnt_ref, *, tm, alpha):
    i = pl.program_id(0)
    reps = tm // LANES
    branch_a = _dot(attn_ref[...], wa_ref[...])
    merged = ga_ref[...] * branch_a + gbb_ref[...]
    mixed = _dot(merged.astype(BF16), wo_ref[...])
    h1 = _layer_norm(alpha * x_ref[...] + mixed, g1_ref[...], b1_ref[...])
    h1_ref[...] = h1
    h1b = h1.astype(BF16)

    lt = _dot_nt(wrt_ref[...], h1b) + jnp.concatenate([brt_ref[...]] * reps, axis=1)
    r = [lt[k:k + 1, :] for k in range(N_GROUPS)]
    gmax = jnp.maximum(jnp.maximum(r[0], r[1]), jnp.maximum(r[2], r[3]))
    gidx = jnp.where(r[0] == gmax, 0, jnp.where(r[1] == gmax, 1, jnp.where(r[2] == gmax, 2, 3)))
    gsum = (jnp.exp(r[0] - gmax) + jnp.exp(r[1] - gmax)) + (jnp.exp(r[2] - gmax) + jnp.exp(r[3] - gmax))
    gprob = 1.0 / gsum
    epg = EXPERTS_PER_GROUP
    slabs = [lt[8 + g * epg:8 + (g + 1) * epg, :] for g in range(N_GROUPS)]
    el = jnp.where(gidx == 0, slabs[0], jnp.where(gidx == 1, slabs[1], jnp.where(gidx == 2, slabs[2], slabs[3])))
    r8 = lax.broadcasted_iota(jnp.int32, (epg, tm), 0)
    m1 = jnp.max(el, axis=0, keepdims=True)
    i1 = jnp.min(jnp.where(el == m1, r8, epg), axis=0, keepdims=True)
    el2 = jnp.where(r8 == i1, -jnp.inf, el)
    m2 = jnp.max(el2, axis=0, keepdims=True)
    i2 = jnp.min(jnp.where(el2 == m2, r8, epg), axis=0, keepdims=True)
    dlt = jnp.exp(m2 - m1)
    w1 = gprob / (1.0 + dlt)
    w2 = gprob * dlt / (1.0 + dlt)
    e1 = gidx * epg + i1
    e2 = gidx * epg + i2

    r32 = lax.broadcasted_iota(jnp.int32, (N_EXPERTS, tm), 0)
    is1 = r32 == e1
    is2 = r32 == e2
    onehot = jnp.where(is1 | is2, 1.0, 0.0).astype(BF16)
    trow = lax.broadcasted_iota(jnp.int32, (tm, tm), 0)
    tcol = lax.broadcasted_iota(jnp.int32, (tm, tm), 1)
    earlier = (trow < tcol).astype(BF16)

    @pl.when(i == 0)
    def _():
        cnt_ref[...] = jnp.zeros((N_EXPERTS, LANES), F32)

    before_tile = _dot(onehot, earlier)
    before = before_tile + jnp.concatenate([cnt_ref[...]] * reps, axis=1)
    rank1 = jnp.sum(jnp.where(is1, before, 0.0), axis=0, keepdims=True)
    rank2 = jnp.sum(jnp.where(is2, before, 0.0), axis=0, keepdims=True)
    ones_b = jnp.ones((tm, LANES), BF16)
    tile_cnt = _dot(onehot, ones_b)
    cnt_ref[...] += tile_cnt
    tcnt_ref[...] = tile_cnt

    zrow = jnp.zeros((1, tm), F32)
    meta_ref[...] = jnp.concatenate(
        [e1.astype(F32), e2.astype(F32), w1, w2, rank1, rank2, zrow, zrow], axis=0)

    e32r = lax.broadcasted_iota(jnp.int32, (N_EXPERTS, N_EXPERTS), 0)
    e32c = lax.broadcasted_iota(jnp.int32, (N_EXPERTS, N_EXPERTS), 1)
    lower = (e32c < e32r).astype(BF16)
    seg_off = _dot(_dot(lower, onehot).astype(BF16), ones_b)
    local = before_tile + jnp.concatenate([seg_off] * reps, axis=1)
    ld1 = jnp.sum(jnp.where(is1, local, 0.0), axis=0, keepdims=True).astype(jnp.int32)
    ld2 = jnp.sum(jnp.where(is2, local, 0.0), axis=0, keepdims=True).astype(jnp.int32)
    srow = lax.broadcasted_iota(jnp.int32, (2 * tm, tm), 0)
    perm = jnp.where((srow == ld1) | (srow == ld2), 1.0, 0.0).astype(BF16)
    xs = _dot(perm, h1b)
    for c in range(xs.shape[1] // LANES):
        xls_ref[pl.ds(c, 2 * tm, stride=SUBLANES), :] = xs[:, c * LANES:(c + 1) * LANES]


def _mix_call(attn2, gate_a, gbb, x2, wa_b, wo_b, g1, b1, wrt_b, brt, alpha):
    t, d = x2.shape
    tm = PROJ_TILE
    kern = functools.partial(_mix_kernel, tm=tm, alpha=alpha)
    const = lambda i: (0, 0)
    rowblk = lambda i: (i, 0)
    return pl.pallas_call(
        kern,
        grid=(t // tm,),
        in_specs=[
            pl.BlockSpec((tm, SB_WIDTH), rowblk),
            pl.BlockSpec((tm, d), rowblk),
            pl.BlockSpec((tm, d), rowblk),
            pl.BlockSpec((tm, d), rowblk),
            pl.BlockSpec((SB_WIDTH, d), const),
            pl.BlockSpec((d, d), const),
            pl.BlockSpec((1, d), const),
            pl.BlockSpec((1, d), const),
            pl.BlockSpec((ROUTER_ROWS, d), const),
            pl.BlockSpec((ROUTER_ROWS, LANES), const),
        ],
        out_specs=[
            pl.BlockSpec((tm, d), rowblk),
            pl.BlockSpec((8, tm), lambda i: (0, i)),
            pl.BlockSpec((N_EXPERTS, LANES), const),
            pl.BlockSpec((2 * tm * SUBLANES, LANES), rowblk),
            pl.BlockSpec((None, N_EXPERTS, LANES), lambda i: (i, 0, 0)),
        ],
        out_shape=[
            jax.ShapeDtypeStruct((t, d), F32),
            jax.ShapeDtypeStruct((8, t), F32),
            jax.ShapeDtypeStruct((N_EXPERTS, LANES), F32),
            jax.ShapeDtypeStruct((2 * t * SUBLANES, LANES), F32),
            jax.ShapeDtypeStruct((t // tm, N_EXPERTS, LANES), F32),
        ],
        compiler_params=pltpu.CompilerParams(
            dimension_semantics=("arbitrary",), vmem_limit_bytes=VMEM_LIMIT),
        name="mix",
    )(attn2, gate_a, gbb, x2, wa_b, wo_b, g1, b1, wrt_b, brt)


def _row_copy(src_ref, src_row, dst_ref, dst_row, sem):
    return pltpu.make_async_copy(src_ref.at[pl.ds(src_row, 1), :], dst_ref.at[pl.ds(dst_row, 1), :], sem)


def _tok_rows(tok, n_tok):
    return pl.ds(pl.multiple_of(tok * SUBLANES, SUBLANES), pl.multiple_of(n_tok * SUBLANES, SUBLANES))


def _moe_kernel(be_ref, r0_ref, t0_ref, t1_ref, nv_ref, nu_ref, cumt_ref, soff_ref,
                xls_ref, wg_ref, wu_ref, wd_ref, y_ref,
                xbuf_ref, zero_ref, sem, wgb_ref, wub_ref, wdb_ref, prev_ref, *, blk, tile_rows):
    i = pl.program_id(0)
    n_used = nu_ref[0]
    e = be_ref[i]
    cur = i % 2
    ne = N_EXPERTS

    def gather(b, slot):
        eb = be_ref[b]
        r0 = r0_ref[b]
        nv = nv_ref[b]

        def seg(t, c):
            first = cumt_ref[t * ne + eb]
            lo = jnp.maximum(first, r0)
            n = jnp.minimum(cumt_ref[(t + 1) * ne + eb], r0 + nv) - lo

            @pl.when(n > 0)
            def _():
                src = t * tile_rows + soff_ref[t * ne + eb] + (lo - first)
                pltpu.make_async_copy(xls_ref.at[_tok_rows(src, n), :],
                                      xbuf_ref.at[slot, _tok_rows(lo - r0, n), :], sem.at[slot]).start()
            return c

        lax.fori_loop(t0_ref[b], t1_ref[b] + 1, seg, 0)

        @pl.when(nv < blk)
        def _():
            pltpu.make_async_copy(zero_ref.at[_tok_rows(0, blk - nv), :],
                                  xbuf_ref.at[slot, _tok_rows(nv, blk - nv), :], sem.at[slot]).start()

    def drain(slot):
        pltpu.make_async_copy(xls_ref.at[pl.ds(0, blk * SUBLANES), :], xbuf_ref.at[slot], sem.at[slot]).wait()

    @pl.when(i == 0)
    def _():
        prev_ref[0] = -1
        zero_ref[...] = jnp.zeros(zero_ref.shape, F32)

    @pl.when(jnp.logical_and(i == 0, n_used > 0))
    def _():
        gather(0, 0)

    @pl.when(i < n_used)
    def _():
        @pl.when(e != prev_ref[0])
        def _():
            wgb_ref[...] = wg_ref[...].astype(BF16)
            wub_ref[...] = wu_ref[...].astype(BF16)
            wdb_ref[...] = wd_ref[...].astype(BF16)
            prev_ref[0] = e

        drain(cur)
        xb = jnp.concatenate([xbuf_ref[cur, pl.ds(c, blk, stride=SUBLANES), :] for c in range(SUBLANES)],
                             axis=1).astype(BF16)
        gather(jnp.minimum(i + 1, n_used - 1), 1 - cur)
        g = _dot(xb, wgb_ref[...])
        u = _dot(xb, wub_ref[...])
        hidden = g * jax.nn.sigmoid(g) * u
        y_ref[...] = _dot(hidden.astype(BF16), wdb_ref[...])

    @pl.when(i == n_used - 1)
    def _():
        drain(1 - cur)

    @pl.when(i >= n_used)
    def _():
        y_ref[...] = jnp.zeros(y_ref.shape, F32)


def _moe_call(tables, xls, tile_rows, w_gate, w_up, w_down):
    n_blocks = tables[0].shape[0]
    blk = EXPERT_BLOCK
    _, d, de = w_gate.shape
    wmap = lambda i, be, *_: (be[i], 0, 0)
    grid_spec = pltpu.PrefetchScalarGridSpec(
        num_scalar_prefetch=len(tables),
        grid=(n_blocks,),
        in_specs=[
            pl.BlockSpec(memory_space=pl.ANY),
            pl.BlockSpec((None, d, de), wmap),
            pl.BlockSpec((None, d, de), wmap),
            pl.BlockSpec((None, de, d), wmap),
        ],
        out_specs=pl.BlockSpec((blk, d), lambda i, *_: (i, 0)),
        scratch_shapes=[
            pltpu.VMEM((2, blk * SUBLANES, LANES), F32), pltpu.VMEM((blk * SUBLANES, LANES), F32),
            pltpu.SemaphoreType.DMA((2,)),
            pltpu.VMEM((d, de), BF16), pltpu.VMEM((d, de), BF16), pltpu.VMEM((de, d), BF16),
            pltpu.SMEM((1,), jnp.int32),
        ],
    )
    return pl.pallas_call(
        functools.partial(_moe_kernel, blk=blk, tile_rows=tile_rows),
        grid_spec=grid_spec,
        out_shape=jax.ShapeDtypeStruct((n_blocks * blk, d), F32),
        compiler_params=pltpu.CompilerParams(
            dimension_semantics=("arbitrary",), vmem_limit_bytes=VMEM_LIMIT),
        name="moe",
    )(*tables, xls, w_gate, w_up, w_down)


def _combine_kernel(dest0_ref, destn_ref, h1_ref, rw_ref, y_ref, g2_ref, b2_ref, o_ref, buf_ref, sem,
                    *, tm, alpha):
    i = pl.program_id(0)
    last = pl.num_programs(0) - 1
    cur = i % 2

    def gather(dest_ref, slot):
        for r in range(tm):
            for k in range(2):
                _row_copy(y_ref, dest_ref[0, 0, k * tm + r], buf_ref.at[slot, k], r, sem.at[slot]).start(priority=k)

    def drain(slot):
        for r in range(tm):
            for k in range(2):
                _row_copy(y_ref, 0, buf_ref.at[slot, k], r, sem.at[slot]).wait()

    @pl.when(i == 0)
    def _():
        gather(dest0_ref, 0)

    gather(destn_ref, 1 - cur)
    drain(cur)
    rw = rw_ref[...]
    ffn = rw[:, 0:1] * buf_ref[cur, 0] + rw[:, 1:2] * buf_ref[cur, 1]
    o_ref[...] = _layer_norm(alpha * h1_ref[...] + ffn, g2_ref[...], b2_ref[...])

    @pl.when(i == last)
    def _():
        drain(1 - cur)


def _combine_call(dest_blk, h1, route_w, y_slots, g2, b2, alpha):
    t, d = h1.shape
    tm = ROW_TILE
    nt = t // tm
    kern = functools.partial(_combine_kernel, tm=tm, alpha=alpha)
    const = lambda i: (0, 0)
    return pl.pallas_call(
        kern,
        grid=(nt,),
        in_specs=[
            pl.BlockSpec((1, 1, 2 * tm), lambda i: (i, 0, 0), memory_space=pltpu.SMEM),
            pl.BlockSpec((1, 1, 2 * tm), lambda i: (jnp.minimum(i + 1, nt - 1), 0, 0), memory_space=pltpu.SMEM),
            pl.BlockSpec((tm, d), lambda i: (i, 0)),
            pl.BlockSpec((tm, 2), lambda i: (i, 0)),
            pl.BlockSpec(memory_space=pl.ANY),
            pl.BlockSpec((1, d), const),
            pl.BlockSpec((1, d), const),
        ],
        out_specs=pl.BlockSpec((tm, d), lambda i: (i, 0)),
        out_shape=jax.ShapeDtypeStruct((t, d), F32),
        scratch_shapes=[pltpu.VMEM((2, 2, tm, d), F32), pltpu.SemaphoreType.DMA((2,))],
        compiler_params=pltpu.CompilerParams(
            dimension_semantics=("arbitrary",), vmem_limit_bytes=VMEM_LIMIT),
        name="combine",
    )(dest_blk, dest_blk, h1, route_w, y_slots, g2, b2)


def _layer(h, w_in, gate_bias, conv_w, w_branch_a, w_branch_b, w_out, ln1_g, ln1_b,
           w_router_g, b_router_g, w_router_e, b_router_e, w_gate, w_up, w_down, ln2_g, ln2_b, alpha):
    bsz, seq, d = h.shape
    t = bsz * seq
    x2 = h.reshape(t, d)

    qkv, gate_a, gbb = _proj_call(x2, w_in.astype(BF16), gate_bias.reshape(1, 2 * d), conv_w,
                                  w_branch_b.astype(BF16), seq)
    attn = _attn_call(qkv.reshape(bsz, seq, 3 * SB_WIDTH)).reshape(t, SB_WIDTH)

    wrt = jnp.zeros((ROUTER_ROWS, d), F32).at[0:N_GROUPS].set(w_router_g.T).at[8:].set(w_router_e.T)
    brt = jnp.zeros((ROUTER_ROWS,), F32).at[0:N_GROUPS].set(b_router_g).at[8:].set(b_router_e.reshape(-1))
    brt = jnp.broadcast_to(brt[:, None], (ROUTER_ROWS, LANES))
    h1, meta, cnt, xls, tcnt = _mix_call(attn, gate_a, gbb, x2, w_branch_a.astype(BF16), w_out.astype(BF16),
                                         ln1_g.reshape(1, d), ln1_b.reshape(1, d), wrt.astype(BF16), brt, alpha)

    blk = EXPERT_BLOCK
    counts = cnt[:, 0].astype(jnp.int32)
    tile_cnt = tcnt[:, :, 0].astype(jnp.int32)
    n_tiles = tile_cnt.shape[0]
    cum_incl = jnp.cumsum(tile_cnt, axis=0)
    cumt = jnp.concatenate([jnp.zeros((1, N_EXPERTS), jnp.int32), cum_incl], axis=0)
    soff = jnp.cumsum(tile_cnt, axis=1) - tile_cnt
    padded = (counts + blk - 1) // blk * blk
    pad_end = jnp.cumsum(padded)
    pad_start = pad_end - padded
    expert = meta[0:2].astype(jnp.int32)
    rank = meta[4:6].astype(jnp.int32)
    onehot = expert[:, :, None] == jnp.arange(N_EXPERTS, dtype=jnp.int32)
    dest = jnp.sum(jnp.where(onehot, pad_start, 0), axis=-1) + rank
    n_blocks = (2 * t) // blk + N_EXPERTS
    block_start = jnp.arange(n_blocks, dtype=jnp.int32) * blk
    block_expert = jnp.minimum(jnp.sum(pad_end[None, :] <= block_start[:, None], axis=1), N_EXPERTS - 1)
    n_used = (pad_end[-1] // blk).reshape(1)
    tm = ROW_TILE
    dest_blk = dest.reshape(2, t // tm, tm).transpose(1, 0, 2).reshape(t // tm, 1, 2 * tm)

    be = block_expert.astype(jnp.int32)
    r0 = block_start - pad_start[be]
    nv = jnp.clip(counts[be] - r0, 0, blk)
    t0 = jnp.minimum(jnp.sum(cum_incl[:, be] <= r0[None, :], axis=0), n_tiles - 1)
    t1 = jnp.sum(cumt[:-1][:, be] < (r0 + nv)[None, :], axis=0) - 1
    tables = (be, r0, t0, t1, nv, n_used, cumt.reshape(-1), soff.reshape(-1))
    y_slots = _moe_call(tuple(a.astype(jnp.int32) for a in tables), xls, 2 * PROJ_TILE, w_gate, w_up, w_down)
    out = _combine_call(dest_blk, h1, meta[2:4].T, y_slots, ln2_g.reshape(1, d), ln2_b.reshape(1, d), alpha)
    return out.reshape(bsz, seq, d)


def kernel(x, w_in, gate_bias, conv_w, w_branch_a, w_branch_b, w_out, ln1_g, ln1_b, w_router_g, b_router_g,
           w_router_e, b_router_e, w_gate, w_up, w_down, ln2_g, ln2_b):
    depth = w_in.shape[0]
    alpha = (2.0 * depth) ** 0.25
    h = x
    for l in range(depth):
        h = _layer(h, w_in[l], gate_bias[l], conv_w[l], w_branch_a[l], w_branch_b[l], w_out[l], ln1_g[l],
                   ln1_b[l], w_router_g[l], b_router_g[l], w_router_e[l], b_router_e[l], w_gate[l], w_up[l],
                   w_down[l], ln2_g[l], ln2_b[l], alpha)
    return h
```

```python
import functools
from typing import NamedTuple

import jax
import jax.numpy as jnp
from jax import lax
from jax.experimental import pallas as pl
from jax.experimental.pallas import tpu as pltpu

F32 = jnp.float32
BF16 = jnp.bfloat16

SB_HEADS = 8
SB_HEAD_DIM = 64
SB_WIDTH = SB_HEADS * SB_HEAD_DIM
N_GROUPS = 4
EXPERTS_PER_GROUP = 8
N_EXPERTS = N_GROUPS * EXPERTS_PER_GROUP
LN_EPS = 1e-5

LANES = 128
SUBLANES = 8
HEAD_PAIR = LANES
PROJ_TILE = 512
SUB_TILE = 256
ATTN_TILE = 256
ROW_TILE = 256
EXPERT_BLOCK = 512
ROUTER_ROWS = 8 + N_EXPERTS
TAIL_CUTOFF = 110.0
MASKED_LOGIT = -1e30
LOG2E = 1.4426950408889634
VMEM_LIMIT = 56 * 1024 * 1024


def _dot(a, b):
    return jnp.dot(a, b, preferred_element_type=F32)


def _dot_nt(a, b):
    return lax.dot_general(a, b, (((1,), (1,)), ((), ())), preferred_element_type=F32)


def _layer_norm(y, g, b):
    mu = jnp.mean(y, axis=-1, keepdims=True)
    d = y - mu
    var = jnp.mean(d * d, axis=-1, keepdims=True)
    return d * lax.rsqrt(var + LN_EPS) * g + b


def _proj_kernel(x_ref, w_ref, gbias_ref, convw_ref, wbb_ref, qkv_ref, ga_ref, gbb_ref, ubuf_ref,
                 *, tm, sub, tiles_per_seq, d_model):
    i = pl.program_id(0)
    cw = SB_WIDTH
    c0 = 3 * SB_WIDTH
    g0 = c0 + 3 * cw
    cwt = convw_ref[...]

    @pl.when(i % tiles_per_seq == 0)
    def _():
        ubuf_ref[0:8, :] = jnp.zeros((8, cw), F32)

    chunks = range(tm // sub)
    rows = [slice(h * sub, (h + 1) * sub) for h in chunks]
    xb = [x_ref[rows[h], :].astype(BF16) for h in chunks]
    conv_in = [_dot(xb[h], w_ref[:, c0:g0]) for h in chunks]
    qkv = [_dot(xb[h], w_ref[:, 0:c0]) for h in chunks]
    gated = []
    for h in chunks:
        cb = conv_in[h][:, 0:cw]
        u = conv_in[h][:, cw:2 * cw] * conv_in[h][:, 2 * cw:3 * cw]
        ubuf_ref[8:sub + 8, :] = u
        y = (cwt[0:1, :] * ubuf_ref[pl.ds(6, sub), :] + cwt[1:2, :] * ubuf_ref[pl.ds(7, sub), :]
             + cwt[2:3, :] * u)
        ubuf_ref[0:8, :] = u[sub - 8:sub, :]
        gated.append((cb * y).astype(BF16))
        qkv_ref[rows[h], 0:SB_WIDTH] = (qkv[h][:, 0:SB_WIDTH] * (SB_HEAD_DIM ** -0.5)).astype(BF16)
        qkv_ref[rows[h], SB_WIDTH:c0] = qkv[h][:, SB_WIDTH:c0].astype(BF16)
    gate_logits = [_dot(xb[h], w_ref[:, g0:g0 + 2 * d_model]) for h in chunks]
    branch_b = [_dot(gated[h], wbb_ref[...]) for h in chunks]
    for h in chunks:
        gates = jax.nn.sigmoid(gate_logits[h] + gbias_ref[...])
        ga_ref[rows[h], :] = gates[:, 0:d_model]
        gbb_ref[rows[h], :] = gates[:, d_model:2 * d_model] * branch_b[h]


def _proj_call(x2, w_in_b, gbias, conv_w, wbb_b, seq):
    t, d = x2.shape
    pw = w_in_b.shape[1]
    tm = PROJ_TILE
    sub = SUB_TILE
    cw = conv_w.shape[1]
    kern = functools.partial(_proj_kernel, tm=tm, sub=sub, tiles_per_seq=seq // tm, d_model=d)
    const = lambda i: (0, 0)
    return pl.pallas_call(
        kern,
        grid=(t // tm,),
        in_specs=[
            pl.BlockSpec((tm, d), lambda i: (i, 0)),
            pl.BlockSpec((d, pw), const),
            pl.BlockSpec((1, 2 * d), const),
            pl.BlockSpec((3, cw), const),
            pl.BlockSpec((cw, d), const),
        ],
        out_specs=[
            pl.BlockSpec((tm, 3 * SB_WIDTH), lambda i: (i, 0)),
            pl.BlockSpec((tm, d), lambda i: (i, 0)),
            pl.BlockSpec((tm, d), lambda i: (i, 0)),
        ],
        out_shape=[
            jax.ShapeDtypeStruct((t, 3 * SB_WIDTH), BF16),
            jax.ShapeDtypeStruct((t, d), F32),
            jax.ShapeDtypeStruct((t, d), F32),
        ],
        scratch_shapes=[pltpu.VMEM((sub + 8, cw), F32)],
        compiler_params=pltpu.CompilerParams(
            dimension_semantics=("arbitrary",), vmem_limit_bytes=VMEM_LIMIT),
        name="proj",
    )(x2, w_in_b, gbias, conv_w, wbb_b)


def _attn_kernel(q_ref, k_ref, v_ref, o_ref, acc_ref, c_ref, *, ts):
    qi = pl.program_id(1)
    lane = lax.broadcasted_iota(jnp.int32, (ts, HEAD_PAIR), 1)
    row = lax.broadcasted_iota(jnp.int32, (ts, ts), 0)
    col = lax.broadcasted_iota(jnp.int32, (ts, ts), 1)
    later = (row > col).astype(BF16)
    later2 = jnp.concatenate([later, later], axis=0)
    ahead = col - row
    reps = ts // LANES
    zero_b = jnp.zeros((ts, HEAD_PAIR), BF16)
    halves = HEAD_PAIR // SB_HEAD_DIM
    in_head = [(lane >= h * SB_HEAD_DIM) & (lane < (h + 1) * SB_HEAD_DIM) for h in range(halves)]

    acc_ref[...] = jnp.zeros(acc_ref.shape, F32)
    c_ref[...] = jnp.zeros(c_ref.shape, F32)

    def tile(j):
        start = pl.multiple_of(j * ts, ts)
        bias = jnp.where(ahead < (qi - j) * ts, 0.0, MASKED_LOGIT)
        for p in range(SB_WIDTH // HEAD_PAIR):
            cols = slice(p * HEAD_PAIR, (p + 1) * HEAD_PAIR)
            q = q_ref[:, cols]
            kp = k_ref[pl.ds(start, ts), cols]
            vp = v_ref[pl.ds(start, ts), cols]
            out = None
            for h in range(halves):
                head = p * halves + h
                z = _dot_nt(q, jnp.where(in_head[h], kp, zero_b)) + bias
                sp = jnp.maximum(z, 0.0) + jnp.log(1.0 + jnp.exp2(jnp.abs(z) * (-LOG2E)))
                hi = sp.astype(BF16)
                lo = (sp - hi.astype(F32)).astype(BF16)
                inner = _dot(jnp.concatenate([hi, lo], axis=1), later2)
                cvec = c_ref[head]
                carry = jnp.concatenate([cvec] * reps, axis=1)
                w = jnp.exp(z - sp - inner - carry)
                pv = _dot(w.astype(BF16), jnp.where(in_head[h], vp, zero_b))
                out = pv if out is None else out + pv
                total = inner[:, 0:1] + sp[:, 0:1]
                c_ref[head] = cvec + jnp.broadcast_to(total, (ts, LANES))
            acc_ref[:, cols] += out

    def cond(carry):
        j, cmin = carry
        return jnp.logical_and(j >= 0, cmin < TAIL_CUTOFF)

    def body(carry):
        j, _ = carry
        tile(j)
        return j - 1, jnp.min(c_ref[...])

    lax.while_loop(cond, body, (qi, jnp.float32(0.0)))
    o_ref[...] = acc_ref[...].astype(BF16)


def _attn_call(qkv3):
    b, s, _ = qkv3.shape
    ts = ATTN_TILE
    kern = functools.partial(_attn_kernel, ts=ts)
    return pl.pallas_call(
        kern,
        grid=(b, s // ts),
        in_specs=[
            pl.BlockSpec((None, ts, SB_WIDTH), lambda bi, qi: (bi, qi, 0)),
            pl.BlockSpec((None, s, SB_WIDTH), lambda bi, qi: (bi, 0, 1)),
            pl.BlockSpec((None, s, SB_WIDTH), lambda bi, qi: (bi, 0, 2)),
        ],
        out_specs=pl.BlockSpec((None, ts, SB_WIDTH), lambda bi, qi: (bi, qi, 0)),
        out_shape=jax.ShapeDtypeStruct((b, s, SB_WIDTH), BF16),
        scratch_shapes=[pltpu.VMEM((ts, SB_WIDTH), F32), pltpu.VMEM((SB_HEADS, ts, LANES), F32)],
        compiler_params=pltpu.CompilerParams(
            dimension_semantics=("arbitrary", "arbitrary"), vmem_limit_bytes=VMEM_LIMIT),
        name="attn",
    )(qkv3, qkv3, qkv3)


def _mix_kernel(attn_ref, ga_ref, gbb_ref, x_ref, wa_ref, wo_ref, g1_ref, b1_ref, wrt_ref, brt_ref,
                h1_ref, meta_ref, cnt_ref, xls_ref, tcnt_ref, *, tile, tm, alpha):
    @pl.when(pl.program_id(0) == 0)
    def _():
        cnt_ref[...] = jnp.zeros((N_EXPERTS, LANES), F32)

    chunks = range(tile // tm)
    reps = tm // LANES
    rows = [slice(h * tm, (h + 1) * tm) for h in chunks]
    branch_a = [_dot(attn_ref[rows[h], :], wa_ref[...]) for h in chunks]
    mixed = [_dot((ga_ref[rows[h], :] * branch_a[h] + gbb_ref[rows[h], :]).astype(BF16), wo_ref[...])
             for h in chunks]
    h1b = []
    for h in chunks:
        h1 = _layer_norm(alpha * x_ref[rows[h], :] + mixed[h], g1_ref[...], b1_ref[...])
        h1_ref[rows[h], :] = h1
        h1b.append(h1.astype(BF16))
    bias = jnp.concatenate([brt_ref[...]] * reps, axis=1)
    logits = [_dot_nt(wrt_ref[...], h1b[h]) + bias for h in chunks]
    route = [_route(logits[h], tm) for h in chunks]

    trow = lax.broadcasted_iota(jnp.int32, (tm, tm), 0)
    tcol = lax.broadcasted_iota(jnp.int32, (tm, tm), 1)
    earlier = (trow < tcol).astype(BF16)
    e32r = lax.broadcasted_iota(jnp.int32, (N_EXPERTS, N_EXPERTS), 0)
    e32c = lax.broadcasted_iota(jnp.int32, (N_EXPERTS, N_EXPERTS), 1)
    lower = (e32c < e32r).astype(BF16)
    ones_b = jnp.ones((tm, LANES), BF16)
    before_tile = [_dot(route[h].onehot, earlier) for h in chunks]
    tile_cnt = [_dot(route[h].onehot, ones_b) for h in chunks]
    smaller = [_dot(lower, route[h].onehot).astype(BF16) for h in chunks]
    seg_off = [_dot(smaller[h], ones_b) for h in chunks]

    zrow = jnp.zeros((1, tm), F32)
    perm = []
    for h in chunks:
        rt = route[h]
        before = before_tile[h] + jnp.concatenate([cnt_ref[...]] * reps, axis=1)
        rank1 = jnp.sum(jnp.where(rt.is1, before, 0.0), axis=0, keepdims=True)
        rank2 = jnp.sum(jnp.where(rt.is2, before, 0.0), axis=0, keepdims=True)
        cnt_ref[...] += tile_cnt[h]
        tcnt_ref[h] = tile_cnt[h]
        meta_ref[:, rows[h]] = jnp.concatenate(
            [rt.e1.astype(F32), rt.e2.astype(F32), rt.w1, rt.w2, rank1, rank2, zrow, zrow], axis=0)
        local = before_tile[h] + jnp.concatenate([seg_off[h]] * reps, axis=1)
        ld1 = jnp.sum(jnp.where(rt.is1, local, 0.0), axis=0, keepdims=True).astype(jnp.int32)
        ld2 = jnp.sum(jnp.where(rt.is2, local, 0.0), axis=0, keepdims=True).astype(jnp.int32)
        srow = lax.broadcasted_iota(jnp.int32, (2 * tm, tm), 0)
        perm.append(jnp.where((srow == ld1) | (srow == ld2), 1.0, 0.0).astype(BF16))
    xs = [_dot(perm[h], h1b[h]) for h in chunks]
    for h in chunks:
        base = h * 2 * tm * SUBLANES
        for c in range(xs[h].shape[1] // LANES):
            xls_ref[pl.ds(base + c, 2 * tm, stride=SUBLANES), :] = xs[h][:, c * LANES:(c + 1) * LANES]


class _Route(NamedTuple):
    e1: jax.Array
    e2: jax.Array
    w1: jax.Array
    w2: jax.Array
    is1: jax.Array
    is2: jax.Array
    onehot: jax.Array


def _route(lt, tm):
    r = [lt[k:k + 1, :] for k in range(N_GROUPS)]
    gmax = jnp.maximum(jnp.maximum(r[0], r[1]), jnp.maximum(r[2], r[3]))
    gidx = jnp.where(r[0] == gmax, 0, jnp.where(r[1] == gmax, 1, jnp.where(r[2] == gmax, 2, 3)))
    gsum = (jnp.exp(r[0] - gmax) + jnp.exp(r[1] - gmax)) + (jnp.exp(r[2] - gmax) + jnp.exp(r[3] - gmax))
    gprob = 1.0 / gsum
    epg = EXPERTS_PER_GROUP
    slabs = [lt[8 + g * epg:8 + (g + 1) * epg, :] for g in range(N_GROUPS)]
    el = jnp.where(gidx == 0, slabs[0], jnp.where(gidx == 1, slabs[1], jnp.where(gidx == 2, slabs[2], slabs[3])))
    r8 = lax.broadcasted_iota(jnp.int32, (epg, tm), 0)
    m1 = jnp.max(el, axis=0, keepdims=True)
    i1 = jnp.min(jnp.where(el == m1, r8, epg), axis=0, keepdims=True)
    el2 = jnp.where(r8 == i1, -jnp.inf, el)
    m2 = jnp.max(el2, axis=0, keepdims=True)
    i2 = jnp.min(jnp.where(el2 == m2, r8, epg), axis=0, keepdims=True)
    dlt = jnp.exp(m2 - m1)
    w1 = gprob / (1.0 + dlt)
    w2 = gprob * dlt / (1.0 + dlt)
    e1 = gidx * epg + i1
    e2 = gidx * epg + i2
    r32 = lax.broadcasted_iota(jnp.int32, (N_EXPERTS, tm), 0)
    is1 = r32 == e1
    is2 = r32 == e2
    onehot = jnp.where(is1 | is2, 1.0, 0.0).astype(BF16)
    return _Route(e1, e2, w1, w2, is1, is2, onehot)


def _mix_call(attn2, gate_a, gbb, x2, wa_b, wo_b, g1, b1, wrt_b, brt, alpha):
    t, d = x2.shape
    tile = PROJ_TILE
    kern = functools.partial(_mix_kernel, tile=tile, tm=SUB_TILE, alpha=alpha)
    const = lambda i: (0, 0)
    rowblk = lambda i: (i, 0)
    tm = tile
    return pl.pallas_call(
        kern,
        grid=(t // tm,),
        in_specs=[
            pl.BlockSpec((tm, SB_WIDTH), rowblk),
            pl.BlockSpec((tm, d), rowblk),
            pl.BlockSpec((tm, d), rowblk),
            pl.BlockSpec((tm, d), rowblk),
            pl.BlockSpec((SB_WIDTH, d), const),
            pl.BlockSpec((d, d), const),
            pl.BlockSpec((1, d), const),
            pl.BlockSpec((1, d), const),
            pl.BlockSpec((ROUTER_ROWS, d), const),
            pl.BlockSpec((ROUTER_ROWS, LANES), const),
        ],
        out_specs=[
            pl.BlockSpec((tm, d), rowblk),
            pl.BlockSpec((8, tm), lambda i: (0, i)),
            pl.BlockSpec((N_EXPERTS, LANES), const),
            pl.BlockSpec((2 * tm * SUBLANES, LANES), rowblk),
            pl.BlockSpec((tile // SUB_TILE, N_EXPERTS, LANES), lambda i: (i, 0, 0)),
        ],
        out_shape=[
            jax.ShapeDtypeStruct((t, d), F32),
            jax.ShapeDtypeStruct((8, t), F32),
            jax.ShapeDtypeStruct((N_EXPERTS, LANES), F32),
            jax.ShapeDtypeStruct((2 * t * SUBLANES, LANES), F32),
            jax.ShapeDtypeStruct((t // SUB_TILE, N_EXPERTS, LANES), F32),
        ],
        compiler_params=pltpu.CompilerParams(
            dimension_semantics=("arbitrary",), vmem_limit_bytes=VMEM_LIMIT),
        name="mix",
    )(attn2, gate_a, gbb, x2, wa_b, wo_b, g1, b1, wrt_b, brt)


def _row_copy(src_ref, src_row, dst_ref, dst_row, sem):
    return pltpu.make_async_copy(src_ref.at[pl.ds(src_row, 1), :], dst_ref.at[pl.ds(dst_row, 1), :], sem)


def _tok_rows(tok, n_tok):
    return pl.ds(pl.multiple_of(tok * SUBLANES, SUBLANES), pl.multiple_of(n_tok * SUBLANES, SUBLANES))


def _moe_kernel(be_ref, r0_ref, t0_ref, t1_ref, nv_ref, nu_ref, cumt_ref, soff_ref,
                xls_ref, wg_ref, wu_ref, wd_ref, y_ref,
                xbuf_ref, zero_ref, sem, wgb_ref, wub_ref, wdb_ref, prev_ref, *, blk, sub, tile_rows):
    i = pl.program_id(0)
    n_used = nu_ref[0]
    e = be_ref[i]
    cur = i % 2
    ne = N_EXPERTS

    def gather(b, slot):
        eb = be_ref[b]
        r0 = r0_ref[b]
        nv = nv_ref[b]

        def seg(t, c):
            first = cumt_ref[t * ne + eb]
            lo = jnp.maximum(first, r0)
            n = jnp.minimum(cumt_ref[(t + 1) * ne + eb], r0 + nv) - lo

            @pl.when(n > 0)
            def _():
                src = t * tile_rows + soff_ref[t * ne + eb] + (lo - first)
                pltpu.make_async_copy(xls_ref.at[_tok_rows(src, n), :],
                                      xbuf_ref.at[slot, _tok_rows(lo - r0, n), :], sem.at[slot]).start()
            return c

        lax.fori_loop(t0_ref[b], t1_ref[b] + 1, seg, 0)

        @pl.when(nv < blk)
        def _():
            pltpu.make_async_copy(zero_ref.at[_tok_rows(0, blk - nv), :],
                                  xbuf_ref.at[slot, _tok_rows(nv, blk - nv), :], sem.at[slot]).start()

    def drain(slot):
        pltpu.make_async_copy(xls_ref.at[pl.ds(0, blk * SUBLANES), :], xbuf_ref.at[slot], sem.at[slot]).wait()

    @pl.when(i == 0)
    def _():
        prev_ref[0] = -1
        zero_ref[...] = jnp.zeros(zero_ref.shape, F32)

    @pl.when(jnp.logical_and(i == 0, n_used > 0))
    def _():
        gather(0, 0)

    @pl.when(i < n_used)
    def _():
        @pl.when(e != prev_ref[0])
        def _():
            wgb_ref[...] = wg_ref[...].astype(BF16)
            wub_ref[...] = wu_ref[...].astype(BF16)
            wdb_ref[...] = wd_ref[...].astype(BF16)
            prev_ref[0] = e

        drain(cur)
        xbs = [jnp.concatenate([xbuf_ref[cur, pl.ds(h * sub * SUBLANES + c, sub, stride=SUBLANES), :]
                                for c in range(SUBLANES)], axis=1).astype(BF16) for h in range(blk // sub)]
        gather(jnp.minimum(i + 1, n_used - 1), 1 - cur)
        gate = [_dot(xb, wgb_ref[...]) for xb in xbs]
        up = [_dot(xb, wub_ref[...]) for xb in xbs]
        hidden = [(g * jax.nn.sigmoid(g) * u).astype(BF16) for g, u in zip(gate, up)]
        for h, hid in enumerate(hidden):
            y_ref[h * sub:(h + 1) * sub, :] = _dot(hid, wdb_ref[...])

    @pl.when(i == n_used - 1)
    def _():
        drain(1 - cur)

    @pl.when(i >= n_used)
    def _():
        y_ref[...] = jnp.zeros(y_ref.shape, F32)


def _moe_call(tables, xls, tile_rows, w_gate, w_up, w_down):
    n_blocks = tables[0].shape[0]
    blk = EXPERT_BLOCK
    _, d, de = w_gate.shape
    wmap = lambda i, be, *_: (be[i], 0, 0)
    grid_spec = pltpu.PrefetchScalarGridSpec(
        num_scalar_prefetch=len(tables),
        grid=(n_blocks,),
        in_specs=[
            pl.BlockSpec(memory_space=pl.ANY),
            pl.BlockSpec((None, d, de), wmap),
            pl.BlockSpec((None, d, de), wmap),
            pl.BlockSpec((None, de, d), wmap),
        ],
        out_specs=pl.BlockSpec((blk, d), lambda i, *_: (i, 0)),
        scratch_shapes=[
            pltpu.VMEM((2, blk * SUBLANES, LANES), F32), pltpu.VMEM((blk * SUBLANES, LANES), F32),
            pltpu.SemaphoreType.DMA((2,)),
            pltpu.VMEM((d, de), BF16), pltpu.VMEM((d, de), BF16), pltpu.VMEM((de, d), BF16),
            pltpu.SMEM((1,), jnp.int32),
        ],
    )
    return pl.pallas_call(
        functools.partial(_moe_kernel, blk=blk, sub=SUB_TILE, tile_rows=tile_rows),
        grid_spec=grid_spec,
        out_shape=jax.ShapeDtypeStruct((n_blocks * blk, d), F32),
        compiler_params=pltpu.CompilerParams(
            dimension_semantics=("arbitrary",), vmem_limit_bytes=VMEM_LIMIT),
        name="moe",
    )(*tables, xls, w_gate, w_up, w_down)


def _combine_kernel(dest0_ref, destn_ref, h1_ref, rw_ref, y_ref, g2_ref, b2_ref, o_ref, buf_ref, sem,
                    *, tm, alpha):
    i = pl.program_id(0)
    last = pl.num_programs(0) - 1
    cur = i % 2

    def gather(dest_ref, slot):
        for r in range(tm):
            for k in range(2):
                _row_copy(y_ref, dest_ref[0, 0, k * tm + r], buf_ref.at[slot, k], r, sem.at[slot]).start(priority=k)

    def drain(slot):
        for r in range(tm):
            for k in range(2):
                _row_copy(y_ref, 0, buf_ref.at[slot, k], r, sem.at[slot]).wait()

    @pl.when(i == 0)
    def _():
        gather(dest0_ref, 0)

    gather(destn_ref, 1 - cur)
    drain(cur)
    rw = rw_ref[...]
    ffn = rw[:, 0:1] * buf_ref[cur, 0] + rw[:, 1:2] * buf_ref[cur, 1]
    o_ref[...] = _layer_norm(alpha * h1_ref[...] + ffn, g2_ref[...], b2_ref[...])

    @pl.when(i == last)
    def _():
        drain(1 - cur)


def _combine_call(dest_blk, h1, route_w, y_slots, g2, b2, alpha):
    t, d = h1.shape
    tm = ROW_TILE
    nt = t // tm
    kern = functools.partial(_combine_kernel, tm=tm, alpha=alpha)
    const = lambda i: (0, 0)
    return pl.pallas_call(
        kern,
        grid=(nt,),
        in_specs=[
            pl.BlockSpec((1, 1, 2 * tm), lambda i: (i, 0, 0), memory_space=pltpu.SMEM),
            pl.BlockSpec((1, 1, 2 * tm), lambda i: (jnp.minimum(i + 1, nt - 1), 0, 0), memory_space=pltpu.SMEM),
            pl.BlockSpec((tm, d), lambda i: (i, 0)),
            pl.BlockSpec((tm, 2), lambda i: (i, 0)),
            pl.BlockSpec(memory_space=pl.ANY),
            pl.BlockSpec((1, d), const),
            pl.BlockSpec((1, d), const),
        ],
        out_specs=pl.BlockSpec((tm, d), lambda i: (i, 0)),
        out_shape=jax.ShapeDtypeStruct((t, d), F32),
        scratch_shapes=[pltpu.VMEM((2, 2, tm, d), F32), pltpu.SemaphoreType.DMA((2,))],
        compiler_params=pltpu.CompilerParams(
            dimension_semantics=("arbitrary",), vmem_limit_bytes=VMEM_LIMIT),
        name="combine",
    )(dest_blk, dest_blk, h1, route_w, y_slots, g2, b2)


def _layer(h, w_in, gate_bias, conv_w, w_branch_a, w_branch_b, w_out, ln1_g, ln1_b,
           w_router_g, b_router_g, w_router_e, b_router_e, w_gate, w_up, w_down, ln2_g, ln2_b, alpha):
    bsz, seq, d = h.shape
    t = bsz * seq
    x2 = h.reshape(t, d)

    qkv, gate_a, gbb = _proj_call(x2, w_in.astype(BF16), gate_bias.reshape(1, 2 * d), conv_w,
                                  w_branch_b.astype(BF16), seq)
    attn = _attn_call(qkv.reshape(bsz, seq, 3 * SB_WIDTH)).reshape(t, SB_WIDTH)

    wrt = jnp.zeros((ROUTER_ROWS, d), F32).at[0:N_GROUPS].set(w_router_g.T).at[8:].set(w_router_e.T)
    brt = jnp.zeros((ROUTER_ROWS,), F32).at[0:N_GROUPS].set(b_router_g).at[8:].set(b_router_e.reshape(-1))
    brt = jnp.broadcast_to(brt[:, None], (ROUTER_ROWS, LANES))
    h1, meta, cnt, xls, tcnt = _mix_call(attn, gate_a, gbb, x2, w_branch_a.astype(BF16), w_out.astype(BF16),
                                         ln1_g.reshape(1, d), ln1_b.reshape(1, d), wrt.astype(BF16), brt, alpha)

    blk = EXPERT_BLOCK
    counts = cnt[:, 0].astype(jnp.int32)
    tile_cnt = tcnt[:, :, 0].astype(jnp.int32)
    n_tiles = tile_cnt.shape[0]
    cum_incl = jnp.cumsum(tile_cnt, axis=0)
    cumt = jnp.concatenate([jnp.zeros((1, N_EXPERTS), jnp.int32), cum_incl], axis=0)
    soff = jnp.cumsum(tile_cnt, axis=1) - tile_cnt
    padded = (counts + blk - 1) // blk * blk
    pad_end = jnp.cumsum(padded)
    pad_start = pad_end - padded
    expert = meta[0:2].astype(jnp.int32)
    rank = meta[4:6].astype(jnp.int32)
    onehot = expert[:, :, None] == jnp.arange(N_EXPERTS, dtype=jnp.int32)
    dest = jnp.sum(jnp.where(onehot, pad_start, 0), axis=-1) + rank
    n_blocks = (2 * t) // blk + N_EXPERTS
    block_start = jnp.arange(n_blocks, dtype=jnp.int32) * blk
    block_expert = jnp.minimum(jnp.sum(pad_end[None, :] <= block_start[:, None], axis=1), N_EXPERTS - 1)
    n_used = (pad_end[-1] // blk).reshape(1)
    tm = ROW_TILE
    dest_blk = dest.reshape(2, t // tm, tm).transpose(1, 0, 2).reshape(t // tm, 1, 2 * tm)

    be = block_expert.astype(jnp.int32)
    r0 = block_start - pad_start[be]
    nv = jnp.clip(counts[be] - r0, 0, blk)
    t0 = jnp.minimum(jnp.sum(cum_incl[:, be] <= r0[None, :], axis=0), n_tiles - 1)
    t1 = jnp.sum(cumt[:-1][:, be] < (r0 + nv)[None, :], axis=0) - 1
    tables = (be, r0, t0, t1, nv, n_used, cumt.reshape(-1), soff.reshape(-1))
    y_slots = _moe_call(tuple(a.astype(jnp.int32) for a in tables), xls, 2 * SUB_TILE, w_gate, w_up, w_down)
    out = _combine_call(dest_blk, h1, meta[2:4].T, y_slots, ln2_g.reshape(1, d), ln2_b.reshape(1, d), alpha)
    return out.reshape(bsz, seq, d)


def kernel(x, w_in, gate_bias, conv_w, w_branch_a, w_branch_b, w_out, ln1_g, ln1_b, w_router_g, b_router_g,
           w_router_e, b_router_e, w_gate, w_up, w_down, ln2_g, ln2_b):
    depth = w_in.shape[0]
    alpha = (2.0 * depth) ** 0.25
    h = x
    for l in range(depth):
        h = _layer(h, w_in[l], gate_bias[l], conv_w[l], w_branch_a[l], w_branch_b[l], w_out[l], ln1_g[l],
                   ln1_b[l], w_router_g[l], b_router_g[l], w_router_e[l], b_router_e[l], w_gate[l], w_up[l],
                   w_down[l], ln2_g[l], ln2_b[l], alpha)
    return h
```

```python
import functools
from typing import NamedTuple

import jax
import jax.numpy as jnp
from jax import lax
from jax.experimental import pallas as pl
from jax.experimental.pallas import tpu as pltpu

F32 = jnp.float32
BF16 = jnp.bfloat16

SB_HEADS = 8
SB_HEAD_DIM = 64
SB_WIDTH = SB_HEADS * SB_HEAD_DIM
N_GROUPS = 4
EXPERTS_PER_GROUP = 8
N_EXPERTS = N_GROUPS * EXPERTS_PER_GROUP
LN_EPS = 1e-5

LANES = 128
SUBLANES = 8
HEAD_PAIR = LANES
PROJ_TILE = 512
SUB_TILE = 256
ATTN_TILE = 128
ATTN_UNROLL = 3
ROW_TILE = 256
EXPERT_BLOCK = 512
ROUTER_ROWS = 8 + N_EXPERTS
TAIL_CUTOFF = 110.0
MASKED_LOGIT = -1e30
LOG2E = 1.4426950408889634
VMEM_LIMIT = 56 * 1024 * 1024


def _dot(a, b):
    return jnp.dot(a, b, preferred_element_type=F32)


def _dot_nt(a, b):
    return lax.dot_general(a, b, (((1,), (1,)), ((), ())), preferred_element_type=F32)


def _layer_norm(y, g, b):
    mu = jnp.mean(y, axis=-1, keepdims=True)
    d = y - mu
    var = jnp.mean(d * d, axis=-1, keepdims=True)
    return d * lax.rsqrt(var + LN_EPS) * g + b


def _proj_kernel(x_ref, w_ref, gbias_ref, convw_ref, wbb_ref, qkv_ref, ga_ref, gbb_ref, ubuf_ref,
                 *, tm, sub, tiles_per_seq, d_model):
    i = pl.program_id(0)
    cw = SB_WIDTH
    c0 = 3 * SB_WIDTH
    g0 = c0 + 3 * cw
    cwt = convw_ref[...]

    @pl.when(i % tiles_per_seq == 0)
    def _():
        ubuf_ref[0:8, :] = jnp.zeros((8, cw), F32)

    chunks = range(tm // sub)
    rows = [slice(h * sub, (h + 1) * sub) for h in chunks]
    xb = [x_ref[rows[h], :].astype(BF16) for h in chunks]
    conv_in = [_dot(xb[h], w_ref[:, c0:g0]) for h in chunks]
    qkv = [_dot(xb[h], w_ref[:, 0:c0]) for h in chunks]
    gated = []
    for h in chunks:
        cb = conv_in[h][:, 0:cw]
        u = conv_in[h][:, cw:2 * cw] * conv_in[h][:, 2 * cw:3 * cw]
        ubuf_ref[8:sub + 8, :] = u
        y = (cwt[0:1, :] * ubuf_ref[pl.ds(6, sub), :] + cwt[1:2, :] * ubuf_ref[pl.ds(7, sub), :]
             + cwt[2:3, :] * u)
        ubuf_ref[0:8, :] = u[sub - 8:sub, :]
        gated.append((cb * y).astype(BF16))
        qkv_ref[rows[h], 0:SB_WIDTH] = (qkv[h][:, 0:SB_WIDTH] * (SB_HEAD_DIM ** -0.5)).astype(BF16)
        qkv_ref[rows[h], SB_WIDTH:c0] = qkv[h][:, SB_WIDTH:c0].astype(BF16)
    gate_logits = [_dot(xb[h], w_ref[:, g0:g0 + 2 * d_model]) for h in chunks]
    branch_b = [_dot(gated[h], wbb_ref[...]) for h in chunks]
    for h in chunks:
        gates = jax.nn.sigmoid(gate_logits[h] + gbias_ref[...])
        ga_ref[rows[h], :] = gates[:, 0:d_model]
        gbb_ref[rows[h], :] = gates[:, d_model:2 * d_model] * branch_b[h]


def _proj_call(x2, w_in_b, gbias, conv_w, wbb_b, seq):
    t, d = x2.shape
    pw = w_in_b.shape[1]
    tm = PROJ_TILE
    sub = SUB_TILE
    cw = conv_w.shape[1]
    kern = functools.partial(_proj_kernel, tm=tm, sub=sub, tiles_per_seq=seq // tm, d_model=d)
    const = lambda i: (0, 0)
    return pl.pallas_call(
        kern,
        grid=(t // tm,),
        in_specs=[
            pl.BlockSpec((tm, d), lambda i: (i, 0)),
            pl.BlockSpec((d, pw), const),
            pl.BlockSpec((1, 2 * d), const),
            pl.BlockSpec((3, cw), const),
            pl.BlockSpec((cw, d), const),
        ],
        out_specs=[
            pl.BlockSpec((tm, 3 * SB_WIDTH), lambda i: (i, 0)),
            pl.BlockSpec((tm, d), lambda i: (i, 0)),
            pl.BlockSpec((tm, d), lambda i: (i, 0)),
        ],
        out_shape=[
            jax.ShapeDtypeStruct((t, 3 * SB_WIDTH), BF16),
            jax.ShapeDtypeStruct((t, d), F32),
            jax.ShapeDtypeStruct((t, d), F32),
        ],
        scratch_shapes=[pltpu.VMEM((sub + 8, cw), F32)],
        compiler_params=pltpu.CompilerParams(
            dimension_semantics=("arbitrary",), vmem_limit_bytes=VMEM_LIMIT),
        name="proj",
    )(x2, w_in_b, gbias, conv_w, wbb_b)


def _attn_kernel(q_ref, k_ref, v_ref, o_ref, acc_ref, c_ref, *, ts, unroll):
    qi = pl.program_id(1)
    lane = lax.broadcasted_iota(jnp.int32, (ts, HEAD_PAIR), 1)
    row = lax.broadcasted_iota(jnp.int32, (ts, ts), 0)
    col = lax.broadcasted_iota(jnp.int32, (ts, ts), 1)
    later = (row > col).astype(BF16)
    later2 = jnp.concatenate([later, later], axis=0)
    ahead = col - row
    reps = ts // LANES
    zero_b = jnp.zeros((ts, HEAD_PAIR), BF16)
    halves = HEAD_PAIR // SB_HEAD_DIM
    in_head = [(lane >= h * SB_HEAD_DIM) & (lane < (h + 1) * SB_HEAD_DIM) for h in range(halves)]

    acc_ref[...] = jnp.zeros(acc_ref.shape, F32)
    c_ref[...] = jnp.zeros(c_ref.shape, F32)

    pairs = range(SB_WIDTH // HEAD_PAIR)
    cols = [slice(p * HEAD_PAIR, (p + 1) * HEAD_PAIR) for p in pairs]

    def sweep(j):
        start, bias = [], []
        for k in range(unroll):
            jj = j - k
            start.append(pl.multiple_of(jnp.maximum(jj, 0) * ts, ts))
            limit = jnp.where(jj >= 0, (qi - jj) * ts, -2 * ts)
            bias.append(jnp.where(ahead < limit, 0.0, MASKED_LOGIT))
        blocks = range(unroll)
        z, sp, split, inner, wb = {}, {}, {}, {}, {}

        def logits(head):
            p, h = divmod(head, halves)
            q = jnp.where(in_head[h], q_ref[:, cols[p]], zero_b)
            for k in blocks:
                z[head, k] = _dot_nt(q, k_ref[pl.ds(start[k], ts), cols[p]]) + bias[k]

        def stay(head):
            for k in blocks:
                zz = z[head, k]
                s = jnp.maximum(zz, 0.0) + jnp.log(1.0 + jnp.exp2(jnp.abs(zz) * (-LOG2E)))
                hi = s.astype(BF16)
                sp[head, k] = s
                split[head, k] = jnp.concatenate([hi, (s - hi.astype(F32)).astype(BF16)], axis=1)

        def tails(head):
            for k in blocks:
                inner[head, k] = _dot(split[head, k], later2)

        def weights(head):
            cvec = c_ref[head]
            for k in blocks:
                c = (head, k)
                carry = jnp.concatenate([cvec] * reps, axis=1)
                wb[c] = jnp.exp(z[c] - sp[c] - inner[c] - carry).astype(BF16)
                total = inner[c][:, 0:1] + sp[c][:, 0:1]
                cvec = cvec + jnp.broadcast_to(total, (ts, LANES))
            c_ref[head] = cvec

        def values(head):
            p, h = divmod(head, halves)
            out = None
            for k in blocks:
                vb = jnp.where(in_head[h], v_ref[pl.ds(start[k], ts), cols[p]], zero_b)
                pv = _dot(wb[head, k], vb)
                out = pv if out is None else out + pv
            acc_ref[:, cols[p]] += out

        stages = (logits, stay, tails, weights, values)
        for step in range(SB_HEADS + len(stages) - 1):
            for s in reversed(range(len(stages))):
                if 0 <= step - s < SB_HEADS:
                    stages[s](step - s)

    def cond(carry):
        j, cmin = carry
        return jnp.logical_and(j >= 0, cmin < TAIL_CUTOFF)

    def body(carry):
        j, _ = carry
        sweep(j)
        return j - unroll, jnp.min(c_ref[...])

    lax.while_loop(cond, body, (qi, jnp.float32(0.0)))
    o_ref[...] = acc_ref[...].astype(BF16)


def _attn_call(qkv3):
    b, s, _ = qkv3.shape
    ts = ATTN_TILE
    kern = functools.partial(_attn_kernel, ts=ts, unroll=ATTN_UNROLL)
    return pl.pallas_call(
        kern,
        grid=(b, s // ts),
        in_specs=[
            pl.BlockSpec((None, ts, SB_WIDTH), lambda bi, qi: (bi, qi, 0)),
            pl.BlockSpec((None, s, SB_WIDTH), lambda bi, qi: (bi, 0, 1)),
            pl.BlockSpec((None, s, SB_WIDTH), lambda bi, qi: (bi, 0, 2)),
        ],
        out_specs=pl.BlockSpec((None, ts, SB_WIDTH), lambda bi, qi: (bi, qi, 0)),
        out_shape=jax.ShapeDtypeStruct((b, s, SB_WIDTH), BF16),
        scratch_shapes=[pltpu.VMEM((ts, SB_WIDTH), F32), pltpu.VMEM((SB_HEADS, ts, LANES), F32)],
        compiler_params=pltpu.CompilerParams(
            dimension_semantics=("arbitrary", "arbitrary"), vmem_limit_bytes=VMEM_LIMIT),
        name="attn",
    )(qkv3, qkv3, qkv3)


def _mix_kernel(attn_ref, ga_ref, gbb_ref, x_ref, wa_ref, wo_ref, g1_ref, b1_ref, wrt_ref, brt_ref,
                h1_ref, meta_ref, cnt_ref, xls_ref, tcnt_ref, *, tile, tm, alpha):
    @pl.when(pl.program_id(0) == 0)
    def _():
        cnt_ref[...] = jnp.zeros((N_EXPERTS, LANES), F32)

    chunks = range(tile // tm)
    reps = tm // LANES
    rows = [slice(h * tm, (h + 1) * tm) for h in chunks]
    branch_a = [_dot(attn_ref[rows[h], :], wa_ref[...]) for h in chunks]
    mixed = [_dot((ga_ref[rows[h], :] * branch_a[h] + gbb_ref[rows[h], :]).astype(BF16), wo_ref[...])
             for h in chunks]
    h1b = []
    for h in chunks:
        h1 = _layer_norm(alpha * x_ref[rows[h], :] + mixed[h], g1_ref[...], b1_ref[...])
        h1_ref[rows[h], :] = h1
        h1b.append(h1.astype(BF16))
    bias = jnp.concatenate([brt_ref[...]] * reps, axis=1)
    logits = [_dot_nt(wrt_ref[...], h1b[h]) + bias for h in chunks]
    route = [_route(logits[h], tm) for h in chunks]

    trow = lax.broadcasted_iota(jnp.int32, (tm, tm), 0)
    tcol = lax.broadcasted_iota(jnp.int32, (tm, tm), 1)
    earlier = (trow < tcol).astype(BF16)
    e32r = lax.broadcasted_iota(jnp.int32, (N_EXPERTS, N_EXPERTS), 0)
    e32c = lax.broadcasted_iota(jnp.int32, (N_EXPERTS, N_EXPERTS), 1)
    lower = (e32c < e32r).astype(BF16)
    ones_b = jnp.ones((tm, LANES), BF16)
    before_tile = [_dot(route[h].onehot, earlier) for h in chunks]
    tile_cnt = [_dot(route[h].onehot, ones_b) for h in chunks]
    smaller = [_dot(lower, route[h].onehot).astype(BF16) for h in chunks]
    seg_off = [_dot(smaller[h], ones_b) for h in chunks]

    zrow = jnp.zeros((1, tm), F32)
    perm = []
    for h in chunks:
        rt = route[h]
        before = before_tile[h] + jnp.concatenate([cnt_ref[...]] * reps, axis=1)
        rank1 = jnp.sum(jnp.where(rt.is1, before, 0.0), axis=0, keepdims=True)
        rank2 = jnp.sum(jnp.where(rt.is2, before, 0.0), axis=0, keepdims=True)
        cnt_ref[...] += tile_cnt[h]
        tcnt_ref[h] = tile_cnt[h]
        meta_ref[:, rows[h]] = jnp.concatenate(
            [rt.e1.astype(F32), rt.e2.astype(F32), rt.w1, rt.w2, rank1, rank2, zrow, zrow], axis=0)
        local = before_tile[h] + jnp.concatenate([seg_off[h]] * reps, axis=1)
        ld1 = jnp.sum(jnp.where(rt.is1, local, 0.0), axis=0, keepdims=True).astype(jnp.int32)
        ld2 = jnp.sum(jnp.where(rt.is2, local, 0.0), axis=0, keepdims=True).astype(jnp.int32)
        srow = lax.broadcasted_iota(jnp.int32, (2 * tm, tm), 0)
        perm.append(jnp.where((srow == ld1) | (srow == ld2), 1.0, 0.0).astype(BF16))
    xs = [_dot(perm[h], h1b[h]) for h in chunks]
    for h in chunks:
        base = h * 2 * tm * SUBLANES
        for c in range(xs[h].shape[1] // LANES):
            xls_ref[pl.ds(base + c, 2 * tm, stride=SUBLANES), :] = xs[h][:, c * LANES:(c + 1) * LANES]


class _Route(NamedTuple):
    e1: jax.Array
    e2: jax.Array
    w1: jax.Array
    w2: jax.Array
    is1: jax.Array
    is2: jax.Array
    onehot: jax.Array


def _route(lt, tm):
    r = [lt[k:k + 1, :] for k in range(N_GROUPS)]
    gmax = jnp.maximum(jnp.maximum(r[0], r[1]), jnp.maximum(r[2], r[3]))
    gidx = jnp.where(r[0] == gmax, 0, jnp.where(r[1] == gmax, 1, jnp.where(r[2] == gmax, 2, 3)))
    gsum = (jnp.exp(r[0] - gmax) + jnp.exp(r[1] - gmax)) + (jnp.exp(r[2] - gmax) + jnp.exp(r[3] - gmax))
    gprob = 1.0 / gsum
    epg = EXPERTS_PER_GROUP
    slabs = [lt[8 + g * epg:8 + (g + 1) * epg, :] for g in range(N_GROUPS)]
    el = jnp.where(gidx == 0, slabs[0], jnp.where(gidx == 1, slabs[1], jnp.where(gidx == 2, slabs[2], slabs[3])))
    r8 = lax.broadcasted_iota(jnp.int32, (epg, tm), 0)
    m1 = jnp.max(el, axis=0, keepdims=True)
    i1 = jnp.min(jnp.where(el == m1, r8, epg), axis=0, keepdims=True)
    el2 = jnp.where(r8 == i1, -jnp.inf, el)
    m2 = jnp.max(el2, axis=0, keepdims=True)
    i2 = jnp.min(jnp.where(el2 == m2, r8, epg), axis=0, keepdims=True)
    dlt = jnp.exp(m2 - m1)
    w1 = gprob / (1.0 + dlt)
    w2 = gprob * dlt / (1.0 + dlt)
    e1 = gidx * epg + i1
    e2 = gidx * epg + i2
    r32 = lax.broadcasted_iota(jnp.int32, (N_EXPERTS, tm), 0)
    is1 = r32 == e1
    is2 = r32 == e2
    onehot = jnp.where(is1 | is2, 1.0, 0.0).astype(BF16)
    return _Route(e1, e2, w1, w2, is1, is2, onehot)


def _mix_call(attn2, gate_a, gbb, x2, wa_b, wo_b, g1, b1, wrt_b, brt, alpha):
    t, d = x2.shape
    tile = PROJ_TILE
    kern = functools.partial(_mix_kernel, tile=tile, tm=SUB_TILE, alpha=alpha)
    const = lambda i: (0, 0)
    rowblk = lambda i: (i, 0)
    tm = tile
    return pl.pallas_call(
        kern,
        grid=(t // tm,),
        in_specs=[
            pl.BlockSpec((tm, SB_WIDTH), rowblk),
            pl.BlockSpec((tm, d), rowblk),
            pl.BlockSpec((tm, d), rowblk),
            pl.BlockSpec((tm, d), rowblk),
            pl.BlockSpec((SB_WIDTH, d), const),
            pl.BlockSpec((d, d), const),
            pl.BlockSpec((1, d), const),
            pl.BlockSpec((1, d), const),
            pl.BlockSpec((ROUTER_ROWS, d), const),
            pl.BlockSpec((ROUTER_ROWS, LANES), const),
        ],
        out_specs=[
            pl.BlockSpec((tm, d), rowblk),
            pl.BlockSpec((8, tm), lambda i: (0, i)),
            pl.BlockSpec((N_EXPERTS, LANES), const),
            pl.BlockSpec((2 * tm * SUBLANES, LANES), rowblk),
            pl.BlockSpec((tile // SUB_TILE, N_EXPERTS, LANES), lambda i: (i, 0, 0)),
        ],
        out_shape=[
            jax.ShapeDtypeStruct((t, d), F32),
            jax.ShapeDtypeStruct((8, t), F32),
            jax.ShapeDtypeStruct((N_EXPERTS, LANES), F32),
            jax.ShapeDtypeStruct((2 * t * SUBLANES, LANES), F32),
            jax.ShapeDtypeStruct((t // SUB_TILE, N_EXPERTS, LANES), F32),
        ],
        compiler_params=pltpu.CompilerParams(
            dimension_semantics=("arbitrary",), vmem_limit_bytes=VMEM_LIMIT),
        name="mix",
    )(attn2, gate_a, gbb, x2, wa_b, wo_b, g1, b1, wrt_b, brt)


def _row_copy(src_ref, src_row, dst_ref, dst_row, sem):
    return pltpu.make_async_copy(src_ref.at[pl.ds(src_row, 1), :], dst_ref.at[pl.ds(dst_row, 1), :], sem)


def _tok_rows(tok, n_tok):
    return pl.ds(pl.multiple_of(tok * SUBLANES, SUBLANES), pl.multiple_of(n_tok * SUBLANES, SUBLANES))


def _moe_kernel(be_ref, r0_ref, t0_ref, t1_ref, nv_ref, nu_ref, cumt_ref, soff_ref,
                xls_ref, wg_ref, wu_ref, wd_ref, y_ref,
                xbuf_ref, zero_ref, sem, wgb_ref, wub_ref, wdb_ref, prev_ref, *, blk, sub, tile_rows):
    i = pl.program_id(0)
    n_used = nu_ref[0]
    e = be_ref[i]
    cur = i % 2
    ne = N_EXPERTS

    def gather(b, slot):
        eb = be_ref[b]
        r0 = r0_ref[b]
        nv = nv_ref[b]

        def seg(t, c):
            first = cumt_ref[t * ne + eb]
            lo = jnp.maximum(first, r0)
            n = jnp.minimum(cumt_ref[(t + 1) * ne + eb], r0 + nv) - lo

            @pl.when(n > 0)
            def _():
                src = t * tile_rows + soff_ref[t * ne + eb] + (lo - first)
                pltpu.make_async_copy(xls_ref.at[_tok_rows(src, n), :],
                                      xbuf_ref.at[slot, _tok_rows(lo - r0, n), :], sem.at[slot]).start()
            return c

        lax.fori_loop(t0_ref[b], t1_ref[b] + 1, seg, 0)

        @pl.when(nv < blk)
        def _():
            pltpu.make_async_copy(zero_ref.at[_tok_rows(0, blk - nv), :],
                                  xbuf_ref.at[slot, _tok_rows(nv, blk - nv), :], sem.at[slot]).start()

    def drain(slot):
        pltpu.make_async_copy(xls_ref.at[pl.ds(0, blk * SUBLANES), :], xbuf_ref.at[slot], sem.at[slot]).wait()

    @pl.when(i == 0)
    def _():
        prev_ref[0] = -1
        zero_ref[...] = jnp.zeros(zero_ref.shape, F32)

    @pl.when(jnp.logical_and(i == 0, n_used > 0))
    def _():
        gather(0, 0)

    @pl.when(i < n_used)
    def _():
        @pl.when(e != prev_ref[0])
        def _():
            wgb_ref[...] = wg_ref[...].astype(BF16)
            wub_ref[...] = wu_ref[...].astype(BF16)
            wdb_ref[...] = wd_ref[...].astype(BF16)
            prev_ref[0] = e

        drain(cur)
        xbs = [jnp.concatenate([xbuf_ref[cur, pl.ds(h * sub * SUBLANES + c, sub, stride=SUBLANES), :]
                                for c in range(SUBLANES)], axis=1).astype(BF16) for h in range(blk // sub)]
        gather(jnp.minimum(i + 1, n_used - 1), 1 - cur)
        gate = [_dot(xb, wgb_ref[...]) for xb in xbs]
        up = [_dot(xb, wub_ref[...]) for xb in xbs]
        hidden = [(g * jax.nn.sigmoid(g) * u).astype(BF16) for g, u in zip(gate, up)]
        for h, hid in enumerate(hidden):
            y_ref[h * sub:(h + 1) * sub, :] = _dot(hid, wdb_ref[...])

    @pl.when(i == n_used - 1)
    def _():
        drain(1 - cur)

    @pl.when(i >= n_used)
    def _():
        y_ref[...] = jnp.zeros(y_ref.shape, F32)


def _moe_call(tables, xls, tile_rows, w_gate, w_up, w_down):
    n_blocks = tables[0].shape[0]
    blk = EXPERT_BLOCK
    _, d, de = w_gate.shape
    wmap = lambda i, be, *_: (be[i], 0, 0)
    grid_spec = pltpu.PrefetchScalarGridSpec(
        num_scalar_prefetch=len(tables),
        grid=(n_blocks,),
        in_specs=[
            pl.BlockSpec(memory_space=pl.ANY),
            pl.BlockSpec((None, d, de), wmap),
            pl.BlockSpec((None, d, de), wmap),
            pl.BlockSpec((None, de, d), wmap),
        ],
        out_specs=pl.BlockSpec((blk, d), lambda i, *_: (i, 0)),
        scratch_shapes=[
            pltpu.VMEM((2, blk * SUBLANES, LANES), F32), pltpu.VMEM((blk * SUBLANES, LANES), F32),
            pltpu.SemaphoreType.DMA((2,)),
            pltpu.VMEM((d, de), BF16), pltpu.VMEM((d, de), BF16), pltpu.VMEM((de, d), BF16),
            pltpu.SMEM((1,), jnp.int32),
        ],
    )
    return pl.pallas_call(
        functools.partial(_moe_kernel, blk=blk, sub=SUB_TILE, tile_rows=tile_rows),
        grid_spec=grid_spec,
        out_shape=jax.ShapeDtypeStruct((n_blocks * blk, d), F32),
        compiler_params=pltpu.CompilerParams(
            dimension_semantics=("arbitrary",), vmem_limit_bytes=VMEM_LIMIT),
        name="moe",
    )(*tables, xls, w_gate, w_up, w_down)


def _combine_kernel(dest0_ref, destn_ref, h1_ref, rw_ref, y_ref, g2_ref, b2_ref, o_ref, buf_ref, sem,
                    *, tm, alpha):
    i = pl.program_id(0)
    last = pl.num_programs(0) - 1
    cur = i % 2

    def gather(dest_ref, slot):
        for r in range(tm):
            for k in range(2):
                _row_copy(y_ref, dest_ref[0, 0, k * tm + r], buf_ref.at[slot, k], r, sem.at[slot]).start(priority=k)

    def drain(slot):
        for r in range(tm):
            for k in range(2):
                _row_copy(y_ref, 0, buf_ref.at[slot, k], r, sem.at[slot]).wait()

    @pl.when(i == 0)
    def _():
        gather(dest0_ref, 0)

    gather(destn_ref, 1 - cur)
    drain(cur)
    rw = rw_ref[...]
    ffn = rw[:, 0:1] * buf_ref[cur, 0] + rw[:, 1:2] * buf_ref[cur, 1]
    o_ref[...] = _layer_norm(alpha * h1_ref[...] + ffn, g2_ref[...], b2_ref[...])

    @pl.when(i == last)
    def _():
        drain(1 - cur)


def _combine_call(dest_blk, h1, route_w, y_slots, g2, b2, alpha):
    t, d = h1.shape
    tm = ROW_TILE
    nt = t // tm
    kern = functools.partial(_combine_kernel, tm=tm, alpha=alpha)
    const = lambda i: (0, 0)
    return pl.pallas_call(
        kern,
        grid=(nt,),
        in_specs=[
            pl.BlockSpec((1, 1, 2 * tm), lambda i: (i, 0, 0), memory_space=pltpu.SMEM),
            pl.BlockSpec((1, 1, 2 * tm), lambda i: (jnp.minimum(i + 1, nt - 1), 0, 0), memory_space=pltpu.SMEM),
            pl.BlockSpec((tm, d), lambda i: (i, 0)),
            pl.BlockSpec((tm, 2), lambda i: (i, 0)),
            pl.BlockSpec(memory_space=pl.ANY),
            pl.BlockSpec((1, d), const),
            pl.BlockSpec((1, d), const),
        ],
        out_specs=pl.BlockSpec((tm, d), lambda i: (i, 0)),
        out_shape=jax.ShapeDtypeStruct((t, d), F32),
        scratch_shapes=[pltpu.VMEM((2, 2, tm, d), F32), pltpu.SemaphoreType.DMA((2,))],
        compiler_params=pltpu.CompilerParams(
            dimension_semantics=("arbitrary",), vmem_limit_bytes=VMEM_LIMIT),
        name="combine",
    )(dest_blk, dest_blk, h1, route_w, y_slots, g2, b2)


def _layer(h, w_in, gate_bias, conv_w, w_branch_a, w_branch_b, w_out, ln1_g, ln1_b,
           w_router_g, b_router_g, w_router_e, b_router_e, w_gate, w_up, w_down, ln2_g, ln2_b, alpha):
    bsz, seq, d = h.shape
    t = bsz * seq
    x2 = h.reshape(t, d)

    qkv, gate_a, gbb = _proj_call(x2, w_in.astype(BF16), gate_bias.reshape(1, 2 * d), conv_w,
                                  w_branch_b.astype(BF16), seq)
    attn = _attn_call(qkv.reshape(bsz, seq, 3 * SB_WIDTH)).reshape(t, SB_WIDTH)

    wrt = jnp.zeros((ROUTER_ROWS, d), F32).at[0:N_GROUPS].set(w_router_g.T).at[8:].set(w_router_e.T)
    brt = jnp.zeros((ROUTER_ROWS,), F32).at[0:N_GROUPS].set(b_router_g).at[8:].set(b_router_e.reshape(-1))
    brt = jnp.broadcast_to(brt[:, None], (ROUTER_ROWS, LANES))
    h1, meta, cnt, xls, tcnt = _mix_call(attn, gate_a, gbb, x2, w_branch_a.astype(BF16), w_out.astype(BF16),
                                         ln1_g.reshape(1, d), ln1_b.reshape(1, d), wrt.astype(BF16), brt, alpha)

    blk = EXPERT_BLOCK
    counts = cnt[:, 0].astype(jnp.int32)
    tile_cnt = tcnt[:, :, 0].astype(jnp.int32)
    n_tiles = tile_cnt.shape[0]
    cum_incl = jnp.cumsum(tile_cnt, axis=0)
    cumt = jnp.concatenate([jnp.zeros((1, N_EXPERTS), jnp.int32), cum_incl], axis=0)
    soff = jnp.cumsum(tile_cnt, axis=1) - tile_cnt
    padded = (counts + blk - 1) // blk * blk
    pad_end = jnp.cumsum(padded)
    pad_start = pad_end - padded
    expert = meta[0:2].astype(jnp.int32)
    rank = meta[4:6].astype(jnp.int32)
    onehot = expert[:, :, None] == jnp.arange(N_EXPERTS, dtype=jnp.int32)
    dest = jnp.sum(jnp.where(onehot, pad_start, 0), axis=-1) + rank
    n_blocks = (2 * t) // blk + N_EXPERTS
    block_start = jnp.arange(n_blocks, dtype=jnp.int32) * blk
    block_expert = jnp.minimum(jnp.sum(pad_end[None, :] <= block_start[:, None], axis=1), N_EXPERTS - 1)
    n_used = (pad_end[-1] // blk).reshape(1)
    tm = ROW_TILE
    dest_blk = dest.reshape(2, t // tm, tm).transpose(1, 0, 2).reshape(t // tm, 1, 2 * tm)

    be = block_expert.astype(jnp.int32)
    r0 = block_start - pad_start[be]
    nv = jnp.clip(counts[be] - r0, 0, blk)
    t0 = jnp.minimum(jnp.sum(cum_incl[:, be] <= r0[None, :], axis=0), n_tiles - 1)
    t1 = jnp.sum(cumt[:-1][:, be] < (r0 + nv)[None, :], axis=0) - 1
    tables = (be, r0, t0, t1, nv, n_used, cumt.reshape(-1), soff.reshape(-1))
    y_slots = _moe_call(tuple(a.astype(jnp.int32) for a in tables), xls, 2 * SUB_TILE, w_gate, w_up, w_down)
    out = _combine_call(dest_blk, h1, meta[2:4].T, y_slots, ln2_g.reshape(1, d), ln2_b.reshape(1, d), alpha)
    return out.reshape(bsz, seq, d)


def kernel(x, w_in, gate_bias, conv_w, w_branch_a, w_branch_b, w_out, ln1_g, ln1_b, w_router_g, b_router_g,
           w_router_e, b_router_e, w_gate, w_up, w_down, ln2_g, ln2_b):
    depth = w_in.shape[0]
    alpha = (2.0 * depth) ** 0.25
    h = x
    for l in range(depth):
        h = _layer(h, w_in[l], gate_bias[l], conv_w[l], w_branch_a[l], w_branch_b[l], w_out[l], ln1_g[l],
                   ln1_b[l], w_router_g[l], b_router_g[l], w_router_e[l], b_router_e[l], w_gate[l], w_up[l],
                   w_down[l], ln2_g[l], ln2_b[l], alpha)
    return h
```

```python
import functools
from typing import NamedTuple

import jax
import jax.numpy as jnp
from jax import lax
from jax.experimental import pallas as pl
from jax.experimental.pallas import tpu as pltpu

F32 = jnp.float32
BF16 = jnp.bfloat16

SB_HEADS = 8
SB_HEAD_DIM = 64
SB_WIDTH = SB_HEADS * SB_HEAD_DIM
N_GROUPS = 4
EXPERTS_PER_GROUP = 8
N_EXPERTS = N_GROUPS * EXPERTS_PER_GROUP
LN_EPS = 1e-5

LANES = 128
SUBLANES = 8
HEAD_PAIR = LANES
PROJ_TILE = 512
SUB_TILE = 256
ATTN_TILE = 128
ATTN_UNROLL = 3
ROW_TILE = 256
EXPERT_BLOCK = 512
ROUTER_ROWS = 8 + N_EXPERTS
TAIL_CUTOFF = 110.0
MASKED_LOGIT = -1e30
LOG2E = 1.4426950408889634
VMEM_LIMIT = 56 * 1024 * 1024


def _dot(a, b):
    return jnp.dot(a, b, preferred_element_type=F32)


def _dot_nt(a, b):
    return lax.dot_general(a, b, (((1,), (1,)), ((), ())), preferred_element_type=F32)


def _dot_tn(a, b):
    return lax.dot_general(a, b, (((0,), (0,)), ((), ())), preferred_element_type=F32)


def _layer_norm(y, g, b):
    mu = jnp.mean(y, axis=-1, keepdims=True)
    d = y - mu
    var = jnp.mean(d * d, axis=-1, keepdims=True)
    return d * lax.rsqrt(var + LN_EPS) * g + b


def _proj_kernel(x_ref, w_ref, gbias_ref, convw_ref, wbb_ref, qkv_ref, ga_ref, gbb_ref, ubuf_ref,
                 *, tm, sub, tiles_per_seq, d_model):
    i = pl.program_id(0)
    cw = SB_WIDTH
    c0 = 3 * SB_WIDTH
    g0 = c0 + 3 * cw
    cwt = convw_ref[...]

    @pl.when(i % tiles_per_seq == 0)
    def _():
        ubuf_ref[0:8, :] = jnp.zeros((8, cw), F32)

    chunks = range(tm // sub)
    rows = [slice(h * sub, (h + 1) * sub) for h in chunks]
    xb = [x_ref[rows[h], :].astype(BF16) for h in chunks]
    conv_in = [_dot(xb[h], w_ref[:, c0:g0]) for h in chunks]
    qkv = [_dot(xb[h], w_ref[:, 0:c0]) for h in chunks]
    gated = []
    for h in chunks:
        cb = conv_in[h][:, 0:cw]
        u = conv_in[h][:, cw:2 * cw] * conv_in[h][:, 2 * cw:3 * cw]
        ubuf_ref[8:sub + 8, :] = u
        y = (cwt[0:1, :] * ubuf_ref[pl.ds(6, sub), :] + cwt[1:2, :] * ubuf_ref[pl.ds(7, sub), :]
             + cwt[2:3, :] * u)
        ubuf_ref[0:8, :] = u[sub - 8:sub, :]
        gated.append((cb * y).astype(BF16))
        qkv_ref[rows[h], 0:SB_WIDTH] = (qkv[h][:, 0:SB_WIDTH] * (SB_HEAD_DIM ** -0.5)).astype(BF16)
        qkv_ref[rows[h], SB_WIDTH:c0] = qkv[h][:, SB_WIDTH:c0].astype(BF16)
    gate_logits = [_dot(xb[h], w_ref[:, g0:g0 + 2 * d_model]) for h in chunks]
    branch_b = [_dot(gated[h], wbb_ref[...]) for h in chunks]
    for h in chunks:
        gates = jax.nn.sigmoid(gate_logits[h] + gbias_ref[...])
        ga_ref[rows[h], :] = gates[:, 0:d_model]
        gbb_ref[rows[h], :] = gates[:, d_model:2 * d_model] * branch_b[h]


def _proj_call(x2, w_in_b, gbias, conv_w, wbb_b, seq):
    t, d = x2.shape
    pw = w_in_b.shape[1]
    tm = PROJ_TILE
    sub = SUB_TILE
    cw = conv_w.shape[1]
    kern = functools.partial(_proj_kernel, tm=tm, sub=sub, tiles_per_seq=seq // tm, d_model=d)
    const = lambda i: (0, 0)
    return pl.pallas_call(
        kern,
        grid=(t // tm,),
        in_specs=[
            pl.BlockSpec((tm, d), lambda i: (i, 0)),
            pl.BlockSpec((d, pw), const),
            pl.BlockSpec((1, 2 * d), const),
            pl.BlockSpec((3, cw), const),
            pl.BlockSpec((cw, d), const),
        ],
        out_specs=[
            pl.BlockSpec((tm, 3 * SB_WIDTH), lambda i: (i, 0)),
            pl.BlockSpec((tm, d), lambda i: (i, 0)),
            pl.BlockSpec((tm, d), lambda i: (i, 0)),
        ],
        out_shape=[
            jax.ShapeDtypeStruct((t, 3 * SB_WIDTH), BF16),
            jax.ShapeDtypeStruct((t, d), F32),
            jax.ShapeDtypeStruct((t, d), F32),
        ],
        scratch_shapes=[pltpu.VMEM((sub + 8, cw), F32)],
        compiler_params=pltpu.CompilerParams(
            dimension_semantics=("arbitrary",), vmem_limit_bytes=VMEM_LIMIT),
        name="proj",
    )(x2, w_in_b, gbias, conv_w, wbb_b)


def _attn_kernel(q_ref, k_ref, v_ref, o_ref, acc_ref, c_ref, *, ts, unroll):
    qi = pl.program_id(1)
    lane = lax.broadcasted_iota(jnp.int32, (ts, HEAD_PAIR), 1)
    row = lax.broadcasted_iota(jnp.int32, (ts, ts), 0)
    col = lax.broadcasted_iota(jnp.int32, (ts, ts), 1)
    later = (row > col).astype(BF16)
    later2 = jnp.concatenate([later, later], axis=0)
    ahead = col - row
    reps = ts // LANES
    zero_b = jnp.zeros((ts, HEAD_PAIR), BF16)
    halves = HEAD_PAIR // SB_HEAD_DIM
    in_head = [(lane >= h * SB_HEAD_DIM) & (lane < (h + 1) * SB_HEAD_DIM) for h in range(halves)]

    acc_ref[...] = jnp.zeros(acc_ref.shape, F32)
    c_ref[...] = jnp.zeros(c_ref.shape, F32)

    pairs = range(SB_WIDTH // HEAD_PAIR)
    cols = [slice(p * HEAD_PAIR, (p + 1) * HEAD_PAIR) for p in pairs]

    def sweep(j):
        start, bias = [], []
        for k in range(unroll):
            jj = j - k
            start.append(pl.multiple_of(jnp.maximum(jj, 0) * ts, ts))
            limit = jnp.where(jj >= 0, (qi - jj) * ts, -2 * ts)
            bias.append(jnp.where(ahead < limit, 0.0, MASKED_LOGIT))
        blocks = range(unroll)
        z, sp, split, inner, wb = {}, {}, {}, {}, {}

        def logits(head):
            p, h = divmod(head, halves)
            q = jnp.where(in_head[h], q_ref[:, cols[p]], zero_b)
            for k in blocks:
                z[head, k] = _dot_nt(q, k_ref[pl.ds(start[k], ts), cols[p]]) + bias[k]

        def stay(head):
            for k in blocks:
                zz = z[head, k]
                s = jnp.maximum(zz, 0.0) + jnp.log(1.0 + jnp.exp2(jnp.abs(zz) * (-LOG2E)))
                hi = s.astype(BF16)
                sp[head, k] = s
                split[head, k] = jnp.concatenate([hi, (s - hi.astype(F32)).astype(BF16)], axis=1)

        def tails(head):
            for k in blocks:
                inner[head, k] = _dot(split[head, k], later2)

        def weights(head):
            cvec = c_ref[head]
            for k in blocks:
                c = (head, k)
                carry = jnp.concatenate([cvec] * reps, axis=1)
                wb[c] = jnp.exp(z[c] - sp[c] - inner[c] - carry).astype(BF16)
                total = inner[c][:, 0:1] + sp[c][:, 0:1]
                cvec = cvec + jnp.broadcast_to(total, (ts, LANES))
            c_ref[head] = cvec

        def values(head):
            p, h = divmod(head, halves)
            out = None
            for k in blocks:
                vb = jnp.where(in_head[h], v_ref[pl.ds(start[k], ts), cols[p]], zero_b)
                pv = _dot(wb[head, k], vb)
                out = pv if out is None else out + pv
            acc_ref[:, cols[p]] += out

        stages = (logits, stay, tails, weights, values)
        for step in range(SB_HEADS + len(stages) - 1):
            for s in reversed(range(len(stages))):
                if 0 <= step - s < SB_HEADS:
                    stages[s](step - s)

    def cond(carry):
        j, cmin = carry
        return jnp.logical_and(j >= 0, cmin < TAIL_CUTOFF)

    def body(carry):
        j, _ = carry
        sweep(j)
        return j - unroll, jnp.min(c_ref[...])

    lax.while_loop(cond, body, (qi, jnp.float32(0.0)))
    o_ref[...] = acc_ref[...].astype(BF16)


def _attn_call(qkv3):
    b, s, _ = qkv3.shape
    ts = ATTN_TILE
    kern = functools.partial(_attn_kernel, ts=ts, unroll=ATTN_UNROLL)
    return pl.pallas_call(
        kern,
        grid=(b, s // ts),
        in_specs=[
            pl.BlockSpec((None, ts, SB_WIDTH), lambda bi, qi: (bi, qi, 0)),
            pl.BlockSpec((None, s, SB_WIDTH), lambda bi, qi: (bi, 0, 1)),
            pl.BlockSpec((None, s, SB_WIDTH), lambda bi, qi: (bi, 0, 2)),
        ],
        out_specs=pl.BlockSpec((None, ts, SB_WIDTH), lambda bi, qi: (bi, qi, 0)),
        out_shape=jax.ShapeDtypeStruct((b, s, SB_WIDTH), BF16),
        scratch_shapes=[pltpu.VMEM((ts, SB_WIDTH), F32), pltpu.VMEM((SB_HEADS, ts, LANES), F32)],
        compiler_params=pltpu.CompilerParams(
            dimension_semantics=("arbitrary", "arbitrary"), vmem_limit_bytes=VMEM_LIMIT),
        name="attn",
    )(qkv3, qkv3, qkv3)


def _mix_kernel(attn_ref, ga_ref, gbb_ref, x_ref, wa_ref, wo_ref, g1_ref, b1_ref, wrt_ref, brt_ref,
                h1_ref, meta_ref, xls_ref, tcnt_ref, *, tile, tm, alpha):
    chunks = range(tile // tm)
    reps = tm // LANES
    rows = [slice(h * tm, (h + 1) * tm) for h in chunks]
    branch_a = [_dot(attn_ref[rows[h], :], wa_ref[...]) for h in chunks]
    mixed = [_dot((ga_ref[rows[h], :] * branch_a[h] + gbb_ref[rows[h], :]).astype(BF16), wo_ref[...])
             for h in chunks]
    h1b = []
    for h in chunks:
        h1 = _layer_norm(alpha * x_ref[rows[h], :] + mixed[h], g1_ref[...], b1_ref[...])
        h1_ref[rows[h], :] = h1
        h1b.append(h1.astype(BF16))
    bias = jnp.concatenate([brt_ref[...]] * reps, axis=1)
    logits = [_dot_nt(wrt_ref[...], h1b[h]) + bias for h in chunks]
    route = [_route(logits[h], tm) for h in chunks]

    trow = lax.broadcasted_iota(jnp.int32, (tm, tm), 0)
    tcol = lax.broadcasted_iota(jnp.int32, (tm, tm), 1)
    earlier = (trow < tcol).astype(BF16)
    e32r = lax.broadcasted_iota(jnp.int32, (N_EXPERTS, N_EXPERTS), 0)
    e32c = lax.broadcasted_iota(jnp.int32, (N_EXPERTS, N_EXPERTS), 1)
    lower = (e32c < e32r).astype(BF16)
    ones_b = jnp.ones((tm, LANES), BF16)
    before_tile = [_dot(route[h].onehot, earlier) for h in chunks]
    tile_cnt = [_dot(route[h].onehot, ones_b) for h in chunks]
    smaller = [_dot(lower, route[h].onehot).astype(BF16) for h in chunks]
    seg_off = [_dot(smaller[h], ones_b) for h in chunks]

    zrow = jnp.zeros((1, tm), F32)
    perm = []
    for h in chunks:
        rt = route[h]
        tcnt_ref[h] = tile_cnt[h]
        local = before_tile[h] + jnp.concatenate([seg_off[h]] * reps, axis=1)
        pos1 = jnp.sum(jnp.where(rt.is1, local, 0.0), axis=0, keepdims=True)
        pos2 = jnp.sum(jnp.where(rt.is2, local, 0.0), axis=0, keepdims=True)
        meta_ref[:, rows[h]] = jnp.concatenate(
            [rt.e1.astype(F32), rt.e2.astype(F32), rt.w1, rt.w2, pos1, pos2, zrow, zrow], axis=0)
        srow = lax.broadcasted_iota(jnp.int32, (2 * tm, tm), 0)
        perm.append(jnp.where((srow == pos1.astype(jnp.int32)) | (srow == pos2.astype(jnp.int32)), 1.0, 0.0)
                    .astype(BF16))
    xs = [_dot(perm[h], h1b[h]) for h in chunks]
    for h in chunks:
        base = h * 2 * tm * SUBLANES
        for c in range(xs[h].shape[1] // LANES):
            xls_ref[pl.ds(base + c, 2 * tm, stride=SUBLANES), :] = xs[h][:, c * LANES:(c + 1) * LANES]


class _Route(NamedTuple):
    e1: jax.Array
    e2: jax.Array
    w1: jax.Array
    w2: jax.Array
    is1: jax.Array
    is2: jax.Array
    onehot: jax.Array


def _route(lt, tm):
    r = [lt[k:k + 1, :] for k in range(N_GROUPS)]
    gmax = jnp.maximum(jnp.maximum(r[0], r[1]), jnp.maximum(r[2], r[3]))
    gidx = jnp.where(r[0] == gmax, 0, jnp.where(r[1] == gmax, 1, jnp.where(r[2] == gmax, 2, 3)))
    gsum = (jnp.exp(r[0] - gmax) + jnp.exp(r[1] - gmax)) + (jnp.exp(r[2] - gmax) + jnp.exp(r[3] - gmax))
    gprob = 1.0 / gsum
    epg = EXPERTS_PER_GROUP
    slabs = [lt[8 + g * epg:8 + (g + 1) * epg, :] for g in range(N_GROUPS)]
    el = jnp.where(gidx == 0, slabs[0], jnp.where(gidx == 1, slabs[1], jnp.where(gidx == 2, slabs[2], slabs[3])))
    r8 = lax.broadcasted_iota(jnp.int32, (epg, tm), 0)
    m1 = jnp.max(el, axis=0, keepdims=True)
    i1 = jnp.min(jnp.where(el == m1, r8, epg), axis=0, keepdims=True)
    el2 = jnp.where(r8 == i1, -jnp.inf, el)
    m2 = jnp.max(el2, axis=0, keepdims=True)
    i2 = jnp.min(jnp.where(el2 == m2, r8, epg), axis=0, keepdims=True)
    dlt = jnp.exp(m2 - m1)
    w1 = gprob / (1.0 + dlt)
    w2 = gprob * dlt / (1.0 + dlt)
    e1 = gidx * epg + i1
    e2 = gidx * epg + i2
    r32 = lax.broadcasted_iota(jnp.int32, (N_EXPERTS, tm), 0)
    is1 = r32 == e1
    is2 = r32 == e2
    onehot = jnp.where(is1 | is2, 1.0, 0.0).astype(BF16)
    return _Route(e1, e2, w1, w2, is1, is2, onehot)


def _mix_call(attn2, gate_a, gbb, x2, wa_b, wo_b, g1, b1, wrt_b, brt, alpha):
    t, d = x2.shape
    tile = PROJ_TILE
    kern = functools.partial(_mix_kernel, tile=tile, tm=SUB_TILE, alpha=alpha)
    const = lambda i: (0, 0)
    rowblk = lambda i: (i, 0)
    tm = tile
    return pl.pallas_call(
        kern,
        grid=(t // tm,),
        in_specs=[
            pl.BlockSpec((tm, SB_WIDTH), rowblk),
            pl.BlockSpec((tm, d), rowblk),
            pl.BlockSpec((tm, d), rowblk),
            pl.BlockSpec((tm, d), rowblk),
            pl.BlockSpec((SB_WIDTH, d), const),
            pl.BlockSpec((d, d), const),
            pl.BlockSpec((1, d), const),
            pl.BlockSpec((1, d), const),
            pl.BlockSpec((ROUTER_ROWS, d), const),
            pl.BlockSpec((ROUTER_ROWS, LANES), const),
        ],
        out_specs=[
            pl.BlockSpec((tm, d), rowblk),
            pl.BlockSpec((8, tm), lambda i: (0, i)),
            pl.BlockSpec((2 * tm * SUBLANES, LANES), rowblk),
            pl.BlockSpec((tile // SUB_TILE, N_EXPERTS, LANES), lambda i: (i, 0, 0)),
        ],
        out_shape=[
            jax.ShapeDtypeStruct((t, d), F32),
            jax.ShapeDtypeStruct((8, t), F32),
            jax.ShapeDtypeStruct((2 * t * SUBLANES, LANES), F32),
            jax.ShapeDtypeStruct((t // SUB_TILE, N_EXPERTS, LANES), F32),
        ],
        compiler_params=pltpu.CompilerParams(
            dimension_semantics=("arbitrary",), vmem_limit_bytes=VMEM_LIMIT),
        name="mix",
    )(attn2, gate_a, gbb, x2, wa_b, wo_b, g1, b1, wrt_b, brt)


def _row_copy(src_ref, src_row, dst_ref, dst_row, sem):
    return pltpu.make_async_copy(src_ref.at[pl.ds(src_row, 1), :], dst_ref.at[pl.ds(dst_row, 1), :], sem)


def _tok_rows(tok, n_tok):
    return pl.ds(pl.multiple_of(tok * SUBLANES, SUBLANES), pl.multiple_of(n_tok * SUBLANES, SUBLANES))


def _moe_kernel(be_ref, r0_ref, t0_ref, t1_ref, nv_ref, nu_ref, cumt_ref, soff_ref,
                xls_ref, wg_ref, wu_ref, wd_ref, y_ref,
                xbuf_ref, zero_ref, sem, wgb_ref, wub_ref, wdb_ref, prev_ref, *, blk, sub, tile_rows):
    i = pl.program_id(0)
    n_used = nu_ref[0]
    e = be_ref[i]
    cur = i % 2
    ne = N_EXPERTS

    def gather(b, slot):
        eb = be_ref[b]
        r0 = r0_ref[b]
        nv = nv_ref[b]

        def seg(t, c):
            first = cumt_ref[t * ne + eb]
            lo = jnp.maximum(first, r0)
            n = jnp.minimum(cumt_ref[(t + 1) * ne + eb], r0 + nv) - lo

            @pl.when(n > 0)
            def _():
                src = t * tile_rows + soff_ref[t * ne + eb] + (lo - first)
                pltpu.make_async_copy(xls_ref.at[_tok_rows(src, n), :],
                                      xbuf_ref.at[slot, _tok_rows(lo - r0, n), :], sem.at[slot]).start()
            return c

        lax.fori_loop(t0_ref[b], t1_ref[b] + 1, seg, 0)

        @pl.when(nv < blk)
        def _():
            pltpu.make_async_copy(zero_ref.at[_tok_rows(0, blk - nv), :],
                                  xbuf_ref.at[slot, _tok_rows(nv, blk - nv), :], sem.at[slot]).start()

    def drain(slot):
        pltpu.make_async_copy(xls_ref.at[pl.ds(0, blk * SUBLANES), :], xbuf_ref.at[slot], sem.at[slot]).wait()

    @pl.when(i == 0)
    def _():
        prev_ref[0] = -1
        zero_ref[...] = jnp.zeros(zero_ref.shape, F32)

    @pl.when(jnp.logical_and(i == 0, n_used > 0))
    def _():
        gather(0, 0)

    @pl.when(i < n_used)
    def _():
        @pl.when(e != prev_ref[0])
        def _():
            wgb_ref[...] = wg_ref[...].astype(BF16)
            wub_ref[...] = wu_ref[...].astype(BF16)
            wdb_ref[...] = wd_ref[...].astype(BF16)
            prev_ref[0] = e

        gather(jnp.minimum(i + 1, n_used - 1), 1 - cur)
        drain(cur)
        xbs = [jnp.concatenate([xbuf_ref[cur, pl.ds(h * sub * SUBLANES + c, sub, stride=SUBLANES), :]
                                for c in range(SUBLANES)], axis=1).astype(BF16) for h in range(blk // sub)]
        gate = [_dot(xb, wgb_ref[...]) for xb in xbs]
        up = [_dot(xb, wub_ref[...]) for xb in xbs]
        hidden = [(g * jax.nn.sigmoid(g) * u).astype(BF16) for g, u in zip(gate, up)]
        for h, hid in enumerate(hidden):
            y = _dot(hid, wdb_ref[...])
            for c in range(SUBLANES):
                y_ref[pl.ds(h * sub * SUBLANES + c, sub, stride=SUBLANES), :] = y[:, c * LANES:(c + 1) * LANES]

    @pl.when(i == n_used - 1)
    def _():
        drain(1 - cur)

    @pl.when(i >= n_used)
    def _():
        y_ref[...] = jnp.zeros(y_ref.shape, F32)


def _moe_call(tables, xls, tile_rows, w_gate, w_up, w_down):
    n_blocks = tables[0].shape[0]
    blk = EXPERT_BLOCK
    _, d, de = w_gate.shape
    wmap = lambda i, be, *_: (be[i], 0, 0)
    grid_spec = pltpu.PrefetchScalarGridSpec(
        num_scalar_prefetch=len(tables),
        grid=(n_blocks,),
        in_specs=[
            pl.BlockSpec(memory_space=pl.ANY),
            pl.BlockSpec((None, d, de), wmap),
            pl.BlockSpec((None, d, de), wmap),
            pl.BlockSpec((None, de, d), wmap),
        ],
        out_specs=pl.BlockSpec((blk * SUBLANES, LANES), lambda i, *_: (i, 0)),
        scratch_shapes=[
            pltpu.VMEM((2, blk * SUBLANES, LANES), F32), pltpu.VMEM((blk * SUBLANES, LANES), F32),
            pltpu.SemaphoreType.DMA((2,)),
            pltpu.VMEM((d, de), BF16), pltpu.VMEM((d, de), BF16), pltpu.VMEM((de, d), BF16),
            pltpu.SMEM((1,), jnp.int32),
        ],
    )
    return pl.pallas_call(
        functools.partial(_moe_kernel, blk=blk, sub=SUB_TILE, tile_rows=tile_rows),
        grid_spec=grid_spec,
        out_shape=jax.ShapeDtypeStruct((n_blocks * blk * SUBLANES, LANES), F32),
        compiler_params=pltpu.CompilerParams(
            dimension_semantics=("arbitrary",), vmem_limit_bytes=VMEM_LIMIT),
        name="moe",
    )(*tables, xls, w_gate, w_up, w_down)


def _combine_kernel(pstart_ref, cumt_ref, soff_ref, tcnt_ref,
                    h1_ref, meta_ref, rw_ref, y_ref, g2_ref, b2_ref, o_ref, ybuf_ref, sem, *, tm, alpha):
    i = pl.program_id(0)
    last = pl.num_programs(0) - 1
    cur = i % 2
    ne = N_EXPERTS

    def gather(t, slot):
        def seg(e, c):
            n = tcnt_ref[t * ne + e]

            @pl.when(n > 0)
            def _():
                src = pstart_ref[e] + cumt_ref[t * ne + e]
                pltpu.make_async_copy(y_ref.at[_tok_rows(src, n), :],
                                      ybuf_ref.at[slot, _tok_rows(soff_ref[t * ne + e], n), :],
                                      sem.at[slot]).start()
            return c

        lax.fori_loop(0, ne, seg, 0)

    def drain(slot):
        pltpu.make_async_copy(y_ref.at[pl.ds(0, 2 * tm * SUBLANES), :], ybuf_ref.at[slot], sem.at[slot]).wait()

    @pl.when(i == 0)
    def _():
        gather(0, 0)

    gather(jnp.minimum(i + 1, last), 1 - cur)
    drain(cur)
    ys = jnp.concatenate([ybuf_ref[cur, pl.ds(c, 2 * tm, stride=SUBLANES), :] for c in range(SUBLANES)],
                         axis=1).astype(BF16)
    srow = lax.broadcasted_iota(jnp.int32, (2 * tm, tm), 0)
    rw = rw_ref[...]
    ffn = None
    for k in range(2):
        pos = meta_ref[4 + k:5 + k, :].astype(jnp.int32)
        sel = jnp.where(srow == pos, 1.0, 0.0).astype(BF16)
        term = rw[:, k:k + 1] * _dot_tn(sel, ys)
        ffn = term if ffn is None else ffn + term
    o_ref[...] = _layer_norm(alpha * h1_ref[...] + ffn, g2_ref[...], b2_ref[...])

    @pl.when(i == last)
    def _():
        drain(1 - cur)


def _combine_call(tables, h1, meta, route_w, y_slots, g2, b2, alpha):
    t, d = h1.shape
    tm = SUB_TILE
    kern = functools.partial(_combine_kernel, tm=tm, alpha=alpha)
    const = lambda i, *_: (0, 0)
    grid_spec = pltpu.PrefetchScalarGridSpec(
        num_scalar_prefetch=len(tables),
        grid=(t // tm,),
        in_specs=[
            pl.BlockSpec((tm, d), lambda i, *_: (i, 0)),
            pl.BlockSpec((8, tm), lambda i, *_: (0, i)),
            pl.BlockSpec((tm, 2), lambda i, *_: (i, 0)),
            pl.BlockSpec(memory_space=pl.ANY),
            pl.BlockSpec((1, d), const),
            pl.BlockSpec((1, d), const),
        ],
        out_specs=pl.BlockSpec((tm, d), lambda i, *_: (i, 0)),
        scratch_shapes=[pltpu.VMEM((2, 2 * tm * SUBLANES, LANES), F32), pltpu.SemaphoreType.DMA((2,))],
    )
    return pl.pallas_call(
        kern,
        grid_spec=grid_spec,
        out_shape=jax.ShapeDtypeStruct((t, d), F32),
        compiler_params=pltpu.CompilerParams(
            dimension_semantics=("arbitrary",), vmem_limit_bytes=VMEM_LIMIT),
        name="combine",
    )(*tables, h1, meta, route_w, y_slots, g2, b2)


def _layer(h, w_in, gate_bias, conv_w, w_branch_a, w_branch_b, w_out, ln1_g, ln1_b,
           w_router_g, b_router_g, w_router_e, b_router_e, w_gate, w_up, w_down, ln2_g, ln2_b, alpha):
    bsz, seq, d = h.shape
    t = bsz * seq
    x2 = h.reshape(t, d)

    qkv, gate_a, gbb = _proj_call(x2, w_in.astype(BF16), gate_bias.reshape(1, 2 * d), conv_w,
                                  w_branch_b.astype(BF16), seq)
    attn = _attn_call(qkv.reshape(bsz, seq, 3 * SB_WIDTH)).reshape(t, SB_WIDTH)

    wrt = jnp.zeros((ROUTER_ROWS, d), F32).at[0:N_GROUPS].set(w_router_g.T).at[8:].set(w_router_e.T)
    brt = jnp.zeros((ROUTER_ROWS,), F32).at[0:N_GROUPS].set(b_router_g).at[8:].set(b_router_e.reshape(-1))
    brt = jnp.broadcast_to(brt[:, None], (ROUTER_ROWS, LANES))
    h1, meta, xls, tcnt = _mix_call(attn, gate_a, gbb, x2, w_branch_a.astype(BF16), w_out.astype(BF16),
                                    ln1_g.reshape(1, d), ln1_b.reshape(1, d), wrt.astype(BF16), brt, alpha)

    blk = EXPERT_BLOCK
    tile_cnt = tcnt[:, :, 0].astype(jnp.int32)
    n_tiles = tile_cnt.shape[0]
    cum_incl = jnp.cumsum(tile_cnt, axis=0)
    cumt = jnp.concatenate([jnp.zeros((1, N_EXPERTS), jnp.int32), cum_incl], axis=0)
    counts = cumt[-1]
    soff = jnp.cumsum(tile_cnt, axis=1) - tile_cnt
    padded = (counts + blk - 1) // blk * blk
    pad_end = jnp.cumsum(padded)
    pad_start = pad_end - padded
    n_blocks = (2 * t) // blk + N_EXPERTS
    block_start = jnp.arange(n_blocks, dtype=jnp.int32) * blk
    block_expert = jnp.minimum(jnp.sum(pad_end[None, :] <= block_start[:, None], axis=1), N_EXPERTS - 1)
    n_used = (pad_end[-1] // blk).reshape(1)

    be = block_expert.astype(jnp.int32)
    r0 = block_start - pad_start[be]
    nv = jnp.clip(counts[be] - r0, 0, blk)
    t0 = jnp.minimum(jnp.sum(cum_incl[:, be] <= r0[None, :], axis=0), n_tiles - 1)
    t1 = jnp.sum(cumt[:-1][:, be] < (r0 + nv)[None, :], axis=0) - 1
    tables = (be, r0, t0, t1, nv, n_used, cumt.reshape(-1), soff.reshape(-1))
    y_slots = _moe_call(tuple(a.astype(jnp.int32) for a in tables), xls, 2 * SUB_TILE, w_gate, w_up, w_down)
    ctables = (pad_start, cumt.reshape(-1), soff.reshape(-1), tile_cnt.reshape(-1))
    out = _combine_call(tuple(a.astype(jnp.int32) for a in ctables), h1, meta, meta[2:4].T, y_slots,
                        ln2_g.reshape(1, d), ln2_b.reshape(1, d), alpha)
    return out.reshape(bsz, seq, d)


def kernel(x, w_in, gate_bias, conv_w, w_branch_a, w_branch_b, w_out, ln1_g, ln1_b, w_router_g, b_router_g,
           w_router_e, b_router_e, w_gate, w_up, w_down, ln2_g, ln2_b):
    depth = w_in.shape[0]
    alpha = (2.0 * depth) ** 0.25
    h = x
    for l in range(depth):
        h = _layer(h, w_in[l], gate_bias[l], conv_w[l], w_branch_a[l], w_branch_b[l], w_out[l], ln1_g[l],
                   ln1_b[l], w_router_g[l], b_router_g[l], w_router_e[l], b_router_e[l], w_gate[l], w_up[l],
                   w_down[l], ln2_g[l], ln2_b[l], alpha)
    return h
```

```python
import functools
from typing import NamedTuple

import jax
import jax.numpy as jnp
from jax import lax
from jax.experimental import pallas as pl
from jax.experimental.pallas import tpu as pltpu

F32 = jnp.float32
BF16 = jnp.bfloat16

SB_HEADS = 8
SB_HEAD_DIM = 64
SB_WIDTH = SB_HEADS * SB_HEAD_DIM
N_GROUPS = 4
EXPERTS_PER_GROUP = 8
N_EXPERTS = N_GROUPS * EXPERTS_PER_GROUP
LN_EPS = 1e-5

LANES = 128
SUBLANES = 8
HEAD_PAIR = LANES
PROJ_TILE = 512
SUB_TILE = 256
ATTN_TILE = 128
ATTN_UNROLL = 3
ROW_TILE = 256
EXPERT_BLOCK = 512
ROUTER_ROWS = 8 + N_EXPERTS
TAIL_CUTOFF = 110.0
MASKED_LOGIT = -1e30
LOG2E = 1.4426950408889634
VMEM_LIMIT = 56 * 1024 * 1024


def _dot(a, b):
    return jnp.dot(a, b, preferred_element_type=F32)


def _dot_nt(a, b):
    return lax.dot_general(a, b, (((1,), (1,)), ((), ())), preferred_element_type=F32)


def _dot_tn(a, b):
    return lax.dot_general(a, b, (((0,), (0,)), ((), ())), preferred_element_type=F32)


def _layer_norm(y, g, b):
    mu = jnp.mean(y, axis=-1, keepdims=True)
    d = y - mu
    var = jnp.mean(d * d, axis=-1, keepdims=True)
    return d * lax.rsqrt(var + LN_EPS) * g + b


def _proj_kernel(x_ref, w_ref, gbias_ref, convw_ref, wbb_ref, qkv_ref, ga_ref, gbb_ref, ubuf_ref,
                 *, tm, sub, tiles_per_seq, d_model):
    i = pl.program_id(0)
    cw = SB_WIDTH
    c0 = 3 * SB_WIDTH
    g0 = c0 + 3 * cw
    cwt = convw_ref[...]

    @pl.when(i % tiles_per_seq == 0)
    def _():
        ubuf_ref[0:8, :] = jnp.zeros((8, cw), F32)

    chunks = range(tm // sub)
    rows = [slice(h * sub, (h + 1) * sub) for h in chunks]
    xb = [x_ref[rows[h], :].astype(BF16) for h in chunks]
    conv_in = [_dot(xb[h], w_ref[:, c0:g0]) for h in chunks]
    qkv = [_dot(xb[h], w_ref[:, 0:c0]) for h in chunks]
    gated = []
    for h in chunks:
        cb = conv_in[h][:, 0:cw]
        u = conv_in[h][:, cw:2 * cw] * conv_in[h][:, 2 * cw:3 * cw]
        ubuf_ref[8:sub + 8, :] = u
        y = (cwt[0:1, :] * ubuf_ref[pl.ds(6, sub), :] + cwt[1:2, :] * ubuf_ref[pl.ds(7, sub), :]
             + cwt[2:3, :] * u)
        ubuf_ref[0:8, :] = u[sub - 8:sub, :]
        gated.append((cb * y).astype(BF16))
        qkv_ref[rows[h], 0:SB_WIDTH] = (qkv[h][:, 0:SB_WIDTH] * (SB_HEAD_DIM ** -0.5)).astype(BF16)
        qkv_ref[rows[h], SB_WIDTH:c0] = qkv[h][:, SB_WIDTH:c0].astype(BF16)
    gate_logits = [_dot(xb[h], w_ref[:, g0:g0 + 2 * d_model]) for h in chunks]
    branch_b = [_dot(gated[h], wbb_ref[...]) for h in chunks]
    for h in chunks:
        gates = jax.nn.sigmoid(gate_logits[h] + gbias_ref[...])
        ga_ref[rows[h], :] = gates[:, 0:d_model].astype(BF16)
        gbb_ref[rows[h], :] = (gates[:, d_model:2 * d_model] * branch_b[h]).astype(BF16)


def _proj_call(x2, w_in_b, gbias, conv_w, wbb_b, seq):
    t, d = x2.shape
    pw = w_in_b.shape[1]
    tm = PROJ_TILE
    sub = SUB_TILE
    cw = conv_w.shape[1]
    kern = functools.partial(_proj_kernel, tm=tm, sub=sub, tiles_per_seq=seq // tm, d_model=d)
    const = lambda i: (0, 0)
    return pl.pallas_call(
        kern,
        grid=(t // tm,),
        in_specs=[
            pl.BlockSpec((tm, d), lambda i: (i, 0)),
            pl.BlockSpec((d, pw), const),
            pl.BlockSpec((1, 2 * d), const),
            pl.BlockSpec((3, cw), const),
            pl.BlockSpec((cw, d), const),
        ],
        out_specs=[
            pl.BlockSpec((tm, 3 * SB_WIDTH), lambda i: (i, 0)),
            pl.BlockSpec((tm, d), lambda i: (i, 0)),
            pl.BlockSpec((tm, d), lambda i: (i, 0)),
        ],
        out_shape=[
            jax.ShapeDtypeStruct((t, 3 * SB_WIDTH), BF16),
            jax.ShapeDtypeStruct((t, d), BF16),
            jax.ShapeDtypeStruct((t, d), BF16),
        ],
        scratch_shapes=[pltpu.VMEM((sub + 8, cw), F32)],
        compiler_params=pltpu.CompilerParams(
            dimension_semantics=("arbitrary",), vmem_limit_bytes=VMEM_LIMIT),
        name="proj",
    )(x2, w_in_b, gbias, conv_w, wbb_b)


def _attn_kernel(q_ref, k_ref, v_ref, o_ref, acc_ref, c_ref, *, ts, unroll):
    qi = pl.program_id(1)
    lane = lax.broadcasted_iota(jnp.int32, (ts, HEAD_PAIR), 1)
    row = lax.broadcasted_iota(jnp.int32, (ts, ts), 0)
    col = lax.broadcasted_iota(jnp.int32, (ts, ts), 1)
    later = (row > col).astype(BF16)
    later2 = jnp.concatenate([later, later], axis=0)
    ahead = col - row
    reps = ts // LANES
    zero_b = jnp.zeros((ts, HEAD_PAIR), BF16)
    halves = HEAD_PAIR // SB_HEAD_DIM
    in_head = [(lane >= h * SB_HEAD_DIM) & (lane < (h + 1) * SB_HEAD_DIM) for h in range(halves)]

    acc_ref[...] = jnp.zeros(acc_ref.shape, F32)
    c_ref[...] = jnp.zeros(c_ref.shape, F32)

    pairs = range(SB_WIDTH // HEAD_PAIR)
    cols = [slice(p * HEAD_PAIR, (p + 1) * HEAD_PAIR) for p in pairs]

    def sweep(j):
        start, bias = [], []
        for k in range(unroll):
            jj = j - k
            start.append(pl.multiple_of(jnp.maximum(jj, 0) * ts, ts))
            limit = jnp.where(jj >= 0, (qi - jj) * ts, -2 * ts)
            bias.append(jnp.where(ahead < limit, 0.0, MASKED_LOGIT))
        blocks = range(unroll)
        z, sp, split, inner, wb = {}, {}, {}, {}, {}

        def logits(head):
            p, h = divmod(head, halves)
            q = jnp.where(in_head[h], q_ref[:, cols[p]], zero_b)
            for k in blocks:
                z[head, k] = _dot_nt(q, k_ref[pl.ds(start[k], ts), cols[p]]) + bias[k]

        def stay(head):
            for k in blocks:
                zz = z[head, k]
                s = jnp.maximum(zz, 0.0) + jnp.log(1.0 + jnp.exp2(jnp.abs(zz) * (-LOG2E)))
                hi = s.astype(BF16)
                sp[head, k] = s
                split[head, k] = jnp.concatenate([hi, (s - hi.astype(F32)).astype(BF16)], axis=1)

        def tails(head):
            for k in blocks:
                inner[head, k] = _dot(split[head, k], later2)

        def weights(head):
            cvec = c_ref[head]
            for k in blocks:
                c = (head, k)
                carry = jnp.concatenate([cvec] * reps, axis=1)
                wb[c] = jnp.exp(z[c] - sp[c] - inner[c] - carry).astype(BF16)
                total = inner[c][:, 0:1] + sp[c][:, 0:1]
                cvec = cvec + jnp.broadcast_to(total, (ts, LANES))
            c_ref[head] = cvec

        def values(head):
            p, h = divmod(head, halves)
            out = None
            for k in blocks:
                vb = jnp.where(in_head[h], v_ref[pl.ds(start[k], ts), cols[p]], zero_b)
                pv = _dot(wb[head, k], vb)
                out = pv if out is None else out + pv
            acc_ref[:, cols[p]] += out

        stages = (logits, stay, tails, weights, values)
        for step in range(SB_HEADS + len(stages) - 1):
            for s in reversed(range(len(stages))):
                if 0 <= step - s < SB_HEADS:
                    stages[s](step - s)

    def cond(carry):
        j, cmin = carry
        return jnp.logical_and(j >= 0, cmin < TAIL_CUTOFF)

    def body(carry):
        j, _ = carry
        sweep(j)
        return j - unroll, jnp.min(c_ref[...])

    lax.while_loop(cond, body, (qi, jnp.float32(0.0)))
    o_ref[...] = acc_ref[...].astype(BF16)


def _attn_call(qkv3):
    b, s, _ = qkv3.shape
    ts = ATTN_TILE
    kern = functools.partial(_attn_kernel, ts=ts, unroll=ATTN_UNROLL)
    return pl.pallas_call(
        kern,
        grid=(b, s // ts),
        in_specs=[
            pl.BlockSpec((None, ts, SB_WIDTH), lambda bi, qi: (bi, qi, 0)),
            pl.BlockSpec((None, s, SB_WIDTH), lambda bi, qi: (bi, 0, 1)),
            pl.BlockSpec((None, s, SB_WIDTH), lambda bi, qi: (bi, 0, 2)),
        ],
        out_specs=pl.BlockSpec((None, ts, SB_WIDTH), lambda bi, qi: (bi, qi, 0)),
        out_shape=jax.ShapeDtypeStruct((b, s, SB_WIDTH), BF16),
        scratch_shapes=[pltpu.VMEM((ts, SB_WIDTH), F32), pltpu.VMEM((SB_HEADS, ts, LANES), F32)],
        compiler_params=pltpu.CompilerParams(
            dimension_semantics=("arbitrary", "arbitrary"), vmem_limit_bytes=VMEM_LIMIT),
        name="attn",
    )(qkv3, qkv3, qkv3)


def _mix_kernel(attn_ref, ga_ref, gbb_ref, x_ref, wa_ref, wo_ref, g1_ref, b1_ref, wrt_ref, brt_ref,
                h1_ref, meta_ref, xls_ref, tcnt_ref, *, tile, tm, alpha):
    chunks = range(tile // tm)
    reps = tm // LANES
    rows = [slice(h * tm, (h + 1) * tm) for h in chunks]
    branch_a = [_dot(attn_ref[rows[h], :], wa_ref[...]) for h in chunks]
    mixed = [_dot((ga_ref[rows[h], :] * branch_a[h] + gbb_ref[rows[h], :]).astype(BF16), wo_ref[...])
             for h in chunks]
    h1b = []
    for h in chunks:
        h1 = _layer_norm(alpha * x_ref[rows[h], :] + mixed[h], g1_ref[...], b1_ref[...])
        h1_ref[rows[h], :] = h1
        h1b.append(h1.astype(BF16))
    bias = jnp.concatenate([brt_ref[...]] * reps, axis=1)
    logits = [_dot_nt(wrt_ref[...], h1b[h]) + bias for h in chunks]
    route = [_route(logits[h], tm) for h in chunks]

    trow = lax.broadcasted_iota(jnp.int32, (tm, tm), 0)
    tcol = lax.broadcasted_iota(jnp.int32, (tm, tm), 1)
    earlier = (trow < tcol).astype(BF16)
    e32r = lax.broadcasted_iota(jnp.int32, (N_EXPERTS, N_EXPERTS), 0)
    e32c = lax.broadcasted_iota(jnp.int32, (N_EXPERTS, N_EXPERTS), 1)
    lower = (e32c < e32r).astype(BF16)
    ones_b = jnp.ones((tm, LANES), BF16)
    before_tile = [_dot(route[h].onehot, earlier) for h in chunks]
    tile_cnt = [_dot(route[h].onehot, ones_b) for h in chunks]
    smaller = [_dot(lower, route[h].onehot).astype(BF16) for h in chunks]
    seg_off = [_dot(smaller[h], ones_b) for h in chunks]

    zrow = jnp.zeros((1, tm), F32)
    perm = []
    for h in chunks:
        rt = route[h]
        tcnt_ref[h] = tile_cnt[h]
        local = before_tile[h] + jnp.concatenate([seg_off[h]] * reps, axis=1)
        pos1 = jnp.sum(jnp.where(rt.is1, local, 0.0), axis=0, keepdims=True)
        pos2 = jnp.sum(jnp.where(rt.is2, local, 0.0), axis=0, keepdims=True)
        meta_ref[:, rows[h]] = jnp.concatenate(
            [rt.e1.astype(F32), rt.e2.astype(F32), rt.w1, rt.w2, pos1, pos2, zrow, zrow], axis=0)
        srow = lax.broadcasted_iota(jnp.int32, (2 * tm, tm), 0)
        perm.append(jnp.where((srow == pos1.astype(jnp.int32)) | (srow == pos2.astype(jnp.int32)), 1.0, 0.0)
                    .astype(BF16))
    xs = [_dot(perm[h], h1b[h]) for h in chunks]
    for h in chunks:
        base = h * 2 * tm * SUBLANES
        for c in range(xs[h].shape[1] // LANES):
            xls_ref[pl.ds(base + c, 2 * tm, stride=SUBLANES), :] = xs[h][:, c * LANES:(c + 1) * LANES]


class _Route(NamedTuple):
    e1: jax.Array
    e2: jax.Array
    w1: jax.Array
    w2: jax.Array
    is1: jax.Array
    is2: jax.Array
    onehot: jax.Array


def _route(lt, tm):
    r = [lt[k:k + 1, :] for k in range(N_GROUPS)]
    gmax = jnp.maximum(jnp.maximum(r[0], r[1]), jnp.maximum(r[2], r[3]))
    gidx = jnp.where(r[0] == gmax, 0, jnp.where(r[1] == gmax, 1, jnp.where(r[2] == gmax, 2, 3)))
    gsum = (jnp.exp(r[0] - gmax) + jnp.exp(r[1] - gmax)) + (jnp.exp(r[2] - gmax) + jnp.exp(r[3] - gmax))
    gprob = 1.0 / gsum
    epg = EXPERTS_PER_GROUP
    slabs = [lt[8 + g * epg:8 + (g + 1) * epg, :] for g in range(N_GROUPS)]
    el = jnp.where(gidx == 0, slabs[0], jnp.where(gidx == 1, slabs[1], jnp.where(gidx == 2, slabs[2], slabs[3])))
    r8 = lax.broadcasted_iota(jnp.int32, (epg, tm), 0)
    m1 = jnp.max(el, axis=0, keepdims=True)
    i1 = jnp.min(jnp.where(el == m1, r8, epg), axis=0, keepdims=True)
    el2 = jnp.where(r8 == i1, -jnp.inf, el)
    m2 = jnp.max(el2, axis=0, keepdims=True)
    i2 = jnp.min(jnp.where(el2 == m2, r8, epg), axis=0, keepdims=True)
    dlt = jnp.exp(m2 - m1)
    w1 = gprob / (1.0 + dlt)
    w2 = gprob * dlt / (1.0 + dlt)
    e1 = gidx * epg + i1
    e2 = gidx * epg + i2
    r32 = lax.broadcasted_iota(jnp.int32, (N_EXPERTS, tm), 0)
    is1 = r32 == e1
    is2 = r32 == e2
    onehot = jnp.where(is1 | is2, 1.0, 0.0).astype(BF16)
    return _Route(e1, e2, w1, w2, is1, is2, onehot)


def _mix_call(attn2, gate_a, gbb, x2, wa_b, wo_b, g1, b1, wrt_b, brt, alpha):
    t, d = x2.shape
    tile = PROJ_TILE
    kern = functools.partial(_mix_kernel, tile=tile, tm=SUB_TILE, alpha=alpha)
    const = lambda i: (0, 0)
    rowblk = lambda i: (i, 0)
    tm = tile
    return pl.pallas_call(
        kern,
        grid=(t // tm,),
        in_specs=[
            pl.BlockSpec((tm, SB_WIDTH), rowblk),
            pl.BlockSpec((tm, d), rowblk),
            pl.BlockSpec((tm, d), rowblk),
            pl.BlockSpec((tm, d), rowblk),
            pl.BlockSpec((SB_WIDTH, d), const),
            pl.BlockSpec((d, d), const),
            pl.BlockSpec((1, d), const),
            pl.BlockSpec((1, d), const),
            pl.BlockSpec((ROUTER_ROWS, d), const),
            pl.BlockSpec((ROUTER_ROWS, LANES), const),
        ],
        out_specs=[
            pl.BlockSpec((tm, d), rowblk),
            pl.BlockSpec((8, tm), lambda i: (0, i)),
            pl.BlockSpec((2 * tm * SUBLANES, LANES), rowblk),
            pl.BlockSpec((tile // SUB_TILE, N_EXPERTS, LANES), lambda i: (i, 0, 0)),
        ],
        out_shape=[
            jax.ShapeDtypeStruct((t, d), F32),
            jax.ShapeDtypeStruct((8, t), F32),
            jax.ShapeDtypeStruct((2 * t * SUBLANES, LANES), F32),
            jax.ShapeDtypeStruct((t // SUB_TILE, N_EXPERTS, LANES), F32),
        ],
        compiler_params=pltpu.CompilerParams(
            dimension_semantics=("arbitrary",), vmem_limit_bytes=VMEM_LIMIT),
        name="mix",
    )(attn2, gate_a, gbb, x2, wa_b, wo_b, g1, b1, wrt_b, brt)


def _row_copy(src_ref, src_row, dst_ref, dst_row, sem):
    return pltpu.make_async_copy(src_ref.at[pl.ds(src_row, 1), :], dst_ref.at[pl.ds(dst_row, 1), :], sem)


def _tok_rows(tok, n_tok):
    return pl.ds(pl.multiple_of(tok * SUBLANES, SUBLANES), pl.multiple_of(n_tok * SUBLANES, SUBLANES))


def _moe_kernel(be_ref, r0_ref, t0_ref, t1_ref, nv_ref, nu_ref, cumt_ref, soff_ref,
                xls_ref, wg_ref, wu_ref, wd_ref, y_ref,
                xbuf_ref, zero_ref, sem, wgb_ref, wub_ref, wdb_ref, prev_ref, *, blk, sub, tile_rows):
    i = pl.program_id(0)
    n_used = nu_ref[0]
    e = be_ref[i]
    cur = i % 2
    ne = N_EXPERTS

    def gather(b, slot):
        eb = be_ref[b]
        r0 = r0_ref[b]
        nv = nv_ref[b]

        def seg(t, c):
            first = cumt_ref[t * ne + eb]
            lo = jnp.maximum(first, r0)
            n = jnp.minimum(cumt_ref[(t + 1) * ne + eb], r0 + nv) - lo

            @pl.when(n > 0)
            def _():
                src = t * tile_rows + soff_ref[t * ne + eb] + (lo - first)
                pltpu.make_async_copy(xls_ref.at[_tok_rows(src, n), :],
                                      xbuf_ref.at[slot, _tok_rows(lo - r0, n), :], sem.at[slot]).start()
            return c

        lax.fori_loop(t0_ref[b], t1_ref[b] + 1, seg, 0)

        @pl.when(nv < blk)
        def _():
            pltpu.make_async_copy(zero_ref.at[_tok_rows(0, blk - nv), :],
                                  xbuf_ref.at[slot, _tok_rows(nv, blk - nv), :], sem.at[slot]).start()

    def drain(slot):
        pltpu.make_async_copy(xls_ref.at[pl.ds(0, blk * SUBLANES), :], xbuf_ref.at[slot], sem.at[slot]).wait()

    @pl.when(i == 0)
    def _():
        prev_ref[0] = -1
        zero_ref[...] = jnp.zeros(zero_ref.shape, F32)

    @pl.when(jnp.logical_and(i == 0, n_used > 0))
    def _():
        gather(0, 0)

    @pl.when(i < n_used)
    def _():
        @pl.when(e != prev_ref[0])
        def _():
            wgb_ref[...] = wg_ref[...].astype(BF16)
            wub_ref[...] = wu_ref[...].astype(BF16)
            wdb_ref[...] = wd_ref[...].astype(BF16)
            prev_ref[0] = e

        gather(jnp.minimum(i + 1, n_used - 1), 1 - cur)
        drain(cur)
        xbs = [jnp.concatenate([xbuf_ref[cur, pl.ds(h * sub * SUBLANES + c, sub, stride=SUBLANES), :]
                                for c in range(SUBLANES)], axis=1).astype(BF16) for h in range(blk // sub)]
        gate = [_dot(xb, wgb_ref[...]) for xb in xbs]
        up = [_dot(xb, wub_ref[...]) for xb in xbs]
        hidden = [(g * jax.nn.sigmoid(g) * u).astype(BF16) for g, u in zip(gate, up)]
        for h, hid in enumerate(hidden):
            y = _dot(hid, wdb_ref[...])
            for c in range(SUBLANES):
                y_ref[pl.ds(h * sub * SUBLANES + c, sub, stride=SUBLANES), :] = y[:, c * LANES:(c + 1) * LANES]

    @pl.when(i == n_used - 1)
    def _():
        drain(1 - cur)

    @pl.when(i >= n_used)
    def _():
        y_ref[...] = jnp.zeros(y_ref.shape, F32)


def _moe_call(tables, xls, tile_rows, w_gate, w_up, w_down):
    n_blocks = tables[0].shape[0]
    blk = EXPERT_BLOCK
    _, d, de = w_gate.shape
    wmap = lambda i, be, *_: (be[i], 0, 0)
    grid_spec = pltpu.PrefetchScalarGridSpec(
        num_scalar_prefetch=len(tables),
        grid=(n_blocks,),
        in_specs=[
            pl.BlockSpec(memory_space=pl.ANY),
            pl.BlockSpec((None, d, de), wmap),
            pl.BlockSpec((None, d, de), wmap),
            pl.BlockSpec((None, de, d), wmap),
        ],
        out_specs=pl.BlockSpec((blk * SUBLANES, LANES), lambda i, *_: (i, 0)),
        scratch_shapes=[
            pltpu.VMEM((2, blk * SUBLANES, LANES), F32), pltpu.VMEM((blk * SUBLANES, LANES), F32),
            pltpu.SemaphoreType.DMA((2,)),
            pltpu.VMEM((d, de), BF16), pltpu.VMEM((d, de), BF16), pltpu.VMEM((de, d), BF16),
            pltpu.SMEM((1,), jnp.int32),
        ],
    )
    return pl.pallas_call(
        functools.partial(_moe_kernel, blk=blk, sub=SUB_TILE, tile_rows=tile_rows),
        grid_spec=grid_spec,
        out_shape=jax.ShapeDtypeStruct((n_blocks * blk * SUBLANES, LANES), F32),
        compiler_params=pltpu.CompilerParams(
            dimension_semantics=("arbitrary",), vmem_limit_bytes=VMEM_LIMIT),
        name="moe",
    )(*tables, xls, w_gate, w_up, w_down)


def _combine_kernel(pstart_ref, cumt_ref, soff_ref, tcnt_ref,
                    h1_ref, meta_ref, y_ref, g2_ref, b2_ref, o_ref, ybuf_ref, sem, *, tm, alpha):
    i = pl.program_id(0)
    last = pl.num_programs(0) - 1
    cur = i % 2
    ne = N_EXPERTS

    def gather(t, slot):
        def seg(e, c):
            n = tcnt_ref[t * ne + e]

            @pl.when(n > 0)
            def _():
                src = pstart_ref[e] + cumt_ref[t * ne + e]
                pltpu.make_async_copy(y_ref.at[_tok_rows(src, n), :],
                                      ybuf_ref.at[slot, _tok_rows(soff_ref[t * ne + e], n), :],
                                      sem.at[slot]).start()
            return c

        lax.fori_loop(0, ne, seg, 0)

    def drain(slot):
        pltpu.make_async_copy(y_ref.at[pl.ds(0, 2 * tm * SUBLANES), :], ybuf_ref.at[slot], sem.at[slot]).wait()

    @pl.when(i == 0)
    def _():
        gather(0, 0)

    gather(jnp.minimum(i + 1, last), 1 - cur)
    drain(cur)
    ys = jnp.concatenate([ybuf_ref[cur, pl.ds(c, 2 * tm, stride=SUBLANES), :] for c in range(SUBLANES)],
                         axis=1).astype(BF16)
    srow = lax.broadcasted_iota(jnp.int32, (2 * tm, tm), 0)
    rw = meta_ref[...].T
    ffn = None
    for k in range(2):
        pos = meta_ref[4 + k:5 + k, :].astype(jnp.int32)
        sel = jnp.where(srow == pos, 1.0, 0.0).astype(BF16)
        term = rw[:, 2 + k:3 + k] * _dot_tn(sel, ys)
        ffn = term if ffn is None else ffn + term
    o_ref[...] = _layer_norm(alpha * h1_ref[...] + ffn, g2_ref[...], b2_ref[...])

    @pl.when(i == last)
    def _():
        drain(1 - cur)


def _combine_call(tables, h1, meta, y_slots, g2, b2, alpha):
    t, d = h1.shape
    tm = SUB_TILE
    kern = functools.partial(_combine_kernel, tm=tm, alpha=alpha)
    const = lambda i, *_: (0, 0)
    grid_spec = pltpu.PrefetchScalarGridSpec(
        num_scalar_prefetch=len(tables),
        grid=(t // tm,),
        in_specs=[
            pl.BlockSpec((tm, d), lambda i, *_: (i, 0)),
            pl.BlockSpec((8, tm), lambda i, *_: (0, i)),
            pl.BlockSpec(memory_space=pl.ANY),
            pl.BlockSpec((1, d), const),
            pl.BlockSpec((1, d), const),
        ],
        out_specs=pl.BlockSpec((tm, d), lambda i, *_: (i, 0)),
        scratch_shapes=[pltpu.VMEM((2, 2 * tm * SUBLANES, LANES), F32), pltpu.SemaphoreType.DMA((2,))],
    )
    return pl.pallas_call(
        kern,
        grid_spec=grid_spec,
        out_shape=jax.ShapeDtypeStruct((t, d), F32),
        compiler_params=pltpu.CompilerParams(
            dimension_semantics=("arbitrary",), vmem_limit_bytes=VMEM_LIMIT),
        name="combine",
    )(*tables, h1, meta, y_slots, g2, b2)


def _layer(h, w_in, gate_bias, conv_w, w_branch_a, w_branch_b, w_out, ln1_g, ln1_b,
           w_router_g, b_router_g, w_router_e, b_router_e, w_gate, w_up, w_down, ln2_g, ln2_b, alpha):
    bsz, seq, d = h.shape
    t = bsz * seq
    x2 = h.reshape(t, d)

    qkv, gate_a, gbb = _proj_call(x2, w_in.astype(BF16), gate_bias.reshape(1, 2 * d), conv_w,
                                  w_branch_b.astype(BF16), seq)
    attn = _attn_call(qkv.reshape(bsz, seq, 3 * SB_WIDTH)).reshape(t, SB_WIDTH)

    wrt = jnp.zeros((ROUTER_ROWS, d), F32).at[0:N_GROUPS].set(w_router_g.T).at[8:].set(w_router_e.T)
    brt = jnp.zeros((ROUTER_ROWS,), F32).at[0:N_GROUPS].set(b_router_g).at[8:].set(b_router_e.reshape(-1))
    brt = jnp.broadcast_to(brt[:, None], (ROUTER_ROWS, LANES))
    h1, meta, xls, tcnt = _mix_call(attn, gate_a, gbb, x2, w_branch_a.astype(BF16), w_out.astype(BF16),
                                    ln1_g.reshape(1, d), ln1_b.reshape(1, d), wrt.astype(BF16), brt, alpha)

    blk = EXPERT_BLOCK
    tile_cnt = tcnt[:, :, 0].astype(jnp.int32)
    n_tiles = tile_cnt.shape[0]
    cum_incl = jnp.cumsum(tile_cnt, axis=0)
    cumt = jnp.concatenate([jnp.zeros((1, N_EXPERTS), jnp.int32), cum_incl], axis=0)
    counts = cumt[-1]
    soff = jnp.cumsum(tile_cnt, axis=1) - tile_cnt
    padded = (counts + blk - 1) // blk * blk
    pad_end = jnp.cumsum(padded)
    pad_start = pad_end - padded
    n_blocks = (2 * t) // blk + N_EXPERTS
    block_start = jnp.arange(n_blocks, dtype=jnp.int32) * blk
    block_expert = jnp.minimum(jnp.sum(pad_end[None, :] <= block_start[:, None], axis=1), N_EXPERTS - 1)
    n_used = (pad_end[-1] // blk).reshape(1)

    be = block_expert.astype(jnp.int32)
    mine = be[:, None] == jnp.arange(N_EXPERTS, dtype=jnp.int32)[None, :]

    def of_block(table):
        return jnp.sum(jnp.where(mine, table[..., None, :], 0), axis=-1)

    r0 = block_start - of_block(pad_start)
    nv = jnp.clip(of_block(counts) - r0, 0, blk)
    t0 = jnp.minimum(jnp.sum(of_block(cum_incl) <= r0[None, :], axis=0), n_tiles - 1)
    t1 = jnp.sum(of_block(cumt[:-1]) < (r0 + nv)[None, :], axis=0) - 1
    tables = (be, r0, t0, t1, nv, n_used, cumt.reshape(-1), soff.reshape(-1))
    y_slots = _moe_call(tuple(a.astype(jnp.int32) for a in tables), xls, 2 * SUB_TILE, w_gate, w_up, w_down)
    ctables = (pad_start, cumt.reshape(-1), soff.reshape(-1), tile_cnt.reshape(-1))
    out = _combine_call(tuple(a.astype(jnp.int32) for a in ctables), h1, meta, y_slots,
                        ln2_g.reshape(1, d), ln2_b.reshape(1, d), alpha)
    return out.reshape(bsz, seq, d)


def kernel(x, w_in, gate_bias, conv_w, w_branch_a, w_branch_b, w_out, ln1_g, ln1_b, w_router_g, b_router_g,
           w_router_e, b_router_e, w_gate, w_up, w_down, ln2_g, ln2_b):
    depth = w_in.shape[0]
    alpha = (2.0 * depth) ** 0.25
    h = x
    for l in range(depth):
        h = _layer(h, w_in[l], gate_bias[l], conv_w[l], w_branch_a[l], w_branch_b[l], w_out[l], ln1_g[l],
                   ln1_b[l], w_router_g[l], b_router_g[l], w_router_e[l], b_router_e[l], w_gate[l], w_up[l],
                   w_down[l], ln2_g[l], ln2_b[l], alpha)
    return h
```

```python
import functools
from typing import NamedTuple

import jax
import jax.numpy as jnp
from jax import lax
from jax.experimental import pallas as pl
from jax.experimental.pallas import tpu as pltpu

F32 = jnp.float32
BF16 = jnp.bfloat16

SB_HEADS = 8
SB_HEAD_DIM = 64
SB_WIDTH = SB_HEADS * SB_HEAD_DIM
N_GROUPS = 4
EXPERTS_PER_GROUP = 8
N_EXPERTS = N_GROUPS * EXPERTS_PER_GROUP
LN_EPS = 1e-5

LANES = 128
SUBLANES = 8
HEAD_PAIR = LANES
PROJ_TILE = 512
SUB_TILE = 256
ATTN_TILE = 128
ATTN_UNROLL = 3
ROW_TILE = 256
EXPERT_BLOCK = 512
ROUTER_ROWS = 8 + N_EXPERTS
TAIL_CUTOFF = 110.0
MASKED_LOGIT = -1e30
LOG2E = 1.4426950408889634
VMEM_LIMIT = 56 * 1024 * 1024


def _dot(a, b):
    return jnp.dot(a, b, preferred_element_type=F32)


def _dot_nt(a, b):
    return lax.dot_general(a, b, (((1,), (1,)), ((), ())), preferred_element_type=F32)


def _dot_tn(a, b):
    return lax.dot_general(a, b, (((0,), (0,)), ((), ())), preferred_element_type=F32)


def _layer_norm(y, g, b):
    mu = jnp.mean(y, axis=-1, keepdims=True)
    d = y - mu
    var = jnp.mean(d * d, axis=-1, keepdims=True)
    return d * lax.rsqrt(var + LN_EPS) * g + b


def _proj_kernel(x_ref, w_ref, gbias_ref, convw_ref, wbb_ref, qkv_ref, ga_ref, gbb_ref, ubuf_ref,
                 *, tm, sub, tiles_per_seq, d_model):
    i = pl.program_id(0)
    cw = SB_WIDTH
    c0 = 3 * SB_WIDTH
    g0 = c0 + 3 * cw
    cwt = convw_ref[...]

    @pl.when(i % tiles_per_seq == 0)
    def _():
        ubuf_ref[0:8, :] = jnp.zeros((8, cw), F32)

    chunks = range(tm // sub)
    rows = [slice(h * sub, (h + 1) * sub) for h in chunks]
    xb = [x_ref[rows[h], :].astype(BF16) for h in chunks]
    conv_in = [_dot(xb[h], w_ref[:, c0:g0]) for h in chunks]
    qkv = [_dot(xb[h], w_ref[:, 0:c0]) for h in chunks]
    gated = []
    for h in chunks:
        cb = conv_in[h][:, 0:cw]
        u = conv_in[h][:, cw:2 * cw] * conv_in[h][:, 2 * cw:3 * cw]
        ubuf_ref[8:sub + 8, :] = u
        y = (cwt[0:1, :] * ubuf_ref[pl.ds(6, sub), :] + cwt[1:2, :] * ubuf_ref[pl.ds(7, sub), :]
             + cwt[2:3, :] * u)
        ubuf_ref[0:8, :] = u[sub - 8:sub, :]
        gated.append((cb * y).astype(BF16))
        qkv_ref[rows[h], 0:SB_WIDTH] = (qkv[h][:, 0:SB_WIDTH] * (SB_HEAD_DIM ** -0.5)).astype(BF16)
        qkv_ref[rows[h], SB_WIDTH:c0] = qkv[h][:, SB_WIDTH:c0].astype(BF16)
    gate_logits = [_dot(xb[h], w_ref[:, g0:g0 + 2 * d_model]) for h in chunks]
    branch_b = [_dot(gated[h], wbb_ref[...]) for h in chunks]
    for h in chunks:
        gates = jax.nn.sigmoid(gate_logits[h] + gbias_ref[...])
        ga_ref[rows[h], :] = gates[:, 0:d_model].astype(BF16)
        gbb_ref[rows[h], :] = (gates[:, d_model:2 * d_model] * branch_b[h]).astype(BF16)


def _proj_call(x2, w_in_b, gbias, conv_w, wbb_b, seq):
    t, d = x2.shape
    pw = w_in_b.shape[1]
    tm = PROJ_TILE
    sub = SUB_TILE
    cw = conv_w.shape[1]
    kern = functools.partial(_proj_kernel, tm=tm, sub=sub, tiles_per_seq=seq // tm, d_model=d)
    const = lambda i: (0, 0)
    return pl.pallas_call(
        kern,
        grid=(t // tm,),
        in_specs=[
            pl.BlockSpec((tm, d), lambda i: (i, 0)),
            pl.BlockSpec((d, pw), const),
            pl.BlockSpec((1, 2 * d), const),
            pl.BlockSpec((3, cw), const),
            pl.BlockSpec((cw, d), const),
        ],
        out_specs=[
            pl.BlockSpec((tm, 3 * SB_WIDTH), lambda i: (i, 0)),
            pl.BlockSpec((tm, d), lambda i: (i, 0)),
            pl.BlockSpec((tm, d), lambda i: (i, 0)),
        ],
        out_shape=[
            jax.ShapeDtypeStruct((t, 3 * SB_WIDTH), BF16),
            jax.ShapeDtypeStruct((t, d), BF16),
            jax.ShapeDtypeStruct((t, d), BF16),
        ],
        scratch_shapes=[pltpu.VMEM((sub + 8, cw), F32)],
        compiler_params=pltpu.CompilerParams(
            dimension_semantics=("arbitrary",), vmem_limit_bytes=VMEM_LIMIT),
        name="proj",
    )(x2, w_in_b, gbias, conv_w, wbb_b)


def _attn_kernel(q_ref, k_ref, v_ref, o_ref, acc_ref, c_ref, *, ts, unroll):
    qi = pl.program_id(1)
    lane = lax.broadcasted_iota(jnp.int32, (ts, HEAD_PAIR), 1)
    row = lax.broadcasted_iota(jnp.int32, (ts, ts), 0)
    col = lax.broadcasted_iota(jnp.int32, (ts, ts), 1)
    later = (row > col).astype(BF16)
    later2 = jnp.concatenate([later, later], axis=0)
    later2 = jnp.concatenate([later2, jnp.ones((2 * ts, LANES), BF16)], axis=1)
    ahead = col - row
    reps = ts // LANES
    zero_b = jnp.zeros((ts, HEAD_PAIR), BF16)
    halves = HEAD_PAIR // SB_HEAD_DIM
    in_head = [(lane >= h * SB_HEAD_DIM) & (lane < (h + 1) * SB_HEAD_DIM) for h in range(halves)]

    acc_ref[...] = jnp.zeros(acc_ref.shape, F32)
    c_ref[...] = jnp.zeros(c_ref.shape, F32)

    pairs = range(SB_WIDTH // HEAD_PAIR)
    cols = [slice(p * HEAD_PAIR, (p + 1) * HEAD_PAIR) for p in pairs]

    def sweep(j):
        start, bias = [], []
        for k in range(unroll):
            jj = j - k
            start.append(pl.multiple_of(jnp.maximum(jj, 0) * ts, ts))
            limit = jnp.where(jj >= 0, (qi - jj) * ts, -2 * ts)
            bias.append(jnp.where(ahead < limit, 0.0, MASKED_LOGIT))
        blocks = range(unroll)
        z, sp, split, inner, wb = {}, {}, {}, {}, {}

        def logits(head):
            p, h = divmod(head, halves)
            q = jnp.where(in_head[h], q_ref[:, cols[p]], zero_b)
            for k in blocks:
                z[head, k] = _dot_nt(q, k_ref[pl.ds(start[k], ts), cols[p]]) + bias[k]

        def stay(head):
            for k in blocks:
                zz = z[head, k]
                s = jnp.maximum(zz, 0.0) + jnp.log(1.0 + jnp.exp2(jnp.abs(zz) * (-LOG2E)))
                hi = s.astype(BF16)
                sp[head, k] = s
                split[head, k] = jnp.concatenate([hi, (s - hi.astype(F32)).astype(BF16)], axis=1)

        def tails(head):
            for k in blocks:
                inner[head, k] = _dot(split[head, k], later2)

        def weights(head):
            cvec = c_ref[head]
            for k in blocks:
                c = (head, k)
                carry = jnp.concatenate([cvec] * reps, axis=1)
                wb[c] = jnp.exp(z[c] - sp[c] - inner[c][:, 0:ts] - carry).astype(BF16)
                cvec = cvec + inner[c][:, ts:ts + LANES]
            c_ref[head] = cvec

        def values(head):
            p, h = divmod(head, halves)
            out = None
            for k in blocks:
                vb = jnp.where(in_head[h], v_ref[pl.ds(start[k], ts), cols[p]], zero_b)
                pv = _dot(wb[head, k], vb)
                out = pv if out is None else out + pv
            acc_ref[:, cols[p]] += out

        stages = (logits, stay, tails, weights, values)
        for step in range(SB_HEADS + len(stages) - 1):
            for s in reversed(range(len(stages))):
                if 0 <= step - s < SB_HEADS:
                    stages[s](step - s)

    def cond(carry):
        j, cmin = carry
        return jnp.logical_and(j >= 0, cmin < TAIL_CUTOFF)

    def body(carry):
        j, _ = carry
        sweep(j)
        return j - unroll, jnp.min(c_ref[...])

    lax.while_loop(cond, body, (qi, jnp.float32(0.0)))
    o_ref[...] = acc_ref[...].astype(BF16)


def _attn_call(qkv3):
    b, s, _ = qkv3.shape
    ts = ATTN_TILE
    kern = functools.partial(_attn_kernel, ts=ts, unroll=ATTN_UNROLL)
    return pl.pallas_call(
        kern,
        grid=(b, s // ts),
        in_specs=[
            pl.BlockSpec((None, ts, SB_WIDTH), lambda bi, qi: (bi, qi, 0)),
            pl.BlockSpec((None, s, SB_WIDTH), lambda bi, qi: (bi, 0, 1)),
            pl.BlockSpec((None, s, SB_WIDTH), lambda bi, qi: (bi, 0, 2)),
        ],
        out_specs=pl.BlockSpec((None, ts, SB_WIDTH), lambda bi, qi: (bi, qi, 0)),
        out_shape=jax.ShapeDtypeStruct((b, s, SB_WIDTH), BF16),
        scratch_shapes=[pltpu.VMEM((ts, SB_WIDTH), F32), pltpu.VMEM((SB_HEADS, ts, LANES), F32)],
        compiler_params=pltpu.CompilerParams(
            dimension_semantics=("arbitrary", "arbitrary"), vmem_limit_bytes=VMEM_LIMIT),
        name="attn",
    )(qkv3, qkv3, qkv3)


def _mix_kernel(attn_ref, ga_ref, gbb_ref, x_ref, wa_ref, wo_ref, g1_ref, b1_ref, wrt_ref, brt_ref,
                h1_ref, meta_ref, xls_ref, tcnt_ref, *, tile, tm, alpha):
    chunks = range(tile // tm)
    reps = tm // LANES
    rows = [slice(h * tm, (h + 1) * tm) for h in chunks]
    branch_a = [_dot(attn_ref[rows[h], :], wa_ref[...]) for h in chunks]
    mixed = [_dot((ga_ref[rows[h], :] * branch_a[h] + gbb_ref[rows[h], :]).astype(BF16), wo_ref[...])
             for h in chunks]
    h1b = []
    for h in chunks:
        h1 = _layer_norm(alpha * x_ref[rows[h], :] + mixed[h], g1_ref[...], b1_ref[...])
        h1_ref[rows[h], :] = h1
        h1b.append(h1.astype(BF16))
    bias = jnp.concatenate([brt_ref[...]] * reps, axis=1)
    logits = [_dot_nt(wrt_ref[...], h1b[h]) + bias for h in chunks]
    route = [_route(logits[h], tm) for h in chunks]

    trow = lax.broadcasted_iota(jnp.int32, (tm, tm), 0)
    tcol = lax.broadcasted_iota(jnp.int32, (tm, tm), 1)
    earlier = (trow < tcol).astype(BF16)
    e32r = lax.broadcasted_iota(jnp.int32, (N_EXPERTS, N_EXPERTS), 0)
    e32c = lax.broadcasted_iota(jnp.int32, (N_EXPERTS, N_EXPERTS), 1)
    lower = (e32c < e32r).astype(BF16)
    ones_b = jnp.ones((tm, LANES), BF16)
    before_tile = [_dot(route[h].onehot, earlier) for h in chunks]
    tile_cnt = [_dot(route[h].onehot, ones_b) for h in chunks]
    smaller = [_dot(lower, route[h].onehot).astype(BF16) for h in chunks]
    seg_off = [_dot(smaller[h], ones_b) for h in chunks]

    zrow = jnp.zeros((1, tm), F32)
    perm = []
    for h in chunks:
        rt = route[h]
        tcnt_ref[h] = tile_cnt[h]
        local = before_tile[h] + jnp.concatenate([seg_off[h]] * reps, axis=1)
        pos1 = jnp.sum(jnp.where(rt.is1, local, 0.0), axis=0, keepdims=True)
        pos2 = jnp.sum(jnp.where(rt.is2, local, 0.0), axis=0, keepdims=True)
        meta_ref[:, rows[h]] = jnp.concatenate(
            [rt.e1.astype(F32), rt.e2.astype(F32), rt.w1, rt.w2, pos1, pos2, zrow, zrow], axis=0)
        srow = lax.broadcasted_iota(jnp.int32, (2 * tm, tm), 0)
        perm.append(jnp.where((srow == pos1.astype(jnp.int32)) | (srow == pos2.astype(jnp.int32)), 1.0, 0.0)
                    .astype(BF16))
    xs = [_dot(perm[h], h1b[h]) for h in chunks]
    for h in chunks:
        base = h * 2 * tm * SUBLANES
        for c in range(xs[h].shape[1] // LANES):
            xls_ref[pl.ds(base + c, 2 * tm, stride=SUBLANES), :] = xs[h][:, c * LANES:(c + 1) * LANES]


class _Route(NamedTuple):
    e1: jax.Array
    e2: jax.Array
    w1: jax.Array
    w2: jax.Array
    is1: jax.Array
    is2: jax.Array
    onehot: jax.Array


def _route(lt, tm):
    r = [lt[k:k + 1, :] for k in range(N_GROUPS)]
    gmax = jnp.maximum(jnp.maximum(r[0], r[1]), jnp.maximum(r[2], r[3]))
    gidx = jnp.where(r[0] == gmax, 0, jnp.where(r[1] == gmax, 1, jnp.where(r[2] == gmax, 2, 3)))
    gsum = (jnp.exp(r[0] - gmax) + jnp.exp(r[1] - gmax)) + (jnp.exp(r[2] - gmax) + jnp.exp(r[3] - gmax))
    gprob = 1.0 / gsum
    epg = EXPERTS_PER_GROUP
    slabs = [lt[8 + g * epg:8 + (g + 1) * epg, :] for g in range(N_GROUPS)]
    el = jnp.where(gidx == 0, slabs[0], jnp.where(gidx == 1, slabs[1], jnp.where(gidx == 2, slabs[2], slabs[3])))
    r8 = lax.broadcasted_iota(jnp.int32, (epg, tm), 0)
    m1 = jnp.max(el, axis=0, keepdims=True)
    i1 = jnp.min(jnp.where(el == m1, r8, epg), axis=0, keepdims=True)
    el2 = jnp.where(r8 == i1, -jnp.inf, el)
    m2 = jnp.max(el2, axis=0, keepdims=True)
    i2 = jnp.min(jnp.where(el2 == m2, r8, epg), axis=0, keepdims=True)
    dlt = jnp.exp(m2 - m1)
    w1 = gprob / (1.0 + dlt)
    w2 = gprob * dlt / (1.0 + dlt)
    e1 = gidx * epg + i1
    e2 = gidx * epg + i2
    r32 = lax.broadcasted_iota(jnp.int32, (N_EXPERTS, tm), 0)
    is1 = r32 == e1
    is2 = r32 == e2
    onehot = jnp.where(is1 | is2, 1.0, 0.0).astype(BF16)
    return _Route(e1, e2, w1, w2, is1, is2, onehot)


def _mix_call(attn2, gate_a, gbb, x2, wa_b, wo_b, g1, b1, wrt_b, brt, alpha):
    t, d = x2.shape
    tile = PROJ_TILE
    kern = functools.partial(_mix_kernel, tile=tile, tm=SUB_TILE, alpha=alpha)
    const = lambda i: (0, 0)
    rowblk = lambda i: (i, 0)
    tm = tile
    return pl.pallas_call(
        kern,
        grid=(t // tm,),
        in_specs=[
            pl.BlockSpec((tm, SB_WIDTH), rowblk),
            pl.BlockSpec((tm, d), rowblk),
            pl.BlockSpec((tm, d), rowblk),
            pl.BlockSpec((tm, d), rowblk),
            pl.BlockSpec((SB_WIDTH, d), const),
            pl.BlockSpec((d, d), const),
            pl.BlockSpec((1, d), const),
            pl.BlockSpec((1, d), const),
            pl.BlockSpec((ROUTER_ROWS, d), const),
            pl.BlockSpec((ROUTER_ROWS, LANES), const),
        ],
        out_specs=[
            pl.BlockSpec((tm, d), rowblk),
            pl.BlockSpec((8, tm), lambda i: (0, i)),
            pl.BlockSpec((2 * tm * SUBLANES, LANES), rowblk),
            pl.BlockSpec((tile // SUB_TILE, N_EXPERTS, LANES), lambda i: (i, 0, 0)),
        ],
        out_shape=[
            jax.ShapeDtypeStruct((t, d), F32),
            jax.ShapeDtypeStruct((8, t), F32),
            jax.ShapeDtypeStruct((2 * t * SUBLANES, LANES), F32),
            jax.ShapeDtypeStruct((t // SUB_TILE, N_EXPERTS, LANES), F32),
        ],
        compiler_params=pltpu.CompilerParams(
            dimension_semantics=("arbitrary",), vmem_limit_bytes=VMEM_LIMIT),
        name="mix",
    )(attn2, gate_a, gbb, x2, wa_b, wo_b, g1, b1, wrt_b, brt)


def _row_copy(src_ref, src_row, dst_ref, dst_row, sem):
    return pltpu.make_async_copy(src_ref.at[pl.ds(src_row, 1), :], dst_ref.at[pl.ds(dst_row, 1), :], sem)


def _tok_rows(tok, n_tok):
    return pl.ds(pl.multiple_of(tok * SUBLANES, SUBLANES), pl.multiple_of(n_tok * SUBLANES, SUBLANES))


def _moe_kernel(be_ref, r0_ref, t0_ref, t1_ref, nv_ref, nu_ref, nxt_ref, cumt_ref, soff_ref,
                xls_ref, wg_ref, wu_ref, wd_ref, y_ref,
                xbuf_ref, zero_ref, sem, wgf_ref, wuf_ref, wdf_ref, wsem, wgb_ref, wub_ref, wdb_ref, state_ref,
                *, blk, sub, tile_rows):
    i = pl.program_id(0)
    n_used = nu_ref[0]
    e = be_ref[i]
    cur = i % 2
    ne = N_EXPERTS
    w_hbm = (wg_ref, wu_ref, wd_ref)
    wf_refs = (wgf_ref, wuf_ref, wdf_ref)

    def fetch_weights(expert, slot):
        for src, dst in zip(w_hbm, wf_refs):
            pltpu.make_async_copy(src.at[expert], dst.at[slot], wsem.at[slot]).start()

    def wait_weights(slot):
        for src, dst in zip(w_hbm, wf_refs):
            pltpu.make_async_copy(src.at[0], dst.at[slot], wsem.at[slot]).wait()

    def gather(b, slot):
        eb = be_ref[b]
        r0 = r0_ref[b]
        nv = nv_ref[b]

        def seg(t, c):
            first = cumt_ref[t * ne + eb]
            lo = jnp.maximum(first, r0)
            n = jnp.minimum(cumt_ref[(t + 1) * ne + eb], r0 + nv) - lo

            @pl.when(n > 0)
            def _():
                src = t * tile_rows + soff_ref[t * ne + eb] + (lo - first)
                pltpu.make_async_copy(xls_ref.at[_tok_rows(src, n), :],
                                      xbuf_ref.at[slot, _tok_rows(lo - r0, n), :], sem.at[slot]).start()
            return c

        lax.fori_loop(t0_ref[b], t1_ref[b] + 1, seg, 0)

        @pl.when(nv < blk)
        def _():
            pltpu.make_async_copy(zero_ref.at[_tok_rows(0, blk - nv), :],
                                  xbuf_ref.at[slot, _tok_rows(nv, blk - nv), :], sem.at[slot]).start()

    def drain(slot):
        pltpu.make_async_copy(xls_ref.at[pl.ds(0, blk * SUBLANES), :], xbuf_ref.at[slot], sem.at[slot]).wait()

    @pl.when(i == 0)
    def _():
        state_ref[0] = -1
        state_ref[1] = 0
        zero_ref[...] = jnp.zeros(zero_ref.shape, F32)

    @pl.when(jnp.logical_and(i == 0, n_used > 0))
    def _():
        gather(0, 0)
        fetch_weights(e, 0)

    @pl.when(i < n_used)
    def _():
        @pl.when(e != state_ref[0])
        def _():
            slot = state_ref[1]
            wait_weights(slot)
            wgb_ref[...] = wf_refs[0][slot].astype(BF16)
            wub_ref[...] = wf_refs[1][slot].astype(BF16)
            wdb_ref[...] = wf_refs[2][slot].astype(BF16)
            state_ref[0] = e
            state_ref[1] = 1 - slot

            @pl.when(nxt_ref[i] >= 0)
            def _():
                fetch_weights(nxt_ref[i], 1 - slot)

        gather(jnp.minimum(i + 1, n_used - 1), 1 - cur)
        drain(cur)
        xbs = [jnp.concatenate([xbuf_ref[cur, pl.ds(h * sub * SUBLANES + c, sub, stride=SUBLANES), :]
                                for c in range(SUBLANES)], axis=1).astype(BF16) for h in range(blk // sub)]
        gate = [_dot(xb, wgb_ref[...]) for xb in xbs]
        up = [_dot(xb, wub_ref[...]) for xb in xbs]
        hidden = [(g * jax.nn.sigmoid(g) * u).astype(BF16) for g, u in zip(gate, up)]
        for h, hid in enumerate(hidden):
            y = _dot(hid, wdb_ref[...])
            for c in range(SUBLANES):
                y_ref[pl.ds(h * sub * SUBLANES + c, sub, stride=SUBLANES), :] = y[:, c * LANES:(c + 1) * LANES]

    @pl.when(i == n_used - 1)
    def _():
        drain(1 - cur)

    @pl.when(i >= n_used)
    def _():
        y_ref[...] = jnp.zeros(y_ref.shape, F32)


def _moe_call(tables, xls, tile_rows, w_gate, w_up, w_down):
    n_blocks = tables[0].shape[0]
    blk = EXPERT_BLOCK
    _, d, de = w_gate.shape
    grid_spec = pltpu.PrefetchScalarGridSpec(
        num_scalar_prefetch=len(tables),
        grid=(n_blocks,),
        in_specs=[pl.BlockSpec(memory_space=pl.ANY)] * 4,
        out_specs=pl.BlockSpec((blk * SUBLANES, LANES), lambda i, *_: (i, 0)),
        scratch_shapes=[
            pltpu.VMEM((2, blk * SUBLANES, LANES), F32), pltpu.VMEM((blk * SUBLANES, LANES), F32),
            pltpu.SemaphoreType.DMA((2,)),
            pltpu.VMEM((2, d, de), F32), pltpu.VMEM((2, d, de), F32), pltpu.VMEM((2, de, d), F32),
            pltpu.SemaphoreType.DMA((2,)),
            pltpu.VMEM((d, de), BF16), pltpu.VMEM((d, de), BF16), pltpu.VMEM((de, d), BF16),
            pltpu.SMEM((2,), jnp.int32),
        ],
    )
    return pl.pallas_call(
        functools.partial(_moe_kernel, blk=blk, sub=SUB_TILE, tile_rows=tile_rows),
        grid_spec=grid_spec,
        out_shape=jax.ShapeDtypeStruct((n_blocks * blk * SUBLANES, LANES), F32),
        compiler_params=pltpu.CompilerParams(
            dimension_semantics=("arbitrary",), vmem_limit_bytes=VMEM_LIMIT),
        name="moe",
    )(*tables, xls, w_gate, w_up, w_down)


def _combine_kernel(pstart_ref, cumt_ref, soff_ref, tcnt_ref,
                    h1_ref, meta_ref, y_ref, g2_ref, b2_ref, o_ref, ybuf_ref, sem, *, tm, alpha):
    i = pl.program_id(0)
    last = pl.num_programs(0) - 1
    cur = i % 2
    ne = N_EXPERTS

    def gather(t, slot):
        def seg(e, c):
            n = tcnt_ref[t * ne + e]

            @pl.when(n > 0)
            def _():
                src = pstart_ref[e] + cumt_ref[t * ne + e]
                pltpu.make_async_copy(y_ref.at[_tok_rows(src, n), :],
                                      ybuf_ref.at[slot, _tok_rows(soff_ref[t * ne + e], n), :],
                                      sem.at[slot]).start()
            return c

        lax.fori_loop(0, ne, seg, 0)

    def drain(slot):
        pltpu.make_async_copy(y_ref.at[pl.ds(0, 2 * tm * SUBLANES), :], ybuf_ref.at[slot], sem.at[slot]).wait()

    @pl.when(i == 0)
    def _():
        gather(0, 0)

    gather(jnp.minimum(i + 1, last), 1 - cur)
    drain(cur)
    ys = jnp.concatenate([ybuf_ref[cur, pl.ds(c, 2 * tm, stride=SUBLANES), :] for c in range(SUBLANES)],
                         axis=1).astype(BF16)
    srow = lax.broadcasted_iota(jnp.int32, (2 * tm, tm), 0)
    rw = meta_ref[...].T
    ffn = None
    for k in range(2):
        pos = meta_ref[4 + k:5 + k, :].astype(jnp.int32)
        sel = jnp.where(srow == pos, 1.0, 0.0).astype(BF16)
        term = rw[:, 2 + k:3 + k] * _dot_tn(sel, ys)
        ffn = term if ffn is None else ffn + term
    o_ref[...] = _layer_norm(alpha * h1_ref[...] + ffn, g2_ref[...], b2_ref[...])

    @pl.when(i == last)
    def _():
        drain(1 - cur)


def _combine_call(tables, h1, meta, y_slots, g2, b2, alpha):
    t, d = h1.shape
    tm = SUB_TILE
    kern = functools.partial(_combine_kernel, tm=tm, alpha=alpha)
    const = lambda i, *_: (0, 0)
    grid_spec = pltpu.PrefetchScalarGridSpec(
        num_scalar_prefetch=len(tables),
        grid=(t // tm,),
        in_specs=[
            pl.BlockSpec((tm, d), lambda i, *_: (i, 0)),
            pl.BlockSpec((8, tm), lambda i, *_: (0, i)),
            pl.BlockSpec(memory_space=pl.ANY),
            pl.BlockSpec((1, d), const),
            pl.BlockSpec((1, d), const),
        ],
        out_specs=pl.BlockSpec((tm, d), lambda i, *_: (i, 0)),
        scratch_shapes=[pltpu.VMEM((2, 2 * tm * SUBLANES, LANES), F32), pltpu.SemaphoreType.DMA((2,))],
    )
    return pl.pallas_call(
        kern,
        grid_spec=grid_spec,
        out_shape=jax.ShapeDtypeStruct((t, d), F32),
        compiler_params=pltpu.CompilerParams(
            dimension_semantics=("arbitrary",), vmem_limit_bytes=VMEM_LIMIT),
        name="combine",
    )(*tables, h1, meta, y_slots, g2, b2)


def _layer(h, w_in, gate_bias, conv_w, w_branch_a, w_branch_b, w_out, ln1_g, ln1_b,
           w_router_g, b_router_g, w_router_e, b_router_e, w_gate, w_up, w_down, ln2_g, ln2_b, alpha):
    bsz, seq, d = h.shape
    t = bsz * seq
    x2 = h.reshape(t, d)

    qkv, gate_a, gbb = _proj_call(x2, w_in.astype(BF16), gate_bias.reshape(1, 2 * d), conv_w,
                                  w_branch_b.astype(BF16), seq)
    attn = _attn_call(qkv.reshape(bsz, seq, 3 * SB_WIDTH)).reshape(t, SB_WIDTH)

    wrt = jnp.zeros((ROUTER_ROWS, d), F32).at[0:N_GROUPS].set(w_router_g.T).at[8:].set(w_router_e.T)
    brt = jnp.zeros((ROUTER_ROWS,), F32).at[0:N_GROUPS].set(b_router_g).at[8:].set(b_router_e.reshape(-1))
    brt = jnp.broadcast_to(brt[:, None], (ROUTER_ROWS, LANES))
    h1, meta, xls, tcnt = _mix_call(attn, gate_a, gbb, x2, w_branch_a.astype(BF16), w_out.astype(BF16),
                                    ln1_g.reshape(1, d), ln1_b.reshape(1, d), wrt.astype(BF16), brt, alpha)

    blk = EXPERT_BLOCK
    tile_cnt = tcnt[:, :, 0].astype(jnp.int32)
    n_tiles = tile_cnt.shape[0]
    cum_incl = jnp.cumsum(tile_cnt, axis=0)
    cumt = jnp.concatenate([jnp.zeros((1, N_EXPERTS), jnp.int32), cum_incl], axis=0)
    counts = cumt[-1]
    soff = jnp.cumsum(tile_cnt, axis=1) - tile_cnt
    padded = (counts + blk - 1) // blk * blk
    pad_end = jnp.cumsum(padded)
    pad_start = pad_end - padded
    n_blocks = (2 * t) // blk + N_EXPERTS
    block_start = jnp.arange(n_blocks, dtype=jnp.int32) * blk
    block_expert = jnp.minimum(jnp.sum(pad_end[None, :] <= block_start[:, None], axis=1), N_EXPERTS - 1)
    n_used = (pad_end[-1] // blk).reshape(1)

    be = block_expert.astype(jnp.int32)
    mine = be[:, None] == jnp.arange(N_EXPERTS, dtype=jnp.int32)[None, :]

    def of_block(table):
        return jnp.sum(jnp.where(mine, table[..., None, :], 0), axis=-1)

    r0 = block_start - of_block(pad_start)
    nv = jnp.clip(of_block(counts) - r0, 0, blk)
    t0 = jnp.minimum(jnp.sum(of_block(cum_incl) <= r0[None, :], axis=0), n_tiles - 1)
    t1 = jnp.sum(of_block(cumt[:-1]) < (r0 + nv)[None, :], axis=0) - 1
    after = of_block(pad_end) // blk
    be_after = jnp.sum(jnp.where(after[:, None] == jnp.arange(n_blocks, dtype=jnp.int32)[None, :], be[None, :], 0),
                       axis=1)
    nxt = jnp.where(after < n_used[0], be_after, -1)
    tables = (be, r0, t0, t1, nv, n_used, nxt, cumt.reshape(-1), soff.reshape(-1))
    y_slots = _moe_call(tuple(a.astype(jnp.int32) for a in tables), xls, 2 * SUB_TILE, w_gate, w_up, w_down)
    ctables = (pad_start, cumt.reshape(-1), soff.reshape(-1), tile_cnt.reshape(-1))
    out = _combine_call(tuple(a.astype(jnp.int32) for a in ctables), h1, meta, y_slots,
                        ln2_g.reshape(1, d), ln2_b.reshape(1, d), alpha)
    return out.reshape(bsz, seq, d)


def kernel(x, w_in, gate_bias, conv_w, w_branch_a, w_branch_b, w_out, ln1_g, ln1_b, w_router_g, b_router_g,
           w_router_e, b_router_e, w_gate, w_up, w_down, ln2_g, ln2_b):
    depth = w_in.shape[0]
    alpha = (2.0 * depth) ** 0.25
    h = x
    for l in range(depth):
        h = _layer(h, w_in[l], gate_bias[l], conv_w[l], w_branch_a[l], w_branch_b[l], w_out[l], ln1_g[l],
                   ln1_b[l], w_router_g[l], b_router_g[l], w_router_e[l], b_router_e[l], w_gate[l], w_up[l],
                   w_down[l], ln2_g[l], ln2_b[l], alpha)
    return h
```

```python
import functools
from typing import NamedTuple

import jax
import jax.numpy as jnp
from jax import lax
from jax.experimental import pallas as pl
from jax.experimental.pallas import tpu as pltpu

F32 = jnp.float32
BF16 = jnp.bfloat16

SB_HEADS = 8
SB_HEAD_DIM = 64
SB_WIDTH = SB_HEADS * SB_HEAD_DIM
N_GROUPS = 4
EXPERTS_PER_GROUP = 8
N_EXPERTS = N_GROUPS * EXPERTS_PER_GROUP
LN_EPS = 1e-5

LANES = 128
SUBLANES = 8
HEAD_PAIR = LANES
PROJ_TILE = 512
SUB_TILE = 256
ATTN_TILE = 128
ATTN_SUB = 64
ATTN_UNROLL = 2
ATTN_LAG = 8
ROW_TILE = 256
EXPERT_BLOCK = 512
ROUTER_ROWS = 8 + N_EXPERTS
TAIL_CUTOFF = 110.0
MASKED_LOGIT = -1e30
LOG2E = 1.4426950408889634
VMEM_LIMIT = 56 * 1024 * 1024


def _dot(a, b):
    return jnp.dot(a, b, preferred_element_type=F32)


def _dot_nt(a, b):
    return lax.dot_general(a, b, (((1,), (1,)), ((), ())), preferred_element_type=F32)


def _dot_tn(a, b):
    return lax.dot_general(a, b, (((0,), (0,)), ((), ())), preferred_element_type=F32)


def _layer_norm(y, g, b):
    mu = jnp.mean(y, axis=-1, keepdims=True)
    d = y - mu
    var = jnp.mean(d * d, axis=-1, keepdims=True)
    return d * lax.rsqrt(var + LN_EPS) * g + b


def _proj_kernel(x_ref, w_ref, gbias_ref, convw_ref, wbb_ref, qkv_ref, ga_ref, gbb_ref, ubuf_ref,
                 *, tm, sub, tiles_per_seq, d_model):
    i = pl.program_id(0)
    cw = SB_WIDTH
    c0 = 3 * SB_WIDTH
    g0 = c0 + 3 * cw
    cwt = convw_ref[...]

    @pl.when(i % tiles_per_seq == 0)
    def _():
        ubuf_ref[0:8, :] = jnp.zeros((8, cw), F32)

    chunks = range(tm // sub)
    rows = [slice(h * sub, (h + 1) * sub) for h in chunks]
    xb = [x_ref[rows[h], :].astype(BF16) for h in chunks]
    conv_in = [_dot(xb[h], w_ref[:, c0:g0]) for h in chunks]
    qkv = [_dot(xb[h], w_ref[:, 0:c0]) for h in chunks]
    gated = []
    for h in chunks:
        cb = conv_in[h][:, 0:cw]
        u = conv_in[h][:, cw:2 * cw] * conv_in[h][:, 2 * cw:3 * cw]
        ubuf_ref[8:sub + 8, :] = u
        y = (cwt[0:1, :] * ubuf_ref[pl.ds(6, sub), :] + cwt[1:2, :] * ubuf_ref[pl.ds(7, sub), :]
             + cwt[2:3, :] * u)
        ubuf_ref[0:8, :] = u[sub - 8:sub, :]
        gated.append((cb * y).astype(BF16))
        qkv_ref[rows[h], 0:SB_WIDTH] = (qkv[h][:, 0:SB_WIDTH] * (SB_HEAD_DIM ** -0.5)).astype(BF16)
        qkv_ref[rows[h], SB_WIDTH:c0] = qkv[h][:, SB_WIDTH:c0].astype(BF16)
    gate_logits = [_dot(xb[h], w_ref[:, g0:g0 + 2 * d_model]) for h in chunks]
    branch_b = [_dot(gated[h], wbb_ref[...]) for h in chunks]
    for h in chunks:
        gates = jax.nn.sigmoid(gate_logits[h] + gbias_ref[...])
        ga_ref[rows[h], :] = gates[:, 0:d_model].astype(BF16)
        gbb_ref[rows[h], :] = (gates[:, d_model:2 * d_model] * branch_b[h]).astype(BF16)


def _proj_call(x2, w_in_b, gbias, conv_w, wbb_b, seq):
    t, d = x2.shape
    pw = w_in_b.shape[1]
    tm = PROJ_TILE
    sub = SUB_TILE
    cw = conv_w.shape[1]
    kern = functools.partial(_proj_kernel, tm=tm, sub=sub, tiles_per_seq=seq // tm, d_model=d)
    const = lambda i: (0, 0)
    return pl.pallas_call(
        kern,
        grid=(t // tm,),
        in_specs=[
            pl.BlockSpec((tm, d), lambda i: (i, 0)),
            pl.BlockSpec((d, pw), const),
            pl.BlockSpec((1, 2 * d), const),
            pl.BlockSpec((3, cw), const),
            pl.BlockSpec((cw, d), const),
        ],
        out_specs=[
            pl.BlockSpec((tm, 3 * SB_WIDTH), lambda i: (i, 0)),
            pl.BlockSpec((tm, d), lambda i: (i, 0)),
            pl.BlockSpec((tm, d), lambda i: (i, 0)),
        ],
        out_shape=[
            jax.ShapeDtypeStruct((t, 3 * SB_WIDTH), BF16),
            jax.ShapeDtypeStruct((t, d), BF16),
            jax.ShapeDtypeStruct((t, d), BF16),
        ],
        scratch_shapes=[pltpu.VMEM((sub + 8, cw), F32)],
        compiler_params=pltpu.CompilerParams(
            dimension_semantics=("arbitrary",), vmem_limit_bytes=VMEM_LIMIT),
        name="proj",
    )(x2, w_in_b, gbias, conv_w, wbb_b)


def _attn_kernel(q_ref, k_ref, v_ref, o_ref, acc_ref, c_ref, *, ts, tq, unroll, lag):
    qi = pl.program_id(1)
    assert ts == LANES
    subs = range(ts // tq)
    lane_q = lax.broadcasted_iota(jnp.int32, (tq, HEAD_PAIR), 1)
    row = lax.broadcasted_iota(jnp.int32, (ts, ts), 0)
    col = lax.broadcasted_iota(jnp.int32, (ts, ts), 1)
    later = (row > col).astype(BF16)
    later2 = jnp.concatenate([later, later], axis=0)
    later2 = jnp.concatenate([later2, jnp.ones((2 * ts, LANES), BF16)], axis=1)
    kcol = lax.broadcasted_iota(jnp.int32, (tq, ts), 1)
    ahead = kcol - lax.broadcasted_iota(jnp.int32, (tq, ts), 0)
    halves = HEAD_PAIR // SB_HEAD_DIM
    in_head_q = [(lane_q >= h * SB_HEAD_DIM) & (lane_q < (h + 1) * SB_HEAD_DIM) for h in range(halves)]
    zero_q = jnp.zeros((tq, HEAD_PAIR), BF16)

    acc_ref[...] = jnp.zeros(acc_ref.shape, F32)
    c_ref[...] = jnp.zeros(c_ref.shape, F32)

    pairs = range(SB_WIDTH // HEAD_PAIR)
    cols = [slice(p * HEAD_PAIR, (p + 1) * HEAD_PAIR) for p in pairs]
    rows = [slice(s * tq, (s + 1) * tq) for s in subs]

    def sweep(w0):
        tiles = [(s, k) for s in subs for k in range(unroll)]
        start, bias = {}, {}
        for s, k in tiles:
            p0 = qi * ts + s * tq
            first = p0 + tq - ts * (w0 + k + 1)
            st = jnp.maximum(first, 0)
            start[s, k] = pl.multiple_of(st, tq)
            ok = (ahead < p0 - st) & (kcol < first + ts - st)
            bias[s, k] = jnp.where(ok, 0.0, MASKED_LOGIT)
        z, sp, split, inner, wb = {}, {}, {}, {}, {}
        groups = [(p, s) for p in pairs for s in subs]
        windows = range(unroll)

        def logits(g):
            p, s = groups[g]
            q = q_ref[rows[s], cols[p]]
            q2 = jnp.concatenate([jnp.where(in_head_q[h], q, zero_q) for h in range(halves)], axis=0)
            for k in windows:
                b2 = jnp.concatenate([bias[s, k]] * halves, axis=0)
                z[g, k] = _dot_nt(q2, k_ref[pl.ds(start[s, k], ts), cols[p]]) + b2

        def stay(g):
            for k in windows:
                zz = z[g, k]
                a = jnp.maximum(zz, 0.0) + jnp.log(1.0 + jnp.exp2(jnp.abs(zz) * (-LOG2E)))
                hi = a.astype(BF16)
                sp[g, k] = a
                split[g, k] = jnp.concatenate([hi, (a - hi.astype(F32)).astype(BF16)], axis=1)

        def tails(g):
            both = _dot(jnp.concatenate([split[g, k] for k in windows], axis=0), later2)
            for k in windows:
                inner[g, k] = both[k * halves * tq:(k + 1) * halves * tq, :]

        def weights(g):
            cvec = c_ref[g]
            for k in windows:
                c = (g, k)
                wb[c] = jnp.exp(z[c] - sp[c] - inner[c][:, 0:ts] - cvec).astype(BF16)
                cvec = cvec + inner[c][:, ts:ts + LANES]
            c_ref[g] = cvec

        def values(g):
            p, s = groups[g]
            out = None
            for k in windows:
                res = _dot(wb[g, k], v_ref[pl.ds(start[s, k], ts), cols[p]])
                pv = jnp.where(in_head_q[0], res[0:tq, :], res[tq:2 * tq, :])
                out = pv if out is None else out + pv
            acc_ref[rows[s], cols[p]] += out

        stages = (logits, stay, tails, weights, values)
        for step in range(len(groups) + lag * (len(stages) - 1)):
            for s in reversed(range(len(stages))):
                g = step - lag * s
                if 0 <= g < len(groups):
                    stages[s](g)

    def cond(carry):
        w0, cmin = carry
        return jnp.logical_and(w0 <= qi, cmin < TAIL_CUTOFF)

    def body(carry):
        w0, _ = carry
        sweep(w0)
        return w0 + unroll, jnp.min(c_ref[...])

    lax.while_loop(cond, body, (jnp.int32(0), jnp.float32(0.0)))
    o_ref[...] = acc_ref[...].astype(BF16)


def _attn_call(qkv3):
    b, s, _ = qkv3.shape
    ts = ATTN_TILE
    kern = functools.partial(_attn_kernel, ts=ts, tq=ATTN_SUB, unroll=ATTN_UNROLL, lag=ATTN_LAG)
    return pl.pallas_call(
        kern,
        grid=(b, s // ts),
        in_specs=[
            pl.BlockSpec((None, ts, SB_WIDTH), lambda bi, qi: (bi, qi, 0)),
            pl.BlockSpec((None, s, SB_WIDTH), lambda bi, qi: (bi, 0, 1)),
            pl.BlockSpec((None, s, SB_WIDTH), lambda bi, qi: (bi, 0, 2)),
        ],
        out_specs=pl.BlockSpec((None, ts, SB_WIDTH), lambda bi, qi: (bi, qi, 0)),
        out_shape=jax.ShapeDtypeStruct((b, s, SB_WIDTH), BF16),
        scratch_shapes=[pltpu.VMEM((ts, SB_WIDTH), F32), pltpu.VMEM((SB_HEADS, ts, LANES), F32)],
        compiler_params=pltpu.CompilerParams(
            dimension_semantics=("arbitrary", "arbitrary"), vmem_limit_bytes=VMEM_LIMIT),
        name="attn",
    )(qkv3, qkv3, qkv3)


def _mix_kernel(attn_ref, ga_ref, gbb_ref, x_ref, wa_ref, wo_ref, g1_ref, b1_ref, wrt_ref, brt_ref,
                h1_ref, meta_ref, xls_ref, tcnt_ref, *, tile, tm, alpha):
    chunks = range(tile // tm)
    reps = tm // LANES
    rows = [slice(h * tm, (h + 1) * tm) for h in chunks]
    branch_a = [_dot(attn_ref[rows[h], :], wa_ref[...]) for h in chunks]
    mixed = [_dot((ga_ref[rows[h], :] * branch_a[h] + gbb_ref[rows[h], :]).astype(BF16), wo_ref[...])
             for h in chunks]
    h1b = []
    for h in chunks:
        h1 = _layer_norm(alpha * x_ref[rows[h], :] + mixed[h], g1_ref[...], b1_ref[...])
        h1_ref[rows[h], :] = h1
        h1b.append(h1.astype(BF16))
    bias = jnp.concatenate([brt_ref[...]] * reps, axis=1)
    logits = [_dot_nt(wrt_ref[...], h1b[h]) + bias for h in chunks]
    route = [_route(logits[h], tm) for h in chunks]

    trow = lax.broadcasted_iota(jnp.int32, (tm, tm), 0)
    tcol = lax.broadcasted_iota(jnp.int32, (tm, tm), 1)
    earlier = (trow < tcol).astype(BF16)
    e32r = lax.broadcasted_iota(jnp.int32, (N_EXPERTS, N_EXPERTS), 0)
    e32c = lax.broadcasted_iota(jnp.int32, (N_EXPERTS, N_EXPERTS), 1)
    lower = (e32c < e32r).astype(BF16)
    ones_b = jnp.ones((tm, LANES), BF16)
    before_tile = [_dot(route[h].onehot, earlier) for h in chunks]
    tile_cnt = [_dot(route[h].onehot, ones_b) for h in chunks]
    smaller = [_dot(lower, route[h].onehot).astype(BF16) for h in chunks]
    seg_off = [_dot(smaller[h], ones_b) for h in chunks]

    zrow = jnp.zeros((1, tm), F32)
    perm = []
    for h in chunks:
        rt = route[h]
        tcnt_ref[h] = tile_cnt[h]
        local = before_tile[h] + jnp.concatenate([seg_off[h]] * reps, axis=1)
        pos1 = jnp.sum(jnp.where(rt.is1, local, 0.0), axis=0, keepdims=True)
        pos2 = jnp.sum(jnp.where(rt.is2, local, 0.0), axis=0, keepdims=True)
        meta_ref[:, rows[h]] = jnp.concatenate(
            [rt.e1.astype(F32), rt.e2.astype(F32), rt.w1, rt.w2, pos1, pos2, zrow, zrow], axis=0)
        srow = lax.broadcasted_iota(jnp.int32, (2 * tm, tm), 0)
        perm.append(jnp.where((srow == pos1.astype(jnp.int32)) | (srow == pos2.astype(jnp.int32)), 1.0, 0.0)
                    .astype(BF16))
    xs = [_dot(perm[h], h1b[h]) for h in chunks]
    for h in chunks:
        base = h * 2 * tm * SUBLANES
        for c in range(xs[h].shape[1] // LANES):
            xls_ref[pl.ds(base + c, 2 * tm, stride=SUBLANES), :] = xs[h][:, c * LANES:(c + 1) * LANES]


class _Route(NamedTuple):
    e1: jax.Array
    e2: jax.Array
    w1: jax.Array
    w2: jax.Array
    is1: jax.Array
    is2: jax.Array
    onehot: jax.Array


def _route(lt, tm):
    r = [lt[k:k + 1, :] for k in range(N_GROUPS)]
    gmax = jnp.maximum(jnp.maximum(r[0], r[1]), jnp.maximum(r[2], r[3]))
    gidx = jnp.where(r[0] == gmax, 0, jnp.where(r[1] == gmax, 1, jnp.where(r[2] == gmax, 2, 3)))
    gsum = (jnp.exp(r[0] - gmax) + jnp.exp(r[1] - gmax)) + (jnp.exp(r[2] - gmax) + jnp.exp(r[3] - gmax))
    gprob = 1.0 / gsum
    epg = EXPERTS_PER_GROUP
    slabs = [lt[8 + g * epg:8 + (g + 1) * epg, :] for g in range(N_GROUPS)]
    el = jnp.where(gidx == 0, slabs[0], jnp.where(gidx == 1, slabs[1], jnp.where(gidx == 2, slabs[2], slabs[3])))
    r8 = lax.broadcasted_iota(jnp.int32, (epg, tm), 0)
    m1 = jnp.max(el, axis=0, keepdims=True)
    i1 = jnp.min(jnp.where(el == m1, r8, epg), axis=0, keepdims=True)
    el2 = jnp.where(r8 == i1, -jnp.inf, el)
    m2 = jnp.max(el2, axis=0, keepdims=True)
    i2 = jnp.min(jnp.where(el2 == m2, r8, epg), axis=0, keepdims=True)
    dlt = jnp.exp(m2 - m1)
    w1 = gprob / (1.0 + dlt)
    w2 = gprob * dlt / (1.0 + dlt)
    e1 = gidx * epg + i1
    e2 = gidx * epg + i2
    r32 = lax.broadcasted_iota(jnp.int32, (N_EXPERTS, tm), 0)
    is1 = r32 == e1
    is2 = r32 == e2
    onehot = jnp.where(is1 | is2, 1.0, 0.0).astype(BF16)
    return _Route(e1, e2, w1, w2, is1, is2, onehot)


def _mix_call(attn2, gate_a, gbb, x2, wa_b, wo_b, g1, b1, wrt_b, brt, alpha):
    t, d = x2.shape
    tile = PROJ_TILE
    kern = functools.partial(_mix_kernel, tile=tile, tm=SUB_TILE, alpha=alpha)
    const = lambda i: (0, 0)
    rowblk = lambda i: (i, 0)
    tm = tile
    return pl.pallas_call(
        kern,
        grid=(t // tm,),
        in_specs=[
            pl.BlockSpec((tm, SB_WIDTH), rowblk),
            pl.BlockSpec((tm, d), rowblk),
            pl.BlockSpec((tm, d), rowblk),
            pl.BlockSpec((tm, d), rowblk),
            pl.BlockSpec((SB_WIDTH, d), const),
            pl.BlockSpec((d, d), const),
            pl.BlockSpec((1, d), const),
            pl.BlockSpec((1, d), const),
            pl.BlockSpec((ROUTER_ROWS, d), const),
            pl.BlockSpec((ROUTER_ROWS, LANES), const),
        ],
        out_specs=[
            pl.BlockSpec((tm, d), rowblk),
            pl.BlockSpec((8, tm), lambda i: (0, i)),
            pl.BlockSpec((2 * tm * SUBLANES, LANES), rowblk),
            pl.BlockSpec((tile // SUB_TILE, N_EXPERTS, LANES), lambda i: (i, 0, 0)),
        ],
        out_shape=[
            jax.ShapeDtypeStruct((t, d), F32),
            jax.ShapeDtypeStruct((8, t), F32),
            jax.ShapeDtypeStruct((2 * t * SUBLANES, LANES), F32),
            jax.ShapeDtypeStruct((t // SUB_TILE, N_EXPERTS, LANES), F32),
        ],
        compiler_params=pltpu.CompilerParams(
            dimension_semantics=("arbitrary",), vmem_limit_bytes=VMEM_LIMIT),
        name="mix",
    )(attn2, gate_a, gbb, x2, wa_b, wo_b, g1, b1, wrt_b, brt)


def _row_copy(src_ref, src_row, dst_ref, dst_row, sem):
    return pltpu.make_async_copy(src_ref.at[pl.ds(src_row, 1), :], dst_ref.at[pl.ds(dst_row, 1), :], sem)


def _tok_rows(tok, n_tok):
    return pl.ds(pl.multiple_of(tok * SUBLANES, SUBLANES), pl.multiple_of(n_tok * SUBLANES, SUBLANES))


def _moe_kernel(be_ref, r0_ref, t0_ref, t1_ref, nv_ref, nu_ref, nxt_ref, cumt_ref, soff_ref,
                xls_ref, wg_ref, wu_ref, wd_ref, y_ref,
                xbuf_ref, zero_ref, sem, wgf_ref, wuf_ref, wdf_ref, wsem, wgb_ref, wub_ref, wdb_ref, state_ref,
                *, blk, sub, tile_rows):
    i = pl.program_id(0)
    n_used = nu_ref[0]
    e = be_ref[i]
    cur = i % 2
    ne = N_EXPERTS
    w_hbm = (wg_ref, wu_ref, wd_ref)
    wf_refs = (wgf_ref, wuf_ref, wdf_ref)

    def fetch_weights(expert, slot):
        for src, dst in zip(w_hbm, wf_refs):
            pltpu.make_async_copy(src.at[expert], dst.at[slot], wsem.at[slot]).start()

    def wait_weights(slot):
        for src, dst in zip(w_hbm, wf_refs):
            pltpu.make_async_copy(src.at[0], dst.at[slot], wsem.at[slot]).wait()

    def gather(b, slot):
        eb = be_ref[b]
        r0 = r0_ref[b]
        nv = nv_ref[b]

        def seg(t, c):
            first = cumt_ref[t * ne + eb]
            lo = jnp.maximum(first, r0)
            n = jnp.minimum(cumt_ref[(t + 1) * ne + eb], r0 + nv) - lo

            @pl.when(n > 0)
            def _():
                src = t * tile_rows + soff_ref[t * ne + eb] + (lo - first)
                pltpu.make_async_copy(xls_ref.at[_tok_rows(src, n), :],
                                      xbuf_ref.at[slot, _tok_rows(lo - r0, n), :], sem.at[slot]).start()
            return c

        lax.fori_loop(t0_ref[b], t1_ref[b] + 1, seg, 0)

        @pl.when(nv < blk)
        def _():
            pltpu.make_async_copy(zero_ref.at[_tok_rows(0, blk - nv), :],
                                  xbuf_ref.at[slot, _tok_rows(nv, blk - nv), :], sem.at[slot]).start()

    def drain(slot):
        pltpu.make_async_copy(xls_ref.at[pl.ds(0, blk * SUBLANES), :], xbuf_ref.at[slot], sem.at[slot]).wait()

    @pl.when(i == 0)
    def _():
        state_ref[0] = -1
        state_ref[1] = 0
        zero_ref[...] = jnp.zeros(zero_ref.shape, F32)

    @pl.when(jnp.logical_and(i == 0, n_used > 0))
    def _():
        gather(0, 0)
        fetch_weights(e, 0)

    @pl.when(i < n_used)
    def _():
        @pl.when(e != state_ref[0])
        def _():
            slot = state_ref[1]
            wait_weights(slot)
            wgb_ref[...] = wf_refs[0][slot].astype(BF16)
            wub_ref[...] = wf_refs[1][slot].astype(BF16)
            wdb_ref[...] = wf_refs[2][slot].astype(BF16)
            state_ref[0] = e
            state_ref[1] = 1 - slot

            @pl.when(nxt_ref[i] >= 0)
            def _():
                fetch_weights(nxt_ref[i], 1 - slot)

        gather(jnp.minimum(i + 1, n_used - 1), 1 - cur)
        drain(cur)
        xbs = [jnp.concatenate([xbuf_ref[cur, pl.ds(h * sub * SUBLANES + c, sub, stride=SUBLANES), :]
                                for c in range(SUBLANES)], axis=1).astype(BF16) for h in range(blk // sub)]
        gate = [_dot(xb, wgb_ref[...]) for xb in xbs]
        up = [_dot(xb, wub_ref[...]) for xb in xbs]
        hidden = [(g * jax.nn.sigmoid(g) * u).astype(BF16) for g, u in zip(gate, up)]
        for h, hid in enumerate(hidden):
            y = _dot(hid, wdb_ref[...])
            for c in range(SUBLANES):
                y_ref[pl.ds(h * sub * SUBLANES + c, sub, stride=SUBLANES), :] = y[:, c * LANES:(c + 1) * LANES]

    @pl.when(i == n_used - 1)
    def _():
        drain(1 - cur)

    @pl.when(i >= n_used)
    def _():
        y_ref[...] = jnp.zeros(y_ref.shape, F32)


def _moe_call(tables, xls, tile_rows, w_gate, w_up, w_down):
    n_blocks = tables[0].shape[0]
    blk = EXPERT_BLOCK
    _, d, de = w_gate.shape
    grid_spec = pltpu.PrefetchScalarGridSpec(
        num_scalar_prefetch=len(tables),
        grid=(n_blocks,),
        in_specs=[pl.BlockSpec(memory_space=pl.ANY)] * 4,
        out_specs=pl.BlockSpec((blk * SUBLANES, LANES), lambda i, *_: (i, 0)),
        scratch_shapes=[
            pltpu.VMEM((2, blk * SUBLANES, LANES), F32), pltpu.VMEM((blk * SUBLANES, LANES), F32),
            pltpu.SemaphoreType.DMA((2,)),
            pltpu.VMEM((2, d, de), F32), pltpu.VMEM((2, d, de), F32), pltpu.VMEM((2, de, d), F32),
            pltpu.SemaphoreType.DMA((2,)),
            pltpu.VMEM((d, de), BF16), pltpu.VMEM((d, de), BF16), pltpu.VMEM((de, d), BF16),
            pltpu.SMEM((2,), jnp.int32),
        ],
    )
    return pl.pallas_call(
        functools.partial(_moe_kernel, blk=blk, sub=SUB_TILE, tile_rows=tile_rows),
        grid_spec=grid_spec,
        out_shape=jax.ShapeDtypeStruct((n_blocks * blk * SUBLANES, LANES), F32),
        compiler_params=pltpu.CompilerParams(
            dimension_semantics=("arbitrary",), vmem_limit_bytes=VMEM_LIMIT),
        name="moe",
    )(*tables, xls, w_gate, w_up, w_down)


def _combine_kernel(pstart_ref, cumt_ref, soff_ref, tcnt_ref,
                    h1_ref, meta_ref, y_ref, g2_ref, b2_ref, o_ref, ybuf_ref, sem, *, tile, tm, alpha):
    i = pl.program_id(0)
    last = pl.num_programs(0) - 1
    cur = i % 2
    ne = N_EXPERTS
    chunks = range(tile // tm)
    seg_rows = 2 * tm * SUBLANES

    def gather(step, slot):
        for h in chunks:
            t = step * len(chunks) + h

            def seg(e, c, t=t, h=h):
                n = tcnt_ref[t * ne + e]

                @pl.when(n > 0)
                def _():
                    src = pstart_ref[e] + cumt_ref[t * ne + e]
                    dst = h * 2 * tm + soff_ref[t * ne + e]
                    pltpu.make_async_copy(y_ref.at[_tok_rows(src, n), :],
                                          ybuf_ref.at[slot, _tok_rows(dst, n), :], sem.at[slot]).start()
                return c

            lax.fori_loop(0, ne, seg, 0)

    def drain(slot):
        pltpu.make_async_copy(y_ref.at[pl.ds(0, len(chunks) * seg_rows), :], ybuf_ref.at[slot],
                              sem.at[slot]).wait()

    @pl.when(i == 0)
    def _():
        gather(0, 0)

    gather(jnp.minimum(i + 1, last), 1 - cur)
    drain(cur)
    srow = lax.broadcasted_iota(jnp.int32, (2 * tm, tm), 0)
    ys = [jnp.concatenate([ybuf_ref[cur, pl.ds(h * seg_rows + c, 2 * tm, stride=SUBLANES), :]
                           for c in range(SUBLANES)], axis=1).astype(BF16) for h in chunks]
    picked = {}
    for h in chunks:
        cols = slice(h * tm, (h + 1) * tm)
        for k in range(2):
            pos = meta_ref[4 + k:5 + k, cols].astype(jnp.int32)
            picked[h, k] = _dot_tn(jnp.where(srow == pos, 1.0, 0.0).astype(BF16), ys[h])
    for h in chunks:
        rows = slice(h * tm, (h + 1) * tm)
        rw = meta_ref[:, rows].T
        ffn = rw[:, 2:3] * picked[h, 0] + rw[:, 3:4] * picked[h, 1]
        o_ref[rows, :] = _layer_norm(alpha * h1_ref[rows, :] + ffn, g2_ref[...], b2_ref[...])

    @pl.when(i == last)
    def _():
        drain(1 - cur)


def _combine_call(tables, h1, meta, y_slots, g2, b2, alpha):
    t, d = h1.shape
    tm = PROJ_TILE
    kern = functools.partial(_combine_kernel, tile=tm, tm=SUB_TILE, alpha=alpha)
    const = lambda i, *_: (0, 0)
    grid_spec = pltpu.PrefetchScalarGridSpec(
        num_scalar_prefetch=len(tables),
        grid=(t // tm,),
        in_specs=[
            pl.BlockSpec((tm, d), lambda i, *_: (i, 0)),
            pl.BlockSpec((8, tm), lambda i, *_: (0, i)),
            pl.BlockSpec(memory_space=pl.ANY),
            pl.BlockSpec((1, d), const),
            pl.BlockSpec((1, d), const),
        ],
        out_specs=pl.BlockSpec((tm, d), lambda i, *_: (i, 0)),
        scratch_shapes=[pltpu.VMEM((2, 2 * tm * SUBLANES, LANES), F32), pltpu.SemaphoreType.DMA((2,))],
    )
    return pl.pallas_call(
        kern,
        grid_spec=grid_spec,
        out_shape=jax.ShapeDtypeStruct((t, d), F32),
        compiler_params=pltpu.CompilerParams(
            dimension_semantics=("arbitrary",), vmem_limit_bytes=VMEM_LIMIT),
        name="combine",
    )(*tables, h1, meta, y_slots, g2, b2)


def _layer(h, w_in, gate_bias, conv_w, w_branch_a, w_branch_b, w_out, ln1_g, ln1_b,
           w_router_g, b_router_g, w_router_e, b_router_e, w_gate, w_up, w_down, ln2_g, ln2_b, alpha):
    bsz, seq, d = h.shape
    t = bsz * seq
    x2 = h.reshape(t, d)

    qkv, gate_a, gbb = _proj_call(x2, w_in.astype(BF16), gate_bias.reshape(1, 2 * d), conv_w,
                                  w_branch_b.astype(BF16), seq)
    attn = _attn_call(qkv.reshape(bsz, seq, 3 * SB_WIDTH)).reshape(t, SB_WIDTH)

    wrt = jnp.zeros((ROUTER_ROWS, d), F32).at[0:N_GROUPS].set(w_router_g.T).at[8:].set(w_router_e.T)
    brt = jnp.zeros((ROUTER_ROWS,), F32).at[0:N_GROUPS].set(b_router_g).at[8:].set(b_router_e.reshape(-1))
    brt = jnp.broadcast_to(brt[:, None], (ROUTER_ROWS, LANES))
    h1, meta, xls, tcnt = _mix_call(attn, gate_a, gbb, x2, w_branch_a.astype(BF16), w_out.astype(BF16),
                                    ln1_g.reshape(1, d), ln1_b.reshape(1, d), wrt.astype(BF16), brt, alpha)

    blk = EXPERT_BLOCK
    tile_cnt = tcnt[:, :, 0].astype(jnp.int32)
    n_tiles = tile_cnt.shape[0]
    cum_incl = jnp.cumsum(tile_cnt, axis=0)
    cumt = jnp.concatenate([jnp.zeros((1, N_EXPERTS), jnp.int32), cum_incl], axis=0)
    counts = cumt[-1]
    soff = jnp.cumsum(tile_cnt, axis=1) - tile_cnt
    padded = (counts + blk - 1) // blk * blk
    pad_end = jnp.cumsum(padded)
    pad_start = pad_end - padded
    n_blocks = (2 * t) // blk + N_EXPERTS
    block_start = jnp.arange(n_blocks, dtype=jnp.int32) * blk
    block_expert = jnp.minimum(jnp.sum(pad_end[None, :] <= block_start[:, None], axis=1), N_EXPERTS - 1)
    n_used = (pad_end[-1] // blk).reshape(1)

    be = block_expert.astype(jnp.int32)
    mine = be[:, None] == jnp.arange(N_EXPERTS, dtype=jnp.int32)[None, :]

    def of_block(table):
        return jnp.sum(jnp.where(mine, table[..., None, :], 0), axis=-1)

    r0 = block_start - of_block(pad_start)
    nv = jnp.clip(of_block(counts) - r0, 0, blk)
    t0 = jnp.minimum(jnp.sum(of_block(cum_incl) <= r0[None, :], axis=0), n_tiles - 1)
    t1 = jnp.sum(of_block(cumt[:-1]) < (r0 + nv)[None, :], axis=0) - 1
    after = of_block(pad_end) // blk
    be_after = jnp.sum(jnp.where(after[:, None] == jnp.arange(n_blocks, dtype=jnp.int32)[None, :], be[None, :], 0),
                       axis=1)
    nxt = jnp.where(after < n_used[0], be_after, -1)
    tables = (be, r0, t0, t1, nv, n_used, nxt, cumt.reshape(-1), soff.reshape(-1))
    y_slots = _moe_call(tuple(a.astype(jnp.int32) for a in tables), xls, 2 * SUB_TILE, w_gate, w_up, w_down)
    ctables = (pad_start, cumt.reshape(-1), soff.reshape(-1), tile_cnt.reshape(-1))
    out = _combine_call(tuple(a.astype(jnp.int32) for a in ctables), h1, meta, y_slots,
                        ln2_g.reshape(1, d), ln2_b.reshape(1, d), alpha)
    return out.reshape(bsz, seq, d)


def kernel(x, w_in, gate_bias, conv_w, w_branch_a, w_branch_b, w_out, ln1_g, ln1_b, w_router_g, b_router_g,
           w_router_e, b_router_e, w_gate, w_up, w_down, ln2_g, ln2_b):
    depth = w_in.shape[0]
    alpha = (2.0 * depth) ** 0.25
    h = x
    for l in range(depth):
        h = _layer(h, w_in[l], gate_bias[l], conv_w[l], w_branch_a[l], w_branch_b[l], w_out[l], ln1_g[l],
                   ln1_b[l], w_router_g[l], b_router_g[l], w_router_e[l], b_router_e[l], w_gate[l], w_up[l],
                   w_down[l], ln2_g[l], ln2_b[l], alpha)
    return h
```

```python
import functools
from typing import NamedTuple

import jax
import jax.numpy as jnp
from jax import lax
from jax.experimental import pallas as pl
from jax.experimental.pallas import tpu as pltpu

F32 = jnp.float32
BF16 = jnp.bfloat16

SB_HEADS = 8
SB_HEAD_DIM = 64
SB_WIDTH = SB_HEADS * SB_HEAD_DIM
N_GROUPS = 4
EXPERTS_PER_GROUP = 8
N_EXPERTS = N_GROUPS * EXPERTS_PER_GROUP
LN_EPS = 1e-5

LANES = 128
SUBLANES = 8
HEAD_PAIR = LANES
PROJ_TILE = 1024
SUB_TILE = 256
ATTN_TILE = 256
ATTN_WINDOW = 128
ATTN_SUB = 64
ATTN_UNROLL = 2
ATTN_LAG = 2
EXPERT_BLOCK = 512
ROUTER_ROWS = 8 + N_EXPERTS
TAIL_CUTOFF = 110.0
MASKED_LOGIT = -1e30
LOG2E = 1.4426950408889634
VMEM_LIMIT = 56 * 1024 * 1024


def _dot(a, b):
    return jnp.dot(a, b, preferred_element_type=F32)


def _dot_nt(a, b):
    return lax.dot_general(a, b, (((1,), (1,)), ((), ())), preferred_element_type=F32)


def _dot_tn(a, b):
    return lax.dot_general(a, b, (((0,), (0,)), ((), ())), preferred_element_type=F32)


def _layer_norm(y, g, b):
    mu = jnp.mean(y, axis=-1, keepdims=True)
    d = y - mu
    var = jnp.mean(d * d, axis=-1, keepdims=True)
    return d * lax.rsqrt(var + LN_EPS) * g + b


def _proj_kernel(x_ref, w_ref, gbias_ref, convw_ref, wbb_ref, qkv_ref, ga_ref, gbb_ref, ubuf_ref,
                 *, tm, sub, seq, d_model):
    i = pl.program_id(0)
    cw = SB_WIDTH
    c0 = 3 * SB_WIDTH
    g0 = c0 + 3 * cw
    cwt = convw_ref[...]

    @pl.when(i == 0)
    def _():
        ubuf_ref[0:8, :] = jnp.zeros((8, cw), F32)

    chunks = range(tm // sub)
    rows = [slice(h * sub, (h + 1) * sub) for h in chunks]
    xb = [x_ref[rows[h], :].astype(BF16) for h in chunks]
    conv_in = [_dot(xb[h], w_ref[:, c0:g0]) for h in chunks]
    qkv = [_dot(xb[h], w_ref[:, 0:c0]) for h in chunks]
    gated = []
    for h in chunks:
        cb = conv_in[h][:, 0:cw]
        u = conv_in[h][:, cw:2 * cw] * conv_in[h][:, 2 * cw:3 * cw]
        ubuf_ref[0:8, :] = jnp.where((i * tm + h * sub) % seq == 0, 0.0, ubuf_ref[0:8, :])
        ubuf_ref[8:sub + 8, :] = u
        y = (cwt[0:1, :] * ubuf_ref[pl.ds(6, sub), :] + cwt[1:2, :] * ubuf_ref[pl.ds(7, sub), :]
             + cwt[2:3, :] * u)
        ubuf_ref[0:8, :] = u[sub - 8:sub, :]
        gated.append((cb * y).astype(BF16))
        qkv_ref[rows[h], 0:SB_WIDTH] = (qkv[h][:, 0:SB_WIDTH] * (SB_HEAD_DIM ** -0.5)).astype(BF16)
        qkv_ref[rows[h], SB_WIDTH:c0] = qkv[h][:, SB_WIDTH:c0].astype(BF16)
    gate_logits = [_dot(xb[h], w_ref[:, g0:g0 + 2 * d_model]) for h in chunks]
    branch_b = [_dot(gated[h], wbb_ref[...]) for h in chunks]
    for h in chunks:
        gates = jax.nn.sigmoid(gate_logits[h] + gbias_ref[...])
        ga_ref[rows[h], :] = gates[:, 0:d_model].astype(BF16)
        gbb_ref[rows[h], :] = (gates[:, d_model:2 * d_model] * branch_b[h]).astype(BF16)


def _proj_call(x2, w_in_b, gbias, conv_w, wbb_b, seq):
    t, d = x2.shape
    pw = w_in_b.shape[1]
    tm = PROJ_TILE
    sub = SUB_TILE
    cw = conv_w.shape[1]
    assert seq % sub == 0 and t % tm == 0
    kern = functools.partial(_proj_kernel, tm=tm, sub=sub, seq=seq, d_model=d)
    const = lambda i: (0, 0)
    return pl.pallas_call(
        kern,
        grid=(t // tm,),
        in_specs=[
            pl.BlockSpec((tm, d), lambda i: (i, 0)),
            pl.BlockSpec((d, pw), const),
            pl.BlockSpec((1, 2 * d), const),
            pl.BlockSpec((3, cw), const),
            pl.BlockSpec((cw, d), const),
        ],
        out_specs=[
            pl.BlockSpec((tm, 3 * SB_WIDTH), lambda i: (i, 0)),
            pl.BlockSpec((tm, d), lambda i: (i, 0)),
            pl.BlockSpec((tm, d), lambda i: (i, 0)),
        ],
        out_shape=[
            jax.ShapeDtypeStruct((t, 3 * SB_WIDTH), BF16),
            jax.ShapeDtypeStruct((t, d), BF16),
            jax.ShapeDtypeStruct((t, d), BF16),
        ],
        scratch_shapes=[pltpu.VMEM((sub + 8, cw), F32)],
        compiler_params=pltpu.CompilerParams(
            dimension_semantics=("arbitrary",), vmem_limit_bytes=VMEM_LIMIT),
        name="proj",
    )(x2, w_in_b, gbias, conv_w, wbb_b)


def _attn_kernel(q_ref, k_ref, v_ref, o_ref, acc_ref, c_ref, *, tb, ts, tq, unroll, lag):
    qi = pl.program_id(1)
    assert ts == LANES
    subs = range(tb // tq)
    lane_q = lax.broadcasted_iota(jnp.int32, (tq, HEAD_PAIR), 1)
    row = lax.broadcasted_iota(jnp.int32, (ts, ts), 0)
    col = lax.broadcasted_iota(jnp.int32, (ts, ts), 1)
    later = (row > col).astype(BF16)
    later2 = jnp.concatenate([later, later], axis=0)
    later2 = jnp.concatenate([later2, jnp.ones((2 * ts, LANES), BF16)], axis=1)
    kcol = lax.broadcasted_iota(jnp.int32, (tq, ts), 1)
    ahead = kcol - lax.broadcasted_iota(jnp.int32, (tq, ts), 0)
    halves = HEAD_PAIR // SB_HEAD_DIM
    in_head_q = [(lane_q >= h * SB_HEAD_DIM) & (lane_q < (h + 1) * SB_HEAD_DIM) for h in range(halves)]
    zero_q = jnp.zeros((tq, HEAD_PAIR), BF16)

    acc_ref[...] = jnp.zeros(acc_ref.shape, F32)
    c_ref[...] = jnp.zeros(c_ref.shape, F32)

    pairs = range(SB_WIDTH // HEAD_PAIR)
    cols = [slice(p * HEAD_PAIR, (p + 1) * HEAD_PAIR) for p in pairs]
    rows = [slice(s * tq, (s + 1) * tq) for s in subs]

    def sweep(w0):
        tiles = [(s, k) for s in subs for k in range(unroll)]
        start, bias = {}, {}
        for s, k in tiles:
            p0 = qi * tb + s * tq
            first = p0 + tq - ts * (w0 + k + 1)
            st = jnp.maximum(first, 0)
            start[s, k] = pl.multiple_of(st, tq)
            ok = (ahead < p0 - st) & (kcol < first + ts - st)
            bias[s, k] = jnp.where(ok, 0.0, MASKED_LOGIT)
        z, sp, split, inner, wb = {}, {}, {}, {}, {}
        groups = [(p, s) for p in pairs for s in subs]
        windows = range(unroll)

        def logits(g):
            p, s = groups[g]
            q = q_ref[rows[s], cols[p]]
            q2 = jnp.concatenate([jnp.where(in_head_q[h], q, zero_q) for h in range(halves)], axis=0)
            for k in windows:
                b2 = jnp.concatenate([bias[s, k]] * halves, axis=0)
                z[g, k] = _dot_nt(q2, k_ref[pl.ds(start[s, k], ts), cols[p]]) + b2

        def stay(g):
            for k in windows:
                zz = z[g, k]
                a = jnp.maximum(zz, 0.0) + jnp.log(1.0 + jnp.exp2(jnp.abs(zz) * (-LOG2E)))
                hi = a.astype(BF16)
                sp[g, k] = a
                split[g, k] = jnp.concatenate([hi, (a - hi.astype(F32)).astype(BF16)], axis=1)

        def tails(g):
            both = _dot(jnp.concatenate([split[g, k] for k in windows], axis=0), later2)
            for k in windows:
                inner[g, k] = both[k * halves * tq:(k + 1) * halves * tq, :]

        def weights(g):
            cvec = c_ref[g]
            for k in windows:
                c = (g, k)
                wb[c] = jnp.exp(z[c] - sp[c] - inner[c][:, 0:ts] - cvec).astype(BF16)
                cvec = cvec + inner[c][:, ts:ts + LANES]
            c_ref[g] = cvec

        def values(g):
            p, s = groups[g]
            out = None
            for k in windows:
                res = _dot(wb[g, k], v_ref[pl.ds(start[s, k], ts), cols[p]])
                pv = jnp.where(in_head_q[0], res[0:tq, :], res[tq:2 * tq, :])
                out = pv if out is None else out + pv
            acc_ref[rows[s], cols[p]] += out

        stages = (logits, stay, tails, weights, values)
        for step in range(len(groups) + lag * (len(stages) - 1)):
            for s in reversed(range(len(stages))):
                g = step - lag * s
                if 0 <= g < len(groups):
                    stages[s](g)

    def cond(carry):
        w0, cmin = carry
        return jnp.logical_and(w0 * ts < (qi + 1) * tb, cmin < TAIL_CUTOFF)

    def body(carry):
        w0, _ = carry
        sweep(w0)
        return w0 + unroll, jnp.min(c_ref[...])

    lax.while_loop(cond, body, (jnp.int32(0), jnp.float32(0.0)))
    o_ref[...] = acc_ref[...].astype(BF16)


def _attn_call(qkv3):
    b, s, _ = qkv3.shape
    tb = ATTN_TILE
    tq = ATTN_SUB
    n_groups = (SB_WIDTH // HEAD_PAIR) * (tb // tq)
    kern = functools.partial(_attn_kernel, tb=tb, ts=ATTN_WINDOW, tq=tq, unroll=ATTN_UNROLL, lag=ATTN_LAG)
    return pl.pallas_call(
        kern,
        grid=(b, s // tb),
        in_specs=[
            pl.BlockSpec((None, tb, SB_WIDTH), lambda bi, qi: (bi, qi, 0)),
            pl.BlockSpec((None, s, SB_WIDTH), lambda bi, qi: (bi, 0, 1)),
            pl.BlockSpec((None, s, SB_WIDTH), lambda bi, qi: (bi, 0, 2)),
        ],
        out_specs=pl.BlockSpec((None, tb, SB_WIDTH), lambda bi, qi: (bi, qi, 0)),
        out_shape=jax.ShapeDtypeStruct((b, s, SB_WIDTH), BF16),
        scratch_shapes=[pltpu.VMEM((tb, SB_WIDTH), F32),
                        pltpu.VMEM((n_groups, (HEAD_PAIR // SB_HEAD_DIM) * tq, LANES), F32)],
        compiler_params=pltpu.CompilerParams(
            dimension_semantics=("arbitrary", "arbitrary"), vmem_limit_bytes=VMEM_LIMIT),
        name="attn",
    )(qkv3, qkv3, qkv3)


def _mix_kernel(attn_ref, ga_ref, gbb_ref, x_ref, wa_ref, wo_ref, g1_ref, b1_ref, wrt_ref, brt_ref,
                h1_ref, meta_ref, xls_ref, tcnt_ref, *, tile, tm, alpha):
    chunks = range(tile // tm)
    reps = tm // LANES
    rows = [slice(h * tm, (h + 1) * tm) for h in chunks]
    branch_a = [_dot(attn_ref[rows[h], :], wa_ref[...]) for h in chunks]
    mixed = [_dot((ga_ref[rows[h], :] * branch_a[h] + gbb_ref[rows[h], :]).astype(BF16), wo_ref[...])
             for h in chunks]
    h1b = []
    for h in chunks:
        h1 = _layer_norm(alpha * x_ref[rows[h], :] + mixed[h], g1_ref[...], b1_ref[...])
        h1_ref[rows[h], :] = h1
        h1b.append(h1.astype(BF16))
    bias = jnp.concatenate([brt_ref[...]] * reps, axis=1)
    logits = [_dot_nt(wrt_ref[...], h1b[h]) + bias for h in chunks]
    route = [_route(logits[h], tm) for h in chunks]

    trow = lax.broadcasted_iota(jnp.int32, (tm, tm), 0)
    tcol = lax.broadcasted_iota(jnp.int32, (tm, tm), 1)
    earlier = (trow < tcol).astype(BF16)
    e32r = lax.broadcasted_iota(jnp.int32, (N_EXPERTS, N_EXPERTS), 0)
    e32c = lax.broadcasted_iota(jnp.int32, (N_EXPERTS, N_EXPERTS), 1)
    lower = (e32c < e32r).astype(BF16)
    ones_b = jnp.ones((tm, LANES), BF16)
    before_tile = [_dot(route[h].onehot, earlier) for h in chunks]
    tile_cnt = [_dot(route[h].onehot, ones_b) for h in chunks]
    smaller = [_dot(lower, route[h].onehot).astype(BF16) for h in chunks]
    seg_off = [_dot(smaller[h], ones_b) for h in chunks]

    zrow = jnp.zeros((1, tm), F32)
    perm = []
    for h in chunks:
        rt = route[h]
        tcnt_ref[h] = tile_cnt[h]
        local = before_tile[h] + jnp.concatenate([seg_off[h]] * reps, axis=1)
        pos1 = jnp.sum(jnp.where(rt.is1, local, 0.0), axis=0, keepdims=True)
        pos2 = jnp.sum(jnp.where(rt.is2, local, 0.0), axis=0, keepdims=True)
        meta_ref[:, rows[h]] = jnp.concatenate(
            [rt.e1.astype(F32), rt.e2.astype(F32), rt.w1, rt.w2, pos1, pos2, zrow, zrow], axis=0)
        srow = lax.broadcasted_iota(jnp.int32, (2 * tm, tm), 0)
        perm.append(jnp.where((srow == pos1.astype(jnp.int32)) | (srow == pos2.astype(jnp.int32)), 1.0, 0.0)
                    .astype(BF16))
    xs = [_dot(perm[h], h1b[h]) for h in chunks]
    for h in chunks:
        base = h * 2 * tm * SUBLANES
        for c in range(xs[h].shape[1] // LANES):
            xls_ref[pl.ds(base + c, 2 * tm, stride=SUBLANES), :] = xs[h][:, c * LANES:(c + 1) * LANES]


class _Route(NamedTuple):
    e1: jax.Array
    e2: jax.Array
    w1: jax.Array
    w2: jax.Array
    is1: jax.Array
    is2: jax.Array
    onehot: jax.Array


def _route(lt, tm):
    r = [lt[k:k + 1, :] for k in range(N_GROUPS)]
    gmax = jnp.maximum(jnp.maximum(r[0], r[1]), jnp.maximum(r[2], r[3]))
    gidx = jnp.where(r[0] == gmax, 0, jnp.where(r[1] == gmax, 1, jnp.where(r[2] == gmax, 2, 3)))
    gsum = (jnp.exp(r[0] - gmax) + jnp.exp(r[1] - gmax)) + (jnp.exp(r[2] - gmax) + jnp.exp(r[3] - gmax))
    gprob = 1.0 / gsum
    epg = EXPERTS_PER_GROUP
    slabs = [lt[8 + g * epg:8 + (g + 1) * epg, :] for g in range(N_GROUPS)]
    el = jnp.where(gidx == 0, slabs[0], jnp.where(gidx == 1, slabs[1], jnp.where(gidx == 2, slabs[2], slabs[3])))
    r8 = lax.broadcasted_iota(jnp.int32, (epg, tm), 0)
    m1 = jnp.max(el, axis=0, keepdims=True)
    i1 = jnp.min(jnp.where(el == m1, r8, epg), axis=0, keepdims=True)
    el2 = jnp.where(r8 == i1, -jnp.inf, el)
    m2 = jnp.max(el2, axis=0, keepdims=True)
    i2 = jnp.min(jnp.where(el2 == m2, r8, epg), axis=0, keepdims=True)
    dlt = jnp.exp(m2 - m1)
    w1 = gprob / (1.0 + dlt)
    w2 = gprob * dlt / (1.0 + dlt)
    e1 = gidx * epg + i1
    e2 = gidx * epg + i2
    r32 = lax.broadcasted_iota(jnp.int32, (N_EXPERTS, tm), 0)
    is1 = r32 == e1
    is2 = r32 == e2
    onehot = jnp.where(is1 | is2, 1.0, 0.0).astype(BF16)
    return _Route(e1, e2, w1, w2, is1, is2, onehot)


def _mix_call(attn2, gate_a, gbb, x2, wa_b, wo_b, g1, b1, wrt_b, brt, alpha):
    t, d = x2.shape
    tile = PROJ_TILE
    kern = functools.partial(_mix_kernel, tile=tile, tm=SUB_TILE, alpha=alpha)
    const = lambda i: (0, 0)
    rowblk = lambda i: (i, 0)
    tm = tile
    return pl.pallas_call(
        kern,
        grid=(t // tm,),
        in_specs=[
            pl.BlockSpec((tm, SB_WIDTH), rowblk),
            pl.BlockSpec((tm, d), rowblk),
            pl.BlockSpec((tm, d), rowblk),
            pl.BlockSpec((tm, d), rowblk),
            pl.BlockSpec((SB_WIDTH, d), const),
            pl.BlockSpec((d, d), const),
            pl.BlockSpec((1, d), const),
            pl.BlockSpec((1, d), const),
            pl.BlockSpec((ROUTER_ROWS, d), const),
            pl.BlockSpec((ROUTER_ROWS, LANES), const),
        ],
        out_specs=[
            pl.BlockSpec((tm, d), rowblk),
            pl.BlockSpec((8, tm), lambda i: (0, i)),
            pl.BlockSpec((2 * tm * SUBLANES, LANES), rowblk),
            pl.BlockSpec((tile // SUB_TILE, N_EXPERTS, LANES), lambda i: (i, 0, 0)),
        ],
        out_shape=[
            jax.ShapeDtypeStruct((t, d), F32),
            jax.ShapeDtypeStruct((8, t), F32),
            jax.ShapeDtypeStruct((2 * t * SUBLANES, LANES), F32),
            jax.ShapeDtypeStruct((t // SUB_TILE, N_EXPERTS, LANES), F32),
        ],
        compiler_params=pltpu.CompilerParams(
            dimension_semantics=("arbitrary",), vmem_limit_bytes=VMEM_LIMIT),
        name="mix",
    )(attn2, gate_a, gbb, x2, wa_b, wo_b, g1, b1, wrt_b, brt)


def _row_copy(src_ref, src_row, dst_ref, dst_row, sem):
    return pltpu.make_async_copy(src_ref.at[pl.ds(src_row, 1), :], dst_ref.at[pl.ds(dst_row, 1), :], sem)


def _tok_rows(tok, n_tok):
    return pl.ds(pl.multiple_of(tok * SUBLANES, SUBLANES), pl.multiple_of(n_tok * SUBLANES, SUBLANES))


def _moe_kernel(be_ref, r0_ref, t0_ref, t1_ref, nv_ref, nu_ref, nxt_ref, cumt_ref, soff_ref,
                xls_ref, wg_ref, wu_ref, wd_ref, y_ref,
                xbuf_ref, zero_ref, sem, wgf_ref, wuf_ref, wdf_ref, wsem, wgb_ref, wub_ref, wdb_ref, state_ref,
                *, blk, sub, tile_rows):
    i = pl.program_id(0)
    n_used = nu_ref[0]
    e = be_ref[i]
    cur = i % 2
    ne = N_EXPERTS
    w_hbm = (wg_ref, wu_ref, wd_ref)
    wf_refs = (wgf_ref, wuf_ref, wdf_ref)

    def fetch_weights(expert, slot):
        for src, dst in zip(w_hbm, wf_refs):
            pltpu.make_async_copy(src.at[expert], dst.at[slot], wsem.at[slot]).start()

    def wait_weights(slot):
        for src, dst in zip(w_hbm, wf_refs):
            pltpu.make_async_copy(src.at[0], dst.at[slot], wsem.at[slot]).wait()

    def gather(b, slot):
        eb = be_ref[b]
        r0 = r0_ref[b]
        nv = nv_ref[b]

        def seg(t, c):
            first = cumt_ref[t * ne + eb]
            lo = jnp.maximum(first, r0)
            n = jnp.minimum(cumt_ref[(t + 1) * ne + eb], r0 + nv) - lo

            @pl.when(n > 0)
            def _():
                src = t * tile_rows + soff_ref[t * ne + eb] + (lo - first)
                pltpu.make_async_copy(xls_ref.at[_tok_rows(src, n), :],
                                      xbuf_ref.at[slot, _tok_rows(lo - r0, n), :], sem.at[slot]).start()
            return c

        lax.fori_loop(t0_ref[b], t1_ref[b] + 1, seg, 0)

        @pl.when(nv < blk)
        def _():
            pltpu.make_async_copy(zero_ref.at[_tok_rows(0, blk - nv), :],
                                  xbuf_ref.at[slot, _tok_rows(nv, blk - nv), :], sem.at[slot]).start()

    def drain(slot):
        pltpu.make_async_copy(xls_ref.at[pl.ds(0, blk * SUBLANES), :], xbuf_ref.at[slot], sem.at[slot]).wait()

    @pl.when(i == 0)
    def _():
        state_ref[0] = -1
        state_ref[1] = 0
        zero_ref[...] = jnp.zeros(zero_ref.shape, F32)

    @pl.when(jnp.logical_and(i == 0, n_used > 0))
    def _():
        gather(0, 0)
        fetch_weights(e, 0)

    @pl.when(i < n_used)
    def _():
        @pl.when(e != state_ref[0])
        def _():
            slot = state_ref[1]
            wait_weights(slot)
            wgb_ref[...] = wf_refs[0][slot].astype(BF16)
            wub_ref[...] = wf_refs[1][slot].astype(BF16)
            wdb_ref[...] = wf_refs[2][slot].astype(BF16)
            state_ref[0] = e
            state_ref[1] = 1 - slot

            @pl.when(nxt_ref[i] >= 0)
            def _():
                fetch_weights(nxt_ref[i], 1 - slot)

        gather(jnp.minimum(i + 1, n_used - 1), 1 - cur)
        drain(cur)
        xbs = [jnp.concatenate([xbuf_ref[cur, pl.ds(h * sub * SUBLANES + c, sub, stride=SUBLANES), :]
                                for c in range(SUBLANES)], axis=1).astype(BF16) for h in range(blk // sub)]
        gate = [_dot(xb, wgb_ref[...]) for xb in xbs]
        up = [_dot(xb, wub_ref[...]) for xb in xbs]
        hidden = [(g * jax.nn.sigmoid(g) * u).astype(BF16) for g, u in zip(gate, up)]
        for h, hid in enumerate(hidden):
            y = _dot(hid, wdb_ref[...])
            for c in range(SUBLANES):
                y_ref[pl.ds(h * sub * SUBLANES + c, sub, stride=SUBLANES), :] = y[:, c * LANES:(c + 1) * LANES]

    @pl.when(i == n_used - 1)
    def _():
        drain(1 - cur)

    @pl.when(i >= n_used)
    def _():
        y_ref[...] = jnp.zeros(y_ref.shape, F32)


def _moe_call(tables, xls, tile_rows, w_gate, w_up, w_down):
    n_blocks = tables[0].shape[0]
    blk = EXPERT_BLOCK
    _, d, de = w_gate.shape
    grid_spec = pltpu.PrefetchScalarGridSpec(
        num_scalar_prefetch=len(tables),
        grid=(n_blocks,),
        in_specs=[pl.BlockSpec(memory_space=pl.ANY)] * 4,
        out_specs=pl.BlockSpec((blk * SUBLANES, LANES), lambda i, *_: (i, 0)),
        scratch_shapes=[
            pltpu.VMEM((2, blk * SUBLANES, LANES), F32), pltpu.VMEM((blk * SUBLANES, LANES), F32),
            pltpu.SemaphoreType.DMA((2,)),
            pltpu.VMEM((2, d, de), F32), pltpu.VMEM((2, d, de), F32), pltpu.VMEM((2, de, d), F32),
            pltpu.SemaphoreType.DMA((2,)),
            pltpu.VMEM((d, de), BF16), pltpu.VMEM((d, de), BF16), pltpu.VMEM((de, d), BF16),
            pltpu.SMEM((2,), jnp.int32),
        ],
    )
    return pl.pallas_call(
        functools.partial(_moe_kernel, blk=blk, sub=SUB_TILE, tile_rows=tile_rows),
        grid_spec=grid_spec,
        out_shape=jax.ShapeDtypeStruct((n_blocks * blk * SUBLANES, LANES), F32),
        compiler_params=pltpu.CompilerParams(
            dimension_semantics=("arbitrary",), vmem_limit_bytes=VMEM_LIMIT),
        name="moe",
    )(*tables, xls, w_gate, w_up, w_down)


def _combine_kernel(pstart_ref, cumt_ref, soff_ref, tcnt_ref,
                    h1_ref, meta_ref, y_ref, g2_ref, b2_ref, o_ref, ybuf_ref, sem, *, tile, tm, alpha):
    i = pl.program_id(0)
    last = pl.num_programs(0) - 1
    cur = i % 2
    ne = N_EXPERTS
    chunks = range(tile // tm)
    seg_rows = 2 * tm * SUBLANES

    def gather(step, slot):
        for h in chunks:
            t = step * len(chunks) + h

            def seg(e, c, t=t, h=h):
                n = tcnt_ref[t * ne + e]

                @pl.when(n > 0)
                def _():
                    src = pstart_ref[e] + cumt_ref[t * ne + e]
                    dst = h * 2 * tm + soff_ref[t * ne + e]
                    pltpu.make_async_copy(y_ref.at[_tok_rows(src, n), :],
                                          ybuf_ref.at[slot, _tok_rows(dst, n), :], sem.at[slot]).start()
                return c

            lax.fori_loop(0, ne, seg, 0)

    def drain(slot):
        pltpu.make_async_copy(y_ref.at[pl.ds(0, len(chunks) * seg_rows), :], ybuf_ref.at[slot],
                              sem.at[slot]).wait()

    @pl.when(i == 0)
    def _():
        gather(0, 0)

    gather(jnp.minimum(i + 1, last), 1 - cur)
    drain(cur)
    srow = lax.broadcasted_iota(jnp.int32, (2 * tm, tm), 0)
    ys = [jnp.concatenate([ybuf_ref[cur, pl.ds(h * seg_rows + c, 2 * tm, stride=SUBLANES), :]
                           for c in range(SUBLANES)], axis=1).astype(BF16) for h in chunks]
    picked = {}
    for h in chunks:
        cols = slice(h * tm, (h + 1) * tm)
        for k in range(2):
            pos = meta_ref[4 + k:5 + k, cols].astype(jnp.int32)
            picked[h, k] = _dot_tn(jnp.where(srow == pos, 1.0, 0.0).astype(BF16), ys[h])
    for h in chunks:
        rows = slice(h * tm, (h + 1) * tm)
        rw = meta_ref[:, rows].T
        ffn = rw[:, 2:3] * picked[h, 0] + rw[:, 3:4] * picked[h, 1]
        o_ref[rows, :] = _layer_norm(alpha * h1_ref[rows, :] + ffn, g2_ref[...], b2_ref[...])

    @pl.when(i == last)
    def _():
        drain(1 - cur)


def _combine_call(tables, h1, meta, y_slots, g2, b2, alpha):
    t, d = h1.shape
    tm = PROJ_TILE
    kern = functools.partial(_combine_kernel, tile=tm, tm=SUB_TILE, alpha=alpha)
    const = lambda i, *_: (0, 0)
    grid_spec = pltpu.PrefetchScalarGridSpec(
        num_scalar_prefetch=len(tables),
        grid=(t // tm,),
        in_specs=[
            pl.BlockSpec((tm, d), lambda i, *_: (i, 0)),
            pl.BlockSpec((8, tm), lambda i, *_: (0, i)),
            pl.BlockSpec(memory_space=pl.ANY),
            pl.BlockSpec((1, d), const),
            pl.BlockSpec((1, d), const),
        ],
        out_specs=pl.BlockSpec((tm, d), lambda i, *_: (i, 0)),
        scratch_shapes=[pltpu.VMEM((2, 2 * tm * SUBLANES, LANES), F32), pltpu.SemaphoreType.DMA((2,))],
    )
    return pl.pallas_call(
        kern,
        grid_spec=grid_spec,
        out_shape=jax.ShapeDtypeStruct((t, d), F32),
        compiler_params=pltpu.CompilerParams(
            dimension_semantics=("arbitrary",), vmem_limit_bytes=VMEM_LIMIT),
        name="combine",
    )(*tables, h1, meta, y_slots, g2, b2)


def _layer(h, w_in, gate_bias, conv_w, w_branch_a, w_branch_b, w_out, ln1_g, ln1_b,
           w_router_g, b_router_g, w_router_e, b_router_e, w_gate, w_up, w_down, ln2_g, ln2_b, alpha):
    bsz, seq, d = h.shape
    t = bsz * seq
    x2 = h.reshape(t, d)

    qkv, gate_a, gbb = _proj_call(x2, w_in.astype(BF16), gate_bias.reshape(1, 2 * d), conv_w,
                                  w_branch_b.astype(BF16), seq)
    attn = _attn_call(qkv.reshape(bsz, seq, 3 * SB_WIDTH)).reshape(t, SB_WIDTH)

    wrt = jnp.zeros((ROUTER_ROWS, d), F32).at[0:N_GROUPS].set(w_router_g.T).at[8:].set(w_router_e.T)
    brt = jnp.zeros((ROUTER_ROWS,), F32).at[0:N_GROUPS].set(b_router_g).at[8:].set(b_router_e.reshape(-1))
    brt = jnp.broadcast_to(brt[:, None], (ROUTER_ROWS, LANES))
    h1, meta, xls, tcnt = _mix_call(attn, gate_a, gbb, x2, w_branch_a.astype(BF16), w_out.astype(BF16),
                                    ln1_g.reshape(1, d), ln1_b.reshape(1, d), wrt.astype(BF16), brt, alpha)

    blk = EXPERT_BLOCK
    tile_cnt = tcnt[:, :, 0].astype(jnp.int32)
    n_tiles = tile_cnt.shape[0]
    cum_incl = jnp.cumsum(tile_cnt, axis=0)
    cumt = jnp.concatenate([jnp.zeros((1, N_EXPERTS), jnp.int32), cum_incl], axis=0)
    counts = cumt[-1]
    soff = jnp.cumsum(tile_cnt, axis=1) - tile_cnt
    padded = (counts + blk - 1) // blk * blk
    pad_end = jnp.cumsum(padded)
    pad_start = pad_end - padded
    n_blocks = (2 * t) // blk + N_EXPERTS
    block_start = jnp.arange(n_blocks, dtype=jnp.int32) * blk
    block_expert = jnp.minimum(jnp.sum(pad_end[None, :] <= block_start[:, None], axis=1), N_EXPERTS - 1)
    n_used = (pad_end[-1] // blk).reshape(1)

    be = block_expert.astype(jnp.int32)
    mine = be[:, None] == jnp.arange(N_EXPERTS, dtype=jnp.int32)[None, :]

    def of_block(table):
        return jnp.sum(jnp.where(mine, table[..., None, :], 0), axis=-1)

    r0 = block_start - of_block(pad_start)
    nv = jnp.clip(of_block(counts) - r0, 0, blk)
    t0 = jnp.minimum(jnp.sum(of_block(cum_incl) <= r0[None, :], axis=0), n_tiles - 1)
    t1 = jnp.sum(of_block(cumt[:-1]) < (r0 + nv)[None, :], axis=0) - 1
    after = of_block(pad_end) // blk
    be_after = jnp.sum(jnp.where(after[:, None] == jnp.arange(n_blocks, dtype=jnp.int32)[None, :], be[None, :], 0),
                       axis=1)
    nxt = jnp.where(after < n_used[0], be_after, -1)
    tables = (be, r0, t0, t1, nv, n_used, nxt, cumt.reshape(-1), soff.reshape(-1))
    y_slots = _moe_call(tuple(a.astype(jnp.int32) for a in tables), xls, 2 * SUB_TILE, w_gate, w_up, w_down)
    ctables = (pad_start, cumt.reshape(-1), soff.reshape(-1), tile_cnt.reshape(-1))
    out = _combine_call(tuple(a.astype(jnp.int32) for a in ctables), h1, meta, y_slots,
                        ln2_g.reshape(1, d), ln2_b.reshape(1, d), alpha)
    return out.reshape(bsz, seq, d)


def kernel(x, w_in, gate_bias, conv_w, w_branch_a, w_branch_b, w_out, ln1_g, ln1_b, w_router_g, b_router_g,
           w_router_e, b_router_e, w_gate, w_up, w_down, ln2_g, ln2_b):
    depth = w_in.shape[0]
    alpha = (2.0 * depth) ** 0.25
    h = x
    for l in range(depth):
        h = _layer(h, w_in[l], gate_bias[l], conv_w[l], w_branch_a[l], w_branch_b[l], w_out[l], ln1_g[l],
                   ln1_b[l], w_router_g[l], b_router_g[l], w_router_e[l], b_router_e[l], w_gate[l], w_up[l],
                   w_down[l], ln2_g[l], ln2_b[l], alpha)
    return h
```

```python
import functools
from typing import NamedTuple

import jax
import jax.numpy as jnp
from jax import lax
from jax.experimental import pallas as pl
from jax.experimental.pallas import tpu as pltpu

F32 = jnp.float32
BF16 = jnp.bfloat16

SB_HEADS = 8
SB_HEAD_DIM = 64
SB_WIDTH = SB_HEADS * SB_HEAD_DIM
N_GROUPS = 4
EXPERTS_PER_GROUP = 8
N_EXPERTS = N_GROUPS * EXPERTS_PER_GROUP
LN_EPS = 1e-5

LANES = 128
SUBLANES = 8
HEAD_PAIR = LANES
PROJ_TILE = 1024
SUB_TILE = 256
ATTN_TILE = 512
ATTN_WINDOW = 128
ATTN_SUB = 64
ATTN_UNROLL = 2
ATTN_LAG = 2
EXPERT_BLOCK = 512
ROUTER_ROWS = 8 + N_EXPERTS
TAIL_CUTOFF = 110.0
MASKED_LOGIT = -1e30
LOG2E = 1.4426950408889634
VMEM_LIMIT = 56 * 1024 * 1024


def _dot(a, b):
    return jnp.dot(a, b, preferred_element_type=F32)


def _dot_nt(a, b):
    return lax.dot_general(a, b, (((1,), (1,)), ((), ())), preferred_element_type=F32)


def _dot_tn(a, b):
    return lax.dot_general(a, b, (((0,), (0,)), ((), ())), preferred_element_type=F32)


def _layer_norm(y, g, b):
    mu = jnp.mean(y, axis=-1, keepdims=True)
    d = y - mu
    var = jnp.mean(d * d, axis=-1, keepdims=True)
    return d * lax.rsqrt(var + LN_EPS) * g + b


def _proj_kernel(x_ref, w_ref, gbias_ref, convw_ref, wbb_ref, qkv_ref, ga_ref, gbb_ref, ubuf_ref,
                 *, tm, sub, seq, d_model):
    i = pl.program_id(0)
    cw = SB_WIDTH
    c0 = 3 * SB_WIDTH
    g0 = c0 + 3 * cw
    cwt = convw_ref[...]

    @pl.when(i == 0)
    def _():
        ubuf_ref[0:8, :] = jnp.zeros((8, cw), F32)

    chunks = range(tm // sub)
    rows = [slice(h * sub, (h + 1) * sub) for h in chunks]
    xb = [x_ref[rows[h], :].astype(BF16) for h in chunks]
    conv_in = [_dot(xb[h], w_ref[:, c0:g0]) for h in chunks]
    qkv = [_dot(xb[h], w_ref[:, 0:c0]) for h in chunks]
    gated = []
    for h in chunks:
        cb = conv_in[h][:, 0:cw]
        u = conv_in[h][:, cw:2 * cw] * conv_in[h][:, 2 * cw:3 * cw]
        ubuf_ref[0:8, :] = jnp.where((i * tm + h * sub) % seq == 0, 0.0, ubuf_ref[0:8, :])
        ubuf_ref[8:sub + 8, :] = u
        y = (cwt[0:1, :] * ubuf_ref[pl.ds(6, sub), :] + cwt[1:2, :] * ubuf_ref[pl.ds(7, sub), :]
             + cwt[2:3, :] * u)
        ubuf_ref[0:8, :] = u[sub - 8:sub, :]
        gated.append((cb * y).astype(BF16))
        qkv_ref[rows[h], 0:SB_WIDTH] = (qkv[h][:, 0:SB_WIDTH] * (SB_HEAD_DIM ** -0.5)).astype(BF16)
        qkv_ref[rows[h], SB_WIDTH:c0] = qkv[h][:, SB_WIDTH:c0].astype(BF16)
    gate_logits = [_dot(xb[h], w_ref[:, g0:g0 + 2 * d_model]) for h in chunks]
    branch_b = [_dot(gated[h], wbb_ref[...]) for h in chunks]
    for h in chunks:
        gates = jax.nn.sigmoid(gate_logits[h] + gbias_ref[...])
        ga_ref[rows[h], :] = gates[:, 0:d_model].astype(BF16)
        gbb_ref[rows[h], :] = (gates[:, d_model:2 * d_model] * branch_b[h]).astype(BF16)


def _proj_call(x2, w_in_b, gbias, conv_w, wbb_b, seq):
    t, d = x2.shape
    pw = w_in_b.shape[1]
    tm = PROJ_TILE
    sub = SUB_TILE
    cw = conv_w.shape[1]
    assert seq % sub == 0 and t % tm == 0
    kern = functools.partial(_proj_kernel, tm=tm, sub=sub, seq=seq, d_model=d)
    const = lambda i: (0, 0)
    return pl.pallas_call(
        kern,
        grid=(t // tm,),
        in_specs=[
            pl.BlockSpec((tm, d), lambda i: (i, 0)),
            pl.BlockSpec((d, pw), const),
            pl.BlockSpec((1, 2 * d), const),
            pl.BlockSpec((3, cw), const),
            pl.BlockSpec((cw, d), const),
        ],
        out_specs=[
            pl.BlockSpec((tm, 3 * SB_WIDTH), lambda i: (i, 0)),
            pl.BlockSpec((tm, d), lambda i: (i, 0)),
            pl.BlockSpec((tm, d), lambda i: (i, 0)),
        ],
        out_shape=[
            jax.ShapeDtypeStruct((t, 3 * SB_WIDTH), BF16),
            jax.ShapeDtypeStruct((t, d), BF16),
            jax.ShapeDtypeStruct((t, d), BF16),
        ],
        scratch_shapes=[pltpu.VMEM((sub + 8, cw), F32)],
        compiler_params=pltpu.CompilerParams(
            dimension_semantics=("arbitrary",), vmem_limit_bytes=VMEM_LIMIT),
        name="proj",
    )(x2, w_in_b, gbias, conv_w, wbb_b)


def _attn_kernel(q_ref, k_ref, v_ref, o_ref, acc_ref, c_ref, *, tb, ts, tq, unroll, lag):
    qi = pl.program_id(1)
    assert ts == LANES
    subs = range(tb // tq)
    lane_q = lax.broadcasted_iota(jnp.int32, (tq, HEAD_PAIR), 1)
    row = lax.broadcasted_iota(jnp.int32, (ts, ts), 0)
    col = lax.broadcasted_iota(jnp.int32, (ts, ts), 1)
    later = (row > col).astype(BF16)
    later2 = jnp.concatenate([later, later], axis=0)
    later2 = jnp.concatenate([later2, jnp.ones((2 * ts, LANES), BF16)], axis=1)
    kcol = lax.broadcasted_iota(jnp.int32, (tq, ts), 1)
    ahead = kcol - lax.broadcasted_iota(jnp.int32, (tq, ts), 0)
    halves = HEAD_PAIR // SB_HEAD_DIM
    in_head_q = [(lane_q >= h * SB_HEAD_DIM) & (lane_q < (h + 1) * SB_HEAD_DIM) for h in range(halves)]
    zero_q = jnp.zeros((tq, HEAD_PAIR), BF16)

    acc_ref[...] = jnp.zeros(acc_ref.shape, F32)
    c_ref[...] = jnp.zeros(c_ref.shape, F32)

    pairs = range(SB_WIDTH // HEAD_PAIR)
    cols = [slice(p * HEAD_PAIR, (p + 1) * HEAD_PAIR) for p in pairs]
    rows = [slice(s * tq, (s + 1) * tq) for s in subs]

    def sweep(w0):
        tiles = [(s, k) for s in subs for k in range(unroll)]
        start, bias = {}, {}
        for s, k in tiles:
            p0 = qi * tb + s * tq
            first = p0 + tq - ts * (w0 + k + 1)
            st = jnp.maximum(first, 0)
            start[s, k] = pl.multiple_of(st, tq)
            ok = (ahead < p0 - st) & (kcol < first + ts - st)
            bias[s, k] = jnp.where(ok, 0.0, MASKED_LOGIT)
        z, sp, split, inner, wb = {}, {}, {}, {}, {}
        groups = [(p, s) for p in pairs for s in subs]
        windows = range(unroll)

        def logits(g):
            p, s = groups[g]
            q = q_ref[rows[s], cols[p]]
            q2 = jnp.concatenate([jnp.where(in_head_q[h], q, zero_q) for h in range(halves)], axis=0)
            for k in windows:
                b2 = jnp.concatenate([bias[s, k]] * halves, axis=0)
                z[g, k] = _dot_nt(q2, k_ref[pl.ds(start[s, k], ts), cols[p]]) + b2

        def stay(g):
            for k in windows:
                zz = z[g, k]
                a = jnp.maximum(zz, 0.0) + jnp.log(1.0 + jnp.exp2(jnp.abs(zz) * (-LOG2E)))
                hi = a.astype(BF16)
                sp[g, k] = a
                split[g, k] = jnp.concatenate([hi, (a - hi.astype(F32)).astype(BF16)], axis=1)

        def tails(g):
            both = _dot(jnp.concatenate([split[g, k] for k in windows], axis=0), later2)
            for k in windows:
                inner[g, k] = both[k * halves * tq:(k + 1) * halves * tq, :]

        def weights(g):
            cvec = c_ref[g]
            for k in windows:
                c = (g, k)
                wb[c] = jnp.exp(z[c] - sp[c] - inner[c][:, 0:ts] - cvec).astype(BF16)
                cvec = cvec + inner[c][:, ts:ts + LANES]
            c_ref[g] = cvec

        def values(g):
            p, s = groups[g]
            out = None
            for k in windows:
                res = _dot(wb[g, k], v_ref[pl.ds(start[s, k], ts), cols[p]])
                pv = jnp.where(in_head_q[0], res[0:tq, :], res[tq:2 * tq, :])
                out = pv if out is None else out + pv
            acc_ref[rows[s], cols[p]] += out

        stages = (logits, stay, tails, weights, values)
        for step in range(len(groups) + lag * (len(stages) - 1)):
            for s in reversed(range(len(stages))):
                g = step - lag * s
                if 0 <= g < len(groups):
                    stages[s](g)

    def cond(carry):
        w0, cmin = carry
        return jnp.logical_and(w0 * ts < (qi + 1) * tb, cmin < TAIL_CUTOFF)

    def body(carry):
        w0, _ = carry
        sweep(w0)
        return w0 + unroll, jnp.min(c_ref[...])

    lax.while_loop(cond, body, (jnp.int32(0), jnp.float32(0.0)))
    o_ref[...] = acc_ref[...].astype(BF16)


def _attn_call(qkv3):
    b, s, _ = qkv3.shape
    tb = ATTN_TILE
    tq = ATTN_SUB
    n_groups = (SB_WIDTH // HEAD_PAIR) * (tb // tq)
    kern = functools.partial(_attn_kernel, tb=tb, ts=ATTN_WINDOW, tq=tq, unroll=ATTN_UNROLL, lag=ATTN_LAG)
    return pl.pallas_call(
        kern,
        grid=(b, s // tb),
        in_specs=[
            pl.BlockSpec((None, tb, SB_WIDTH), lambda bi, qi: (bi, qi, 0)),
            pl.BlockSpec((None, s, SB_WIDTH), lambda bi, qi: (bi, 0, 1)),
            pl.BlockSpec((None, s, SB_WIDTH), lambda bi, qi: (bi, 0, 2)),
        ],
        out_specs=pl.BlockSpec((None, tb, SB_WIDTH), lambda bi, qi: (bi, qi, 0)),
        out_shape=jax.ShapeDtypeStruct((b, s, SB_WIDTH), BF16),
        scratch_shapes=[pltpu.VMEM((tb, SB_WIDTH), F32),
                        pltpu.VMEM((n_groups, (HEAD_PAIR // SB_HEAD_DIM) * tq, LANES), F32)],
        compiler_params=pltpu.CompilerParams(
            dimension_semantics=("arbitrary", "arbitrary"), vmem_limit_bytes=VMEM_LIMIT),
        name="attn",
    )(qkv3, qkv3, qkv3)


def _mix_kernel(attn_ref, ga_ref, gbb_ref, x_ref, wa_ref, wo_ref, g1_ref, b1_ref, wrt_ref, brt_ref,
                h1_ref, meta_ref, xls_ref, tcnt_ref, *, tile, tm, alpha):
    chunks = range(tile // tm)
    reps = tm // LANES
    rows = [slice(h * tm, (h + 1) * tm) for h in chunks]
    branch_a = [_dot(attn_ref[rows[h], :], wa_ref[...]) for h in chunks]
    mixed = [_dot((ga_ref[rows[h], :] * branch_a[h] + gbb_ref[rows[h], :]).astype(BF16), wo_ref[...])
             for h in chunks]
    h1b = []
    for h in chunks:
        h1 = _layer_norm(alpha * x_ref[rows[h], :] + mixed[h], g1_ref[...], b1_ref[...])
        h1_ref[rows[h], :] = h1
        h1b.append(h1.astype(BF16))
    bias = jnp.concatenate([brt_ref[...]] * reps, axis=1)
    logits = [_dot_nt(wrt_ref[...], h1b[h]) + bias for h in chunks]
    route = [_route(logits[h], tm) for h in chunks]

    trow = lax.broadcasted_iota(jnp.int32, (tm, tm), 0)
    tcol = lax.broadcasted_iota(jnp.int32, (tm, tm), 1)
    earlier = (trow < tcol).astype(BF16)
    e32r = lax.broadcasted_iota(jnp.int32, (N_EXPERTS, N_EXPERTS), 0)
    e32c = lax.broadcasted_iota(jnp.int32, (N_EXPERTS, N_EXPERTS), 1)
    lower = (e32c < e32r).astype(BF16)
    ones_b = jnp.ones((tm, LANES), BF16)
    before_tile = [_dot(route[h].onehot, earlier) for h in chunks]
    tile_cnt = [_dot(route[h].onehot, ones_b) for h in chunks]
    smaller = [_dot(lower, route[h].onehot).astype(BF16) for h in chunks]
    seg_off = [_dot(smaller[h], ones_b) for h in chunks]

    zrow = jnp.zeros((1, tm), F32)
    perm = []
    for h in chunks:
        rt = route[h]
        tcnt_ref[h] = tile_cnt[h]
        local = before_tile[h] + jnp.concatenate([seg_off[h]] * reps, axis=1)
        pos1 = jnp.sum(jnp.where(rt.is1, local, 0.0), axis=0, keepdims=True)
        pos2 = jnp.sum(jnp.where(rt.is2, local, 0.0), axis=0, keepdims=True)
        meta_ref[:, rows[h]] = jnp.concatenate(
            [rt.e1.astype(F32), rt.e2.astype(F32), rt.w1, rt.w2, pos1, pos2, zrow, zrow], axis=0)
        srow = lax.broadcasted_iota(jnp.int32, (2 * tm, tm), 0)
        perm.append(jnp.where((srow == pos1.astype(jnp.int32)) | (srow == pos2.astype(jnp.int32)), 1.0, 0.0)
                    .astype(BF16))
    xs = [_dot(perm[h], h1b[h]) for h in chunks]
    for h in chunks:
        base = h * 2 * tm * SUBLANES
        for c in range(xs[h].shape[1] // LANES):
            xls_ref[pl.ds(base + c, 2 * tm, stride=SUBLANES), :] = xs[h][:, c * LANES:(c + 1) * LANES]


class _Route(NamedTuple):
    e1: jax.Array
    e2: jax.Array
    w1: jax.Array
    w2: jax.Array
    is1: jax.Array
    is2: jax.Array
    onehot: jax.Array


def _route(lt, tm):
    r = [lt[k:k + 1, :] for k in range(N_GROUPS)]
    gmax = jnp.maximum(jnp.maximum(r[0], r[1]), jnp.maximum(r[2], r[3]))
    gidx = jnp.where(r[0] == gmax, 0, jnp.where(r[1] == gmax, 1, jnp.where(r[2] == gmax, 2, 3)))
    gsum = (jnp.exp(r[0] - gmax) + jnp.exp(r[1] - gmax)) + (jnp.exp(r[2] - gmax) + jnp.exp(r[3] - gmax))
    gprob = 1.0 / gsum
    epg = EXPERTS_PER_GROUP
    slabs = [lt[8 + g * epg:8 + (g + 1) * epg, :] for g in range(N_GROUPS)]
    el = jnp.where(gidx == 0, slabs[0], jnp.where(gidx == 1, slabs[1], jnp.where(gidx == 2, slabs[2], slabs[3])))
    r8 = lax.broadcasted_iota(jnp.int32, (epg, tm), 0)
    m1 = jnp.max(el, axis=0, keepdims=True)
    i1 = jnp.min(jnp.where(el == m1, r8, epg), axis=0, keepdims=True)
    el2 = jnp.where(r8 == i1, -jnp.inf, el)
    m2 = jnp.max(el2, axis=0, keepdims=True)
    i2 = jnp.min(jnp.where(el2 == m2, r8, epg), axis=0, keepdims=True)
    dlt = jnp.exp(m2 - m1)
    w1 = gprob / (1.0 + dlt)
    w2 = gprob * dlt / (1.0 + dlt)
    e1 = gidx * epg + i1
    e2 = gidx * epg + i2
    r32 = lax.broadcasted_iota(jnp.int32, (N_EXPERTS, tm), 0)
    is1 = r32 == e1
    is2 = r32 == e2
    onehot = jnp.where(is1 | is2, 1.0, 0.0).astype(BF16)
    return _Route(e1, e2, w1, w2, is1, is2, onehot)


def _mix_call(attn2, gate_a, gbb, x2, wa_b, wo_b, g1, b1, wrt_b, brt, alpha):
    t, d = x2.shape
    tile = PROJ_TILE
    kern = functools.partial(_mix_kernel, tile=tile, tm=SUB_TILE, alpha=alpha)
    const = lambda i: (0, 0)
    rowblk = lambda i: (i, 0)
    tm = tile
    return pl.pallas_call(
        kern,
        grid=(t // tm,),
        in_specs=[
            pl.BlockSpec((tm, SB_WIDTH), rowblk),
            pl.BlockSpec((tm, d), rowblk),
            pl.BlockSpec((tm, d), rowblk),
            pl.BlockSpec((tm, d), rowblk),
            pl.BlockSpec((SB_WIDTH, d), const),
            pl.BlockSpec((d, d), const),
            pl.BlockSpec((1, d), const),
            pl.BlockSpec((1, d), const),
            pl.BlockSpec((ROUTER_ROWS, d), const),
            pl.BlockSpec((ROUTER_ROWS, LANES), const),
        ],
        out_specs=[
            pl.BlockSpec((tm, d), rowblk),
            pl.BlockSpec((8, tm), lambda i: (0, i)),
            pl.BlockSpec((2 * tm * SUBLANES, LANES), rowblk),
            pl.BlockSpec((tile // SUB_TILE, N_EXPERTS, LANES), lambda i: (i, 0, 0)),
        ],
        out_shape=[
            jax.ShapeDtypeStruct((t, d), F32),
            jax.ShapeDtypeStruct((8, t), F32),
            jax.ShapeDtypeStruct((2 * t * SUBLANES, LANES), F32),
            jax.ShapeDtypeStruct((t // SUB_TILE, N_EXPERTS, LANES), F32),
        ],
        compiler_params=pltpu.CompilerParams(
            dimension_semantics=("arbitrary",), vmem_limit_bytes=VMEM_LIMIT),
        name="mix",
    )(attn2, gate_a, gbb, x2, wa_b, wo_b, g1, b1, wrt_b, brt)


def _row_copy(src_ref, src_row, dst_ref, dst_row, sem):
    return pltpu.make_async_copy(src_ref.at[pl.ds(src_row, 1), :], dst_ref.at[pl.ds(dst_row, 1), :], sem)


def _tok_rows(tok, n_tok):
    return pl.ds(pl.multiple_of(tok * SUBLANES, SUBLANES), pl.multiple_of(n_tok * SUBLANES, SUBLANES))


def _moe_kernel(be_ref, r0_ref, t0_ref, t1_ref, nv_ref, nu_ref, nxt_ref, cumt_ref, soff_ref,
                xls_ref, wg_ref, wu_ref, wd_ref, y_ref,
                xbuf_ref, zero_ref, sem, wgf_ref, wuf_ref, wdf_ref, wsem, wgb_ref, wub_ref, wdb_ref, state_ref,
                *, blk, sub, tile_rows):
    i = pl.program_id(0)
    n_used = nu_ref[0]
    e = be_ref[i]
    cur = i % 2
    ne = N_EXPERTS
    w_hbm = (wg_ref, wu_ref, wd_ref)
    wf_refs = (wgf_ref, wuf_ref, wdf_ref)

    def fetch_weights(expert, slot):
        for src, dst in zip(w_hbm, wf_refs):
            pltpu.make_async_copy(src.at[expert], dst.at[slot], wsem.at[slot]).start()

    def wait_weights(slot):
        for src, dst in zip(w_hbm, wf_refs):
            pltpu.make_async_copy(src.at[0], dst.at[slot], wsem.at[slot]).wait()

    def gather(b, slot):
        eb = be_ref[b]
        r0 = r0_ref[b]
        nv = nv_ref[b]

        def seg(t, c):
            first = cumt_ref[t * ne + eb]
            lo = jnp.maximum(first, r0)
            n = jnp.minimum(cumt_ref[(t + 1) * ne + eb], r0 + nv) - lo

            @pl.when(n > 0)
            def _():
                src = t * tile_rows + soff_ref[t * ne + eb] + (lo - first)
                pltpu.make_async_copy(xls_ref.at[_tok_rows(src, n), :],
                                      xbuf_ref.at[slot, _tok_rows(lo - r0, n), :], sem.at[slot]).start()
            return c

        lax.fori_loop(t0_ref[b], t1_ref[b] + 1, seg, 0)

        @pl.when(nv < blk)
        def _():
            pltpu.make_async_copy(zero_ref.at[_tok_rows(0, blk - nv), :],
                                  xbuf_ref.at[slot, _tok_rows(nv, blk - nv), :], sem.at[slot]).start()

    def drain(slot):
        pltpu.make_async_copy(xls_ref.at[pl.ds(0, blk * SUBLANES), :], xbuf_ref.at[slot], sem.at[slot]).wait()

    @pl.when(i == 0)
    def _():
        state_ref[0] = -1
        state_ref[1] = 0
        zero_ref[...] = jnp.zeros(zero_ref.shape, F32)

    @pl.when(jnp.logical_and(i == 0, n_used > 0))
    def _():
        gather(0, 0)
        fetch_weights(e, 0)

    @pl.when(i < n_used)
    def _():
        @pl.when(e != state_ref[0])
        def _():
            slot = state_ref[1]
            wait_weights(slot)
            wgb_ref[...] = wf_refs[0][slot].astype(BF16)
            wub_ref[...] = wf_refs[1][slot].astype(BF16)
            wdb_ref[...] = wf_refs[2][slot].astype(BF16)
            state_ref[0] = e
            state_ref[1] = 1 - slot

            @pl.when(nxt_ref[i] >= 0)
            def _():
                fetch_weights(nxt_ref[i], 1 - slot)

        gather(jnp.minimum(i + 1, n_used - 1), 1 - cur)
        drain(cur)

        def ffn(chunks):
            xbs = [jnp.concatenate([xbuf_ref[cur, pl.ds(h * sub * SUBLANES + c, sub, stride=SUBLANES), :]
                                    for c in range(SUBLANES)], axis=1).astype(BF16) for h in chunks]
            gate = [_dot(xb, wgb_ref[...]) for xb in xbs]
            up = [_dot(xb, wub_ref[...]) for xb in xbs]
            hidden = [(g * jax.nn.sigmoid(g) * u).astype(BF16) for g, u in zip(gate, up)]
            for h, hid in zip(chunks, hidden):
                y = _dot(hid, wdb_ref[...])
                for c in range(SUBLANES):
                    y_ref[pl.ds(h * sub * SUBLANES + c, sub, stride=SUBLANES), :] = y[:, c * LANES:(c + 1) * LANES]

        all_chunks = list(range(blk // sub))

        @pl.when(nv_ref[i] > sub)
        def _():
            ffn(all_chunks)

        @pl.when(nv_ref[i] <= sub)
        def _():
            ffn(all_chunks[:1])
            y_ref[sub * SUBLANES:blk * SUBLANES, :] = jnp.zeros(((blk - sub) * SUBLANES, LANES), F32)

    @pl.when(i == n_used - 1)
    def _():
        drain(1 - cur)

    @pl.when(i >= n_used)
    def _():
        y_ref[...] = jnp.zeros(y_ref.shape, F32)


def _moe_call(tables, xls, tile_rows, w_gate, w_up, w_down):
    n_blocks = tables[0].shape[0]
    blk = EXPERT_BLOCK
    _, d, de = w_gate.shape
    grid_spec = pltpu.PrefetchScalarGridSpec(
        num_scalar_prefetch=len(tables),
        grid=(n_blocks,),
        in_specs=[pl.BlockSpec(memory_space=pl.ANY)] * 4,
        out_specs=pl.BlockSpec((blk * SUBLANES, LANES), lambda i, *_: (i, 0)),
        scratch_shapes=[
            pltpu.VMEM((2, blk * SUBLANES, LANES), F32), pltpu.VMEM((blk * SUBLANES, LANES), F32),
            pltpu.SemaphoreType.DMA((2,)),
            pltpu.VMEM((2, d, de), F32), pltpu.VMEM((2, d, de), F32), pltpu.VMEM((2, de, d), F32),
            pltpu.SemaphoreType.DMA((2,)),
            pltpu.VMEM((d, de), BF16), pltpu.VMEM((d, de), BF16), pltpu.VMEM((de, d), BF16),
            pltpu.SMEM((2,), jnp.int32),
        ],
    )
    return pl.pallas_call(
        functools.partial(_moe_kernel, blk=blk, sub=SUB_TILE, tile_rows=tile_rows),
        grid_spec=grid_spec,
        out_shape=jax.ShapeDtypeStruct((n_blocks * blk * SUBLANES, LANES), F32),
        compiler_params=pltpu.CompilerParams(
            dimension_semantics=("arbitrary",), vmem_limit_bytes=VMEM_LIMIT),
        name="moe",
    )(*tables, xls, w_gate, w_up, w_down)


def _combine_kernel(pstart_ref, cumt_ref, soff_ref, tcnt_ref,
                    h1_ref, meta_ref, y_ref, g2_ref, b2_ref, o_ref, ybuf_ref, sem, *, tile, tm, alpha):
    i = pl.program_id(0)
    last = pl.num_programs(0) - 1
    cur = i % 2
    ne = N_EXPERTS
    chunks = range(tile // tm)
    seg_rows = 2 * tm * SUBLANES

    def gather(step, slot):
        for h in chunks:
            t = step * len(chunks) + h

            def seg(e, c, t=t, h=h):
                n = tcnt_ref[t * ne + e]

                @pl.when(n > 0)
                def _():
                    src = pstart_ref[e] + cumt_ref[t * ne + e]
                    dst = h * 2 * tm + soff_ref[t * ne + e]
                    pltpu.make_async_copy(y_ref.at[_tok_rows(src, n), :],
                                          ybuf_ref.at[slot, _tok_rows(dst, n), :], sem.at[slot]).start()
                return c

            lax.fori_loop(0, ne, seg, 0)

    def drain(slot):
        pltpu.make_async_copy(y_ref.at[pl.ds(0, len(chunks) * seg_rows), :], ybuf_ref.at[slot],
                              sem.at[slot]).wait()

    @pl.when(i == 0)
    def _():
        gather(0, 0)

    gather(jnp.minimum(i + 1, last), 1 - cur)
    drain(cur)
    srow = lax.broadcasted_iota(jnp.int32, (2 * tm, tm), 0)
    ys = [jnp.concatenate([ybuf_ref[cur, pl.ds(h * seg_rows + c, 2 * tm, stride=SUBLANES), :]
                           for c in range(SUBLANES)], axis=1).astype(BF16) for h in chunks]
    picked = {}
    for h in chunks:
        cols = slice(h * tm, (h + 1) * tm)
        for k in range(2):
            pos = meta_ref[4 + k:5 + k, cols].astype(jnp.int32)
            picked[h, k] = _dot_tn(jnp.where(srow == pos, 1.0, 0.0).astype(BF16), ys[h])
    for h in chunks:
        rows = slice(h * tm, (h + 1) * tm)
        rw = meta_ref[:, rows].T
        ffn = rw[:, 2:3] * picked[h, 0] + rw[:, 3:4] * picked[h, 1]
        o_ref[rows, :] = _layer_norm(alpha * h1_ref[rows, :] + ffn, g2_ref[...], b2_ref[...])

    @pl.when(i == last)
    def _():
        drain(1 - cur)


def _combine_call(tables, h1, meta, y_slots, g2, b2, alpha):
    t, d = h1.shape
    tm = PROJ_TILE
    kern = functools.partial(_combine_kernel, tile=tm, tm=SUB_TILE, alpha=alpha)
    const = lambda i, *_: (0, 0)
    grid_spec = pltpu.PrefetchScalarGridSpec(
        num_scalar_prefetch=len(tables),
        grid=(t // tm,),
        in_specs=[
            pl.BlockSpec((tm, d), lambda i, *_: (i, 0)),
            pl.BlockSpec((8, tm), lambda i, *_: (0, i)),
            pl.BlockSpec(memory_space=pl.ANY),
            pl.BlockSpec((1, d), const),
            pl.BlockSpec((1, d), const),
        ],
        out_specs=pl.BlockSpec((tm, d), lambda i, *_: (i, 0)),
        scratch_shapes=[pltpu.VMEM((2, 2 * tm * SUBLANES, LANES), F32), pltpu.SemaphoreType.DMA((2,))],
    )
    return pl.pallas_call(
        kern,
        grid_spec=grid_spec,
        out_shape=jax.ShapeDtypeStruct((t, d), F32),
        compiler_params=pltpu.CompilerParams(
            dimension_semantics=("arbitrary",), vmem_limit_bytes=VMEM_LIMIT),
        name="combine",
    )(*tables, h1, meta, y_slots, g2, b2)


def _layer(h, w_in, gate_bias, conv_w, w_branch_a, w_branch_b, w_out, ln1_g, ln1_b,
           w_router_g, b_router_g, w_router_e, b_router_e, w_gate, w_up, w_down, ln2_g, ln2_b, alpha):
    bsz, seq, d = h.shape
    t = bsz * seq
    x2 = h.reshape(t, d)

    qkv, gate_a, gbb = _proj_call(x2, w_in.astype(BF16), gate_bias.reshape(1, 2 * d), conv_w,
                                  w_branch_b.astype(BF16), seq)
    attn = _attn_call(qkv.reshape(bsz, seq, 3 * SB_WIDTH)).reshape(t, SB_WIDTH)

    wrt = jnp.zeros((ROUTER_ROWS, d), F32).at[0:N_GROUPS].set(w_router_g.T).at[8:].set(w_router_e.T)
    brt = jnp.zeros((ROUTER_ROWS,), F32).at[0:N_GROUPS].set(b_router_g).at[8:].set(b_router_e.reshape(-1))
    brt = jnp.broadcast_to(brt[:, None], (ROUTER_ROWS, LANES))
    h1, meta, xls, tcnt = _mix_call(attn, gate_a, gbb, x2, w_branch_a.astype(BF16), w_out.astype(BF16),
                                    ln1_g.reshape(1, d), ln1_b.reshape(1, d), wrt.astype(BF16), brt, alpha)

    blk = EXPERT_BLOCK
    tile_cnt = tcnt[:, :, 0].astype(jnp.int32)
    n_tiles = tile_cnt.shape[0]
    cum_incl = jnp.cumsum(tile_cnt, axis=0)
    cumt = jnp.concatenate([jnp.zeros((1, N_EXPERTS), jnp.int32), cum_incl], axis=0)
    counts = cumt[-1]
    soff = jnp.cumsum(tile_cnt, axis=1) - tile_cnt
    padded = (counts + blk - 1) // blk * blk
    pad_end = jnp.cumsum(padded)
    pad_start = pad_end - padded
    n_blocks = (2 * t) // blk + N_EXPERTS
    block_start = jnp.arange(n_blocks, dtype=jnp.int32) * blk
    block_expert = jnp.minimum(jnp.sum(pad_end[None, :] <= block_start[:, None], axis=1), N_EXPERTS - 1)
    n_used = (pad_end[-1] // blk).reshape(1)

    be = block_expert.astype(jnp.int32)
    mine = be[:, None] == jnp.arange(N_EXPERTS, dtype=jnp.int32)[None, :]

    def of_block(table):
        return jnp.sum(jnp.where(mine, table[..., None, :], 0), axis=-1)

    r0 = block_start - of_block(pad_start)
    nv = jnp.clip(of_block(counts) - r0, 0, blk)
    t0 = jnp.minimum(jnp.sum(of_block(cum_incl) <= r0[None, :], axis=0), n_tiles - 1)
    t1 = jnp.sum(of_block(cumt[:-1]) < (r0 + nv)[None, :], axis=0) - 1
    after = of_block(pad_end) // blk
    be_after = jnp.sum(jnp.where(after[:, None] == jnp.arange(n_blocks, dtype=jnp.int32)[None, :], be[None, :], 0),
                       axis=1)
    nxt = jnp.where(after < n_used[0], be_after, -1)
    tables = (be, r0, t0, t1, nv, n_used, nxt, cumt.reshape(-1), soff.reshape(-1))
    y_slots = _moe_call(tuple(a.astype(jnp.int32) for a in tables), xls, 2 * SUB_TILE, w_gate, w_up, w_down)
    ctables = (pad_start, cumt.reshape(-1), soff.reshape(-1), tile_cnt.reshape(-1))
    out = _combine_call(tuple(a.astype(jnp.int32) for a in ctables), h1, meta, y_slots,
                        ln2_g.reshape(1, d), ln2_b.reshape(1, d), alpha)
    return out.reshape(bsz, seq, d)


def kernel(x, w_in, gate_bias, conv_w, w_branch_a, w_branch_b, w_out, ln1_g, ln1_b, w_router_g, b_router_g,
           w_router_e, b_router_e, w_gate, w_up, w_down, ln2_g, ln2_b):
    depth = w_in.shape[0]
    alpha = (2.0 * depth) ** 0.25
    h = x
    for l in range(depth):
        h = _layer(h, w_in[l], gate_bias[l], conv_w[l], w_branch_a[l], w_branch_b[l], w_out[l], ln1_g[l],
                   ln1_b[l], w_router_g[l], b_router_g[l], w_router_e[l], b_router_e[l], w_gate[l], w_up[l],
                   w_down[l], ln2_g[l], ln2_b[l], alpha)
    return h
```

```python
import functools
from typing import NamedTuple

import jax
import jax.numpy as jnp
from jax import lax
from jax.experimental import pallas as pl
from jax.experimental.pallas import tpu as pltpu

F32 = jnp.float32
BF16 = jnp.bfloat16

SB_HEADS = 8
SB_HEAD_DIM = 64
SB_WIDTH = SB_HEADS * SB_HEAD_DIM
N_GROUPS = 4
EXPERTS_PER_GROUP = 8
N_EXPERTS = N_GROUPS * EXPERTS_PER_GROUP
LN_EPS = 1e-5

LANES = 128
SUBLANES = 8
HEAD_PAIR = LANES
PROJ_TILE = 1024
SUB_TILE = 256
ATTN_TILE = 512
ATTN_WINDOW = 128
ATTN_SUB = 64
ATTN_UNROLL = 2
ATTN_LAG = 2
EXPERT_BLOCK = 1024
ROUTER_ROWS = 8 + N_EXPERTS
TAIL_CUTOFF = 110.0
MASKED_LOGIT = -1e30
LOG2E = 1.4426950408889634
VMEM_LIMIT = 56 * 1024 * 1024


def _dot(a, b):
    return jnp.dot(a, b, preferred_element_type=F32)


def _dot_nt(a, b):
    return lax.dot_general(a, b, (((1,), (1,)), ((), ())), preferred_element_type=F32)


def _dot_tn(a, b):
    return lax.dot_general(a, b, (((0,), (0,)), ((), ())), preferred_element_type=F32)


def _layer_norm(y, g, b):
    mu = jnp.mean(y, axis=-1, keepdims=True)
    d = y - mu
    var = jnp.mean(d * d, axis=-1, keepdims=True)
    return d * lax.rsqrt(var + LN_EPS) * g + b


def _proj_kernel(x_ref, w_ref, gbias_ref, convw_ref, wbb_ref, qkv_ref, ga_ref, gbb_ref, ubuf_ref,
                 *, tm, sub, seq, d_model):
    i = pl.program_id(0)
    cw = SB_WIDTH
    c0 = 3 * SB_WIDTH
    g0 = c0 + 3 * cw
    cwt = convw_ref[...]

    @pl.when(i == 0)
    def _():
        ubuf_ref[0:8, :] = jnp.zeros((8, cw), F32)

    chunks = range(tm // sub)
    rows = [slice(h * sub, (h + 1) * sub) for h in chunks]
    xb = [x_ref[rows[h], :].astype(BF16) for h in chunks]
    conv_in = [_dot(xb[h], w_ref[:, c0:g0]) for h in chunks]
    qkv = [_dot(xb[h], w_ref[:, 0:c0]) for h in chunks]
    gated = []
    for h in chunks:
        cb = conv_in[h][:, 0:cw]
        u = conv_in[h][:, cw:2 * cw] * conv_in[h][:, 2 * cw:3 * cw]
        ubuf_ref[0:8, :] = jnp.where((i * tm + h * sub) % seq == 0, 0.0, ubuf_ref[0:8, :])
        ubuf_ref[8:sub + 8, :] = u
        y = (cwt[0:1, :] * ubuf_ref[pl.ds(6, sub), :] + cwt[1:2, :] * ubuf_ref[pl.ds(7, sub), :]
             + cwt[2:3, :] * u)
        ubuf_ref[0:8, :] = u[sub - 8:sub, :]
        gated.append((cb * y).astype(BF16))
        qkv_ref[rows[h], 0:SB_WIDTH] = (qkv[h][:, 0:SB_WIDTH] * (SB_HEAD_DIM ** -0.5)).astype(BF16)
        qkv_ref[rows[h], SB_WIDTH:c0] = qkv[h][:, SB_WIDTH:c0].astype(BF16)
    gate_logits = [_dot(xb[h], w_ref[:, g0:g0 + 2 * d_model]) for h in chunks]
    branch_b = [_dot(gated[h], wbb_ref[...]) for h in chunks]
    for h in chunks:
        gates = jax.nn.sigmoid(gate_logits[h] + gbias_ref[...])
        ga_ref[rows[h], :] = gates[:, 0:d_model].astype(BF16)
        gbb_ref[rows[h], :] = (gates[:, d_model:2 * d_model] * branch_b[h]).astype(BF16)


def _proj_call(x2, w_in_b, gbias, conv_w, wbb_b, seq):
    t, d = x2.shape
    pw = w_in_b.shape[1]
    tm = PROJ_TILE
    sub = SUB_TILE
    cw = conv_w.shape[1]
    assert seq % sub == 0 and t % tm == 0
    kern = functools.partial(_proj_kernel, tm=tm, sub=sub, seq=seq, d_model=d)
    const = lambda i: (0, 0)
    return pl.pallas_call(
        kern,
        grid=(t // tm,),
        in_specs=[
            pl.BlockSpec((tm, d), lambda i: (i, 0)),
            pl.BlockSpec((d, pw), const),
            pl.BlockSpec((1, 2 * d), const),
            pl.BlockSpec((3, cw), const),
            pl.BlockSpec((cw, d), const),
        ],
        out_specs=[
            pl.BlockSpec((tm, 3 * SB_WIDTH), lambda i: (i, 0)),
            pl.BlockSpec((tm, d), lambda i: (i, 0)),
            pl.BlockSpec((tm, d), lambda i: (i, 0)),
        ],
        out_shape=[
            jax.ShapeDtypeStruct((t, 3 * SB_WIDTH), BF16),
            jax.ShapeDtypeStruct((t, d), BF16),
            jax.ShapeDtypeStruct((t, d), BF16),
        ],
        scratch_shapes=[pltpu.VMEM((sub + 8, cw), F32)],
        compiler_params=pltpu.CompilerParams(
            dimension_semantics=("arbitrary",), vmem_limit_bytes=VMEM_LIMIT),
        name="proj",
    )(x2, w_in_b, gbias, conv_w, wbb_b)


def _attn_kernel(q_ref, k_ref, v_ref, o_ref, acc_ref, c_ref, *, tb, ts, tq, unroll, lag):
    qi = pl.program_id(1)
    assert ts == LANES
    subs = range(tb // tq)
    lane_q = lax.broadcasted_iota(jnp.int32, (tq, HEAD_PAIR), 1)
    row = lax.broadcasted_iota(jnp.int32, (ts, ts), 0)
    col = lax.broadcasted_iota(jnp.int32, (ts, ts), 1)
    later = (row > col).astype(BF16)
    later2 = jnp.concatenate([later, later], axis=0)
    later2 = jnp.concatenate([later2, jnp.ones((2 * ts, LANES), BF16)], axis=1)
    kcol = lax.broadcasted_iota(jnp.int32, (tq, ts), 1)
    ahead = kcol - lax.broadcasted_iota(jnp.int32, (tq, ts), 0)
    halves = HEAD_PAIR // SB_HEAD_DIM
    in_head_q = [(lane_q >= h * SB_HEAD_DIM) & (lane_q < (h + 1) * SB_HEAD_DIM) for h in range(halves)]
    zero_q = jnp.zeros((tq, HEAD_PAIR), BF16)

    acc_ref[...] = jnp.zeros(acc_ref.shape, F32)
    c_ref[...] = jnp.zeros(c_ref.shape, F32)

    pairs = range(SB_WIDTH // HEAD_PAIR)
    cols = [slice(p * HEAD_PAIR, (p + 1) * HEAD_PAIR) for p in pairs]
    rows = [slice(s * tq, (s + 1) * tq) for s in subs]

    def sweep(w0):
        tiles = [(s, k) for s in subs for k in range(unroll)]
        start, bias = {}, {}
        for s, k in tiles:
            p0 = qi * tb + s * tq
            first = p0 + tq - ts * (w0 + k + 1)
            st = jnp.maximum(first, 0)
            start[s, k] = pl.multiple_of(st, tq)
            ok = (ahead < p0 - st) & (kcol < first + ts - st)
            bias[s, k] = jnp.where(ok, 0.0, MASKED_LOGIT)
        z, sp, split, inner, wb = {}, {}, {}, {}, {}
        groups = [(p, s) for p in pairs for s in subs]
        windows = range(unroll)

        def logits(g):
            p, s = groups[g]
            q = q_ref[rows[s], cols[p]]
            q2 = jnp.concatenate([jnp.where(in_head_q[h], q, zero_q) for h in range(halves)], axis=0)
            for k in windows:
                b2 = jnp.concatenate([bias[s, k]] * halves, axis=0)
                z[g, k] = _dot_nt(q2, k_ref[pl.ds(start[s, k], ts), cols[p]]) + b2

        def stay(g):
            for k in windows:
                zz = z[g, k]
                a = jnp.maximum(zz, 0.0) + jnp.log(1.0 + jnp.exp2(jnp.abs(zz) * (-LOG2E)))
                hi = a.astype(BF16)
                sp[g, k] = a
                split[g, k] = jnp.concatenate([hi, (a - hi.astype(F32)).astype(BF16)], axis=1)

        def tails(g):
            both = _dot(jnp.concatenate([split[g, k] for k in windows], axis=0), later2)
            for k in windows:
                inner[g, k] = both[k * halves * tq:(k + 1) * halves * tq, :]

        def weights(g):
            cvec = c_ref[g]
            for k in windows:
                c = (g, k)
                wb[c] = jnp.exp(z[c] - sp[c] - inner[c][:, 0:ts] - cvec).astype(BF16)
                cvec = cvec + inner[c][:, ts:ts + LANES]
            c_ref[g] = cvec

        def values(g):
            p, s = groups[g]
            out = None
            for k in windows:
                res = _dot(wb[g, k], v_ref[pl.ds(start[s, k], ts), cols[p]])
                pv = jnp.where(in_head_q[0], res[0:tq, :], res[tq:2 * tq, :])
                out = pv if out is None else out + pv
            acc_ref[rows[s], cols[p]] += out

        stages = (logits, stay, tails, weights, values)
        for step in range(len(groups) + lag * (len(stages) - 1)):
            for s in reversed(range(len(stages))):
                g = step - lag * s
                if 0 <= g < len(groups):
                    stages[s](g)

    def cond(carry):
        w0, cmin = carry
        return jnp.logical_and(w0 * ts < (qi + 1) * tb, cmin < TAIL_CUTOFF)

    def body(carry):
        w0, _ = carry
        sweep(w0)
        return w0 + unroll, jnp.min(c_ref[...])

    lax.while_loop(cond, body, (jnp.int32(0), jnp.float32(0.0)))
    o_ref[...] = acc_ref[...].astype(BF16)


def _attn_call(qkv3):
    b, s, _ = qkv3.shape
    tb = ATTN_TILE
    tq = ATTN_SUB
    n_groups = (SB_WIDTH // HEAD_PAIR) * (tb // tq)
    kern = functools.partial(_attn_kernel, tb=tb, ts=ATTN_WINDOW, tq=tq, unroll=ATTN_UNROLL, lag=ATTN_LAG)
    return pl.pallas_call(
        kern,
        grid=(b, s // tb),
        in_specs=[
            pl.BlockSpec((None, tb, SB_WIDTH), lambda bi, qi: (bi, qi, 0)),
            pl.BlockSpec((None, s, SB_WIDTH), lambda bi, qi: (bi, 0, 1)),
            pl.BlockSpec((None, s, SB_WIDTH), lambda bi, qi: (bi, 0, 2)),
        ],
        out_specs=pl.BlockSpec((None, tb, SB_WIDTH), lambda bi, qi: (bi, qi, 0)),
        out_shape=jax.ShapeDtypeStruct((b, s, SB_WIDTH), BF16),
        scratch_shapes=[pltpu.VMEM((tb, SB_WIDTH), F32),
                        pltpu.VMEM((n_groups, (HEAD_PAIR // SB_HEAD_DIM) * tq, LANES), F32)],
        compiler_params=pltpu.CompilerParams(
            dimension_semantics=("arbitrary", "arbitrary"), vmem_limit_bytes=VMEM_LIMIT),
        name="attn",
    )(qkv3, qkv3, qkv3)


def _mix_kernel(attn_ref, ga_ref, gbb_ref, x_ref, wa_ref, wo_ref, g1_ref, b1_ref, wrt_ref, brt_ref,
                h1_ref, meta_ref, xls_ref, tcnt_ref, *, tile, tm, alpha):
    chunks = range(tile // tm)
    reps = tm // LANES
    rows = [slice(h * tm, (h + 1) * tm) for h in chunks]
    branch_a = [_dot(attn_ref[rows[h], :], wa_ref[...]) for h in chunks]
    mixed = [_dot((ga_ref[rows[h], :] * branch_a[h] + gbb_ref[rows[h], :]).astype(BF16), wo_ref[...])
             for h in chunks]
    h1b = []
    for h in chunks:
        h1 = _layer_norm(alpha * x_ref[rows[h], :] + mixed[h], g1_ref[...], b1_ref[...])
        h1_ref[rows[h], :] = h1
        h1b.append(h1.astype(BF16))
    bias = jnp.concatenate([brt_ref[...]] * reps, axis=1)
    logits = [_dot_nt(wrt_ref[...], h1b[h]) + bias for h in chunks]
    route = [_route(logits[h], tm) for h in chunks]

    trow = lax.broadcasted_iota(jnp.int32, (tm, tm), 0)
    tcol = lax.broadcasted_iota(jnp.int32, (tm, tm), 1)
    earlier = (trow < tcol).astype(BF16)
    e32r = lax.broadcasted_iota(jnp.int32, (N_EXPERTS, N_EXPERTS), 0)
    e32c = lax.broadcasted_iota(jnp.int32, (N_EXPERTS, N_EXPERTS), 1)
    lower = (e32c < e32r).astype(BF16)
    ones_b = jnp.ones((tm, LANES), BF16)
    before_tile = [_dot(route[h].onehot, earlier) for h in chunks]
    tile_cnt = [_dot(route[h].onehot, ones_b) for h in chunks]
    smaller = [_dot(lower, route[h].onehot).astype(BF16) for h in chunks]
    seg_off = [_dot(smaller[h], ones_b) for h in chunks]

    zrow = jnp.zeros((1, tm), F32)
    perm = []
    for h in chunks:
        rt = route[h]
        tcnt_ref[h] = tile_cnt[h]
        local = before_tile[h] + jnp.concatenate([seg_off[h]] * reps, axis=1)
        pos1 = jnp.sum(jnp.where(rt.is1, local, 0.0), axis=0, keepdims=True)
        pos2 = jnp.sum(jnp.where(rt.is2, local, 0.0), axis=0, keepdims=True)
        meta_ref[:, rows[h]] = jnp.concatenate(
            [rt.e1.astype(F32), rt.e2.astype(F32), rt.w1, rt.w2, pos1, pos2, zrow, zrow], axis=0)
        srow = lax.broadcasted_iota(jnp.int32, (2 * tm, tm), 0)
        perm.append(jnp.where((srow == pos1.astype(jnp.int32)) | (srow == pos2.astype(jnp.int32)), 1.0, 0.0)
                    .astype(BF16))
    xs = [_dot(perm[h], h1b[h]) for h in chunks]
    for h in chunks:
        base = h * 2 * tm * SUBLANES
        for c in range(xs[h].shape[1] // LANES):
            xls_ref[pl.ds(base + c, 2 * tm, stride=SUBLANES), :] = xs[h][:, c * LANES:(c + 1) * LANES]


class _Route(NamedTuple):
    e1: jax.Array
    e2: jax.Array
    w1: jax.Array
    w2: jax.Array
    is1: jax.Array
    is2: jax.Array
    onehot: jax.Array


def _route(lt, tm):
    r = [lt[k:k + 1, :] for k in range(N_GROUPS)]
    gmax = jnp.maximum(jnp.maximum(r[0], r[1]), jnp.maximum(r[2], r[3]))
    gidx = jnp.where(r[0] == gmax, 0, jnp.where(r[1] == gmax, 1, jnp.where(r[2] == gmax, 2, 3)))
    gsum = (jnp.exp(r[0] - gmax) + jnp.exp(r[1] - gmax)) + (jnp.exp(r[2] - gmax) + jnp.exp(r[3] - gmax))
    gprob = 1.0 / gsum
    epg = EXPERTS_PER_GROUP
    slabs = [lt[8 + g * epg:8 + (g + 1) * epg, :] for g in range(N_GROUPS)]
    el = jnp.where(gidx == 0, slabs[0], jnp.where(gidx == 1, slabs[1], jnp.where(gidx == 2, slabs[2], slabs[3])))
    r8 = lax.broadcasted_iota(jnp.int32, (epg, tm), 0)
    m1 = jnp.max(el, axis=0, keepdims=True)
    i1 = jnp.min(jnp.where(el == m1, r8, epg), axis=0, keepdims=True)
    el2 = jnp.where(r8 == i1, -jnp.inf, el)
    m2 = jnp.max(el2, axis=0, keepdims=True)
    i2 = jnp.min(jnp.where(el2 == m2, r8, epg), axis=0, keepdims=True)
    dlt = jnp.exp(m2 - m1)
    w1 = gprob / (1.0 + dlt)
    w2 = gprob * dlt / (1.0 + dlt)
    e1 = gidx * epg + i1
    e2 = gidx * epg + i2
    r32 = lax.broadcasted_iota(jnp.int32, (N_EXPERTS, tm), 0)
    is1 = r32 == e1
    is2 = r32 == e2
    onehot = jnp.where(is1 | is2, 1.0, 0.0).astype(BF16)
    return _Route(e1, e2, w1, w2, is1, is2, onehot)


def _mix_call(attn2, gate_a, gbb, x2, wa_b, wo_b, g1, b1, wrt_b, brt, alpha):
    t, d = x2.shape
    tile = PROJ_TILE
    kern = functools.partial(_mix_kernel, tile=tile, tm=SUB_TILE, alpha=alpha)
    const = lambda i: (0, 0)
    rowblk = lambda i: (i, 0)
    tm = tile
    return pl.pallas_call(
        kern,
        grid=(t // tm,),
        in_specs=[
            pl.BlockSpec((tm, SB_WIDTH), rowblk),
            pl.BlockSpec((tm, d), rowblk),
            pl.BlockSpec((tm, d), rowblk),
            pl.BlockSpec((tm, d), rowblk),
            pl.BlockSpec((SB_WIDTH, d), const),
            pl.BlockSpec((d, d), const),
            pl.BlockSpec((1, d), const),
            pl.BlockSpec((1, d), const),
            pl.BlockSpec((ROUTER_ROWS, d), const),
            pl.BlockSpec((ROUTER_ROWS, LANES), const),
        ],
        out_specs=[
            pl.BlockSpec((tm, d), rowblk),
            pl.BlockSpec((8, tm), lambda i: (0, i)),
            pl.BlockSpec((2 * tm * SUBLANES, LANES), rowblk),
            pl.BlockSpec((tile // SUB_TILE, N_EXPERTS, LANES), lambda i: (i, 0, 0)),
        ],
        out_shape=[
            jax.ShapeDtypeStruct((t, d), F32),
            jax.ShapeDtypeStruct((8, t), F32),
            jax.ShapeDtypeStruct((2 * t * SUBLANES, LANES), F32),
            jax.ShapeDtypeStruct((t // SUB_TILE, N_EXPERTS, LANES), F32),
        ],
        compiler_params=pltpu.CompilerParams(
            dimension_semantics=("arbitrary",), vmem_limit_bytes=VMEM_LIMIT),
        name="mix",
    )(attn2, gate_a, gbb, x2, wa_b, wo_b, g1, b1, wrt_b, brt)


def _row_copy(src_ref, src_row, dst_ref, dst_row, sem):
    return pltpu.make_async_copy(src_ref.at[pl.ds(src_row, 1), :], dst_ref.at[pl.ds(dst_row, 1), :], sem)


def _tok_rows(tok, n_tok):
    return pl.ds(pl.multiple_of(tok * SUBLANES, SUBLANES), pl.multiple_of(n_tok * SUBLANES, SUBLANES))


def _moe_kernel(be_ref, r0_ref, t0_ref, t1_ref, nv_ref, nu_ref, nxt_ref, cumt_ref, soff_ref,
                xls_ref, wg_ref, wu_ref, wd_ref, y_ref,
                xbuf_ref, zero_ref, sem, wgf_ref, wuf_ref, wdf_ref, wsem, wgb_ref, wub_ref, wdb_ref, state_ref,
                *, blk, sub, tile_rows):
    i = pl.program_id(0)
    n_used = nu_ref[0]
    e = be_ref[i]
    cur = i % 2
    ne = N_EXPERTS
    w_hbm = (wg_ref, wu_ref, wd_ref)
    wf_refs = (wgf_ref, wuf_ref, wdf_ref)

    def fetch_weights(expert, slot):
        for src, dst in zip(w_hbm, wf_refs):
            pltpu.make_async_copy(src.at[expert], dst.at[slot], wsem.at[slot]).start()

    def wait_weights(slot):
        for src, dst in zip(w_hbm, wf_refs):
            pltpu.make_async_copy(src.at[0], dst.at[slot], wsem.at[slot]).wait()

    def gather(b, slot):
        eb = be_ref[b]
        r0 = r0_ref[b]
        nv = nv_ref[b]

        def seg(t, c):
            first = cumt_ref[t * ne + eb]
            lo = jnp.maximum(first, r0)
            n = jnp.minimum(cumt_ref[(t + 1) * ne + eb], r0 + nv) - lo

            @pl.when(n > 0)
            def _():
                src = t * tile_rows + soff_ref[t * ne + eb] + (lo - first)
                pltpu.make_async_copy(xls_ref.at[_tok_rows(src, n), :],
                                      xbuf_ref.at[slot, _tok_rows(lo - r0, n), :], sem.at[slot]).start()
            return c

        lax.fori_loop(t0_ref[b], t1_ref[b] + 1, seg, 0)

        @pl.when(nv < blk)
        def _():
            pltpu.make_async_copy(zero_ref.at[_tok_rows(0, blk - nv), :],
                                  xbuf_ref.at[slot, _tok_rows(nv, blk - nv), :], sem.at[slot]).start()

    def drain(slot):
        pltpu.make_async_copy(xls_ref.at[pl.ds(0, blk * SUBLANES), :], xbuf_ref.at[slot], sem.at[slot]).wait()

    @pl.when(i == 0)
    def _():
        state_ref[0] = -1
        state_ref[1] = 0
        zero_ref[...] = jnp.zeros(zero_ref.shape, F32)

    @pl.when(jnp.logical_and(i == 0, n_used > 0))
    def _():
        gather(0, 0)
        fetch_weights(e, 0)

    @pl.when(i < n_used)
    def _():
        @pl.when(e != state_ref[0])
        def _():
            slot = state_ref[1]
            wait_weights(slot)
            wgb_ref[...] = wf_refs[0][slot].astype(BF16)
            wub_ref[...] = wf_refs[1][slot].astype(BF16)
            wdb_ref[...] = wf_refs[2][slot].astype(BF16)
            state_ref[0] = e
            state_ref[1] = 1 - slot

            @pl.when(nxt_ref[i] >= 0)
            def _():
                fetch_weights(nxt_ref[i], 1 - slot)

        gather(jnp.minimum(i + 1, n_used - 1), 1 - cur)
        drain(cur)

        def ffn(chunks):
            xbs = [jnp.concatenate([xbuf_ref[cur, pl.ds(h * sub * SUBLANES + c, sub, stride=SUBLANES), :]
                                    for c in range(SUBLANES)], axis=1).astype(BF16) for h in chunks]
            gate = [_dot(xb, wgb_ref[...]) for xb in xbs]
            up = [_dot(xb, wub_ref[...]) for xb in xbs]
            hidden = [(g * jax.nn.sigmoid(g) * u).astype(BF16) for g, u in zip(gate, up)]
            for h, hid in zip(chunks, hidden):
                y = _dot(hid, wdb_ref[...])
                for c in range(SUBLANES):
                    y_ref[pl.ds(h * sub * SUBLANES + c, sub, stride=SUBLANES), :] = y[:, c * LANES:(c + 1) * LANES]

        n_chunks = blk // sub
        live = (nv_ref[i] + sub - 1) // sub
        for n in range(1, n_chunks + 1):
            @pl.when(live == n)
            def _(n=n):
                ffn(list(range(n)))
                if n < n_chunks:
                    y_ref[n * sub * SUBLANES:blk * SUBLANES, :] = jnp.zeros(((blk - n * sub) * SUBLANES, LANES), F32)

    @pl.when(i == n_used - 1)
    def _():
        drain(1 - cur)

    @pl.when(i >= n_used)
    def _():
        y_ref[...] = jnp.zeros(y_ref.shape, F32)


def _moe_call(tables, xls, tile_rows, w_gate, w_up, w_down):
    n_blocks = tables[0].shape[0]
    blk = EXPERT_BLOCK
    _, d, de = w_gate.shape
    grid_spec = pltpu.PrefetchScalarGridSpec(
        num_scalar_prefetch=len(tables),
        grid=(n_blocks,),
        in_specs=[pl.BlockSpec(memory_space=pl.ANY)] * 4,
        out_specs=pl.BlockSpec((blk * SUBLANES, LANES), lambda i, *_: (i, 0)),
        scratch_shapes=[
            pltpu.VMEM((2, blk * SUBLANES, LANES), F32), pltpu.VMEM((blk * SUBLANES, LANES), F32),
            pltpu.SemaphoreType.DMA((2,)),
            pltpu.VMEM((2, d, de), F32), pltpu.VMEM((2, d, de), F32), pltpu.VMEM((2, de, d), F32),
            pltpu.SemaphoreType.DMA((2,)),
            pltpu.VMEM((d, de), BF16), pltpu.VMEM((d, de), BF16), pltpu.VMEM((de, d), BF16),
            pltpu.SMEM((2,), jnp.int32),
        ],
    )
    return pl.pallas_call(
        functools.partial(_moe_kernel, blk=blk, sub=SUB_TILE, tile_rows=tile_rows),
        grid_spec=grid_spec,
        out_shape=jax.ShapeDtypeStruct((n_blocks * blk * SUBLANES, LANES), F32),
        compiler_params=pltpu.CompilerParams(
            dimension_semantics=("arbitrary",), vmem_limit_bytes=VMEM_LIMIT),
        name="moe",
    )(*tables, xls, w_gate, w_up, w_down)


def _combine_kernel(pstart_ref, cumt_ref, soff_ref, tcnt_ref,
                    h1_ref, meta_ref, y_ref, g2_ref, b2_ref, o_ref, ybuf_ref, sem, *, tile, tm, alpha):
    i = pl.program_id(0)
    last = pl.num_programs(0) - 1
    cur = i % 2
    ne = N_EXPERTS
    chunks = range(tile // tm)
    seg_rows = 2 * tm * SUBLANES

    def gather(step, slot):
        for h in chunks:
            t = step * len(chunks) + h

            def seg(e, c, t=t, h=h):
                n = tcnt_ref[t * ne + e]

                @pl.when(n > 0)
                def _():
                    src = pstart_ref[e] + cumt_ref[t * ne + e]
                    dst = h * 2 * tm + soff_ref[t * ne + e]
                    pltpu.make_async_copy(y_ref.at[_tok_rows(src, n), :],
                                          ybuf_ref.at[slot, _tok_rows(dst, n), :], sem.at[slot]).start()
                return c

            lax.fori_loop(0, ne, seg, 0)

    def drain(slot):
        pltpu.make_async_copy(y_ref.at[pl.ds(0, len(chunks) * seg_rows), :], ybuf_ref.at[slot],
                              sem.at[slot]).wait()

    @pl.when(i == 0)
    def _():
        gather(0, 0)

    gather(jnp.minimum(i + 1, last), 1 - cur)
    drain(cur)
    srow = lax.broadcasted_iota(jnp.int32, (2 * tm, tm), 0)
    ys = [jnp.concatenate([ybuf_ref[cur, pl.ds(h * seg_rows + c, 2 * tm, stride=SUBLANES), :]
                           for c in range(SUBLANES)], axis=1).astype(BF16) for h in chunks]
    picked = {}
    for h in chunks:
        cols = slice(h * tm, (h + 1) * tm)
        for k in range(2):
            pos = meta_ref[4 + k:5 + k, cols].astype(jnp.int32)
            picked[h, k] = _dot_tn(jnp.where(srow == pos, 1.0, 0.0).astype(BF16), ys[h])
    for h in chunks:
        rows = slice(h * tm, (h + 1) * tm)
        rw = meta_ref[:, rows].T
        ffn = rw[:, 2:3] * picked[h, 0] + rw[:, 3:4] * picked[h, 1]
        o_ref[rows, :] = _layer_norm(alpha * h1_ref[rows, :] + ffn, g2_ref[...], b2_ref[...])

    @pl.when(i == last)
    def _():
        drain(1 - cur)


def _combine_call(tables, h1, meta, y_slots, g2, b2, alpha):
    t, d = h1.shape
    tm = PROJ_TILE
    kern = functools.partial(_combine_kernel, tile=tm, tm=SUB_TILE, alpha=alpha)
    const = lambda i, *_: (0, 0)
    grid_spec = pltpu.PrefetchScalarGridSpec(
        num_scalar_prefetch=len(tables),
        grid=(t // tm,),
        in_specs=[
            pl.BlockSpec((tm, d), lambda i, *_: (i, 0)),
            pl.BlockSpec((8, tm), lambda i, *_: (0, i)),
            pl.BlockSpec(memory_space=pl.ANY),
            pl.BlockSpec((1, d), const),
            pl.BlockSpec((1, d), const),
        ],
        out_specs=pl.BlockSpec((tm, d), lambda i, *_: (i, 0)),
        scratch_shapes=[pltpu.VMEM((2, 2 * tm * SUBLANES, LANES), F32), pltpu.SemaphoreType.DMA((2,))],
    )
    return pl.pallas_call(
        kern,
        grid_spec=grid_spec,
        out_shape=jax.ShapeDtypeStruct((t, d), F32),
        compiler_params=pltpu.CompilerParams(
            dimension_semantics=("arbitrary",), vmem_limit_bytes=VMEM_LIMIT),
        name="combine",
    )(*tables, h1, meta, y_slots, g2, b2)


def _layer(h, w_in, gate_bias, conv_w, w_branch_a, w_branch_b, w_out, ln1_g, ln1_b,
           w_router_g, b_router_g, w_router_e, b_router_e, w_gate, w_up, w_down, ln2_g, ln2_b, alpha):
    bsz, seq, d = h.shape
    t = bsz * seq
    x2 = h.reshape(t, d)

    qkv, gate_a, gbb = _proj_call(x2, w_in.astype(BF16), gate_bias.reshape(1, 2 * d), conv_w,
                                  w_branch_b.astype(BF16), seq)
    attn = _attn_call(qkv.reshape(bsz, seq, 3 * SB_WIDTH)).reshape(t, SB_WIDTH)

    wrt = jnp.zeros((ROUTER_ROWS, d), F32).at[0:N_GROUPS].set(w_router_g.T).at[8:].set(w_router_e.T)
    brt = jnp.zeros((ROUTER_ROWS,), F32).at[0:N_GROUPS].set(b_router_g).at[8:].set(b_router_e.reshape(-1))
    brt = jnp.broadcast_to(brt[:, None], (ROUTER_ROWS, LANES))
    h1, meta, xls, tcnt = _mix_call(attn, gate_a, gbb, x2, w_branch_a.astype(BF16), w_out.astype(BF16),
                                    ln1_g.reshape(1, d), ln1_b.reshape(1, d), wrt.astype(BF16), brt, alpha)

    blk = EXPERT_BLOCK
    tile_cnt = tcnt[:, :, 0].astype(jnp.int32)
    n_tiles = tile_cnt.shape[0]
    cum_incl = jnp.cumsum(tile_cnt, axis=0)
    cumt = jnp.concatenate([jnp.zeros((1, N_EXPERTS), jnp.int32), cum_incl], axis=0)
    counts = cumt[-1]
    soff = jnp.cumsum(tile_cnt, axis=1) - tile_cnt
    padded = (counts + blk - 1) // blk * blk
    pad_end = jnp.cumsum(padded)
    pad_start = pad_end - padded
    n_blocks = (2 * t) // blk + N_EXPERTS
    block_start = jnp.arange(n_blocks, dtype=jnp.int32) * blk
    block_expert = jnp.minimum(jnp.sum(pad_end[None, :] <= block_start[:, None], axis=1), N_EXPERTS - 1)
    n_used = (pad_end[-1] // blk).reshape(1)

    be = block_expert.astype(jnp.int32)
    mine = be[:, None] == jnp.arange(N_EXPERTS, dtype=jnp.int32)[None, :]

    def of_block(table):
        return jnp.sum(jnp.where(mine, table[..., None, :], 0), axis=-1)

    r0 = block_start - of_block(pad_start)
    nv = jnp.clip(of_block(counts) - r0, 0, blk)
    t0 = jnp.minimum(jnp.sum(of_block(cum_incl) <= r0[None, :], axis=0), n_tiles - 1)
    t1 = jnp.sum(of_block(cumt[:-1]) < (r0 + nv)[None, :], axis=0) - 1
    after = of_block(pad_end) // blk
    be_after = jnp.sum(jnp.where(after[:, None] == jnp.arange(n_blocks, dtype=jnp.int32)[None, :], be[None, :], 0),
                       axis=1)
    nxt = jnp.where(after < n_used[0], be_after, -1)
    tables = (be, r0, t0, t1, nv, n_used, nxt, cumt.reshape(-1), soff.reshape(-1))
    y_slots = _moe_call(tuple(a.astype(jnp.int32) for a in tables), xls, 2 * SUB_TILE, w_gate, w_up, w_down)
    ctables = (pad_start, cumt.reshape(-1), soff.reshape(-1), tile_cnt.reshape(-1))
    out = _combine_call(tuple(a.astype(jnp.int32) for a in ctables), h1, meta, y_slots,
                        ln2_g.reshape(1, d), ln2_b.reshape(1, d), alpha)
    return out.reshape(bsz, seq, d)


def kernel(x, w_in, gate_bias, conv_w, w_branch_a, w_branch_b, w_out, ln1_g, ln1_b, w_router_g, b_router_g,
           w_router_e, b_router_e, w_gate, w_up, w_down, ln2_g, ln2_b):
    depth = w_in.shape[0]
    alpha = (2.0 * depth) ** 0.25
    h = x
    for l in range(depth):
        h = _layer(h, w_in[l], gate_bias[l], conv_w[l], w_branch_a[l], w_branch_b[l], w_out[l], ln1_g[l],
                   ln1_b[l], w_router_g[l], b_router_g[l], w_router_e[l], b_router_e[l], w_gate[l], w_up[l],
                   w_down[l], ln2_g[l], ln2_b[l], alpha)
    return h
```

```python
import functools
from typing import NamedTuple

import jax
import jax.numpy as jnp
from jax import lax
from jax.experimental import pallas as pl
from jax.experimental.pallas import tpu as pltpu

F32 = jnp.float32
BF16 = jnp.bfloat16

SB_HEADS = 8
SB_HEAD_DIM = 64
SB_WIDTH = SB_HEADS * SB_HEAD_DIM
N_GROUPS = 4
EXPERTS_PER_GROUP = 8
N_EXPERTS = N_GROUPS * EXPERTS_PER_GROUP
LN_EPS = 1e-5

LANES = 128
SUBLANES = 8
HEAD_PAIR = LANES
PROJ_TILE = 1024
SUB_TILE = 256
ATTN_TILE = 512
ATTN_WINDOW = 128
ATTN_SUB = 64
ATTN_UNROLL = 2
ATTN_LAG = 2
EXPERT_BLOCK = 1024
ROUTER_ROWS = 8 + N_EXPERTS
TAIL_CUTOFF = 110.0
MASKED_LOGIT = -1e30
LOG2E = 1.4426950408889634
VMEM_LIMIT = 56 * 1024 * 1024


def _dot(a, b):
    return jnp.dot(a, b, preferred_element_type=F32)


def _dot_nt(a, b):
    return lax.dot_general(a, b, (((1,), (1,)), ((), ())), preferred_element_type=F32)


def _dot_tn(a, b):
    return lax.dot_general(a, b, (((0,), (0,)), ((), ())), preferred_element_type=F32)


def _layer_norm(y, g, b):
    mu = jnp.mean(y, axis=-1, keepdims=True)
    d = y - mu
    var = jnp.mean(d * d, axis=-1, keepdims=True)
    return d * lax.rsqrt(var + LN_EPS) * g + b


def _proj_kernel(x_ref, w_ref, gbias_ref, convw_ref, wbb_ref, qkv_ref, ga_ref, gbb_ref, ubuf_ref,
                 *, tm, sub, seq, d_model):
    i = pl.program_id(0)
    cw = SB_WIDTH
    c0 = 3 * SB_WIDTH
    g0 = c0 + 3 * cw
    cwt = convw_ref[...]

    @pl.when(i == 0)
    def _():
        ubuf_ref[0:8, :] = jnp.zeros((8, cw), F32)

    chunks = range(tm // sub)
    rows = [slice(h * sub, (h + 1) * sub) for h in chunks]
    xb = [x_ref[rows[h], :].astype(BF16) for h in chunks]
    conv_in = [_dot(xb[h], w_ref[:, c0:g0]) for h in chunks]
    qkv = [_dot(xb[h], w_ref[:, 0:c0]) for h in chunks]
    gated = []
    for h in chunks:
        cb = conv_in[h][:, 0:cw]
        u = conv_in[h][:, cw:2 * cw] * conv_in[h][:, 2 * cw:3 * cw]
        ubuf_ref[0:8, :] = jnp.where((i * tm + h * sub) % seq == 0, 0.0, ubuf_ref[0:8, :])
        ubuf_ref[8:sub + 8, :] = u
        y = (cwt[0:1, :] * ubuf_ref[pl.ds(6, sub), :] + cwt[1:2, :] * ubuf_ref[pl.ds(7, sub), :]
             + cwt[2:3, :] * u)
        ubuf_ref[0:8, :] = u[sub - 8:sub, :]
        gated.append((cb * y).astype(BF16))
        qkv_ref[rows[h], 0:SB_WIDTH] = (qkv[h][:, 0:SB_WIDTH] * (SB_HEAD_DIM ** -0.5)).astype(BF16)
        qkv_ref[rows[h], SB_WIDTH:c0] = qkv[h][:, SB_WIDTH:c0].astype(BF16)
    gate_logits = [_dot(xb[h], w_ref[:, g0:g0 + 2 * d_model]) for h in chunks]
    branch_b = [_dot(gated[h], wbb_ref[...]) for h in chunks]
    for h in chunks:
        gates = jax.nn.sigmoid(gate_logits[h] + gbias_ref[...])
        ga_ref[rows[h], :] = gates[:, 0:d_model].astype(BF16)
        gbb_ref[rows[h], :] = (gates[:, d_model:2 * d_model] * branch_b[h]).astype(BF16)


def _proj_call(x2, w_in_b, gbias, conv_w, wbb_b, seq):
    t, d = x2.shape
    pw = w_in_b.shape[1]
    tm = PROJ_TILE
    sub = SUB_TILE
    cw = conv_w.shape[1]
    assert seq % sub == 0 and t % tm == 0
    kern = functools.partial(_proj_kernel, tm=tm, sub=sub, seq=seq, d_model=d)
    const = lambda i: (0, 0)
    return pl.pallas_call(
        kern,
        grid=(t // tm,),
        in_specs=[
            pl.BlockSpec((tm, d), lambda i: (i, 0)),
            pl.BlockSpec((d, pw), const),
            pl.BlockSpec((1, 2 * d), const),
            pl.BlockSpec((3, cw), const),
            pl.BlockSpec((cw, d), const),
        ],
        out_specs=[
            pl.BlockSpec((tm, 3 * SB_WIDTH), lambda i: (i, 0)),
            pl.BlockSpec((tm, d), lambda i: (i, 0)),
            pl.BlockSpec((tm, d), lambda i: (i, 0)),
        ],
        out_shape=[
            jax.ShapeDtypeStruct((t, 3 * SB_WIDTH), BF16),
            jax.ShapeDtypeStruct((t, d), BF16),
            jax.ShapeDtypeStruct((t, d), BF16),
        ],
        scratch_shapes=[pltpu.VMEM((sub + 8, cw), F32)],
        compiler_params=pltpu.CompilerParams(
            dimension_semantics=("arbitrary",), vmem_limit_bytes=VMEM_LIMIT),
        name="proj",
    )(x2, w_in_b, gbias, conv_w, wbb_b)


def _attn_kernel(q_ref, k_ref, v_ref, o_ref, acc_ref, c_ref, *, tb, ts, tq, unroll, lag):
    qi = pl.program_id(1)
    assert ts == LANES
    subs = range(tb // tq)
    lane_q = lax.broadcasted_iota(jnp.int32, (tq, HEAD_PAIR), 1)
    row = lax.broadcasted_iota(jnp.int32, (ts, ts), 0)
    col = lax.broadcasted_iota(jnp.int32, (ts, ts), 1)
    later = (row > col).astype(BF16)
    later2 = jnp.concatenate([later, later], axis=0)
    later2 = jnp.concatenate([later2, jnp.ones((2 * ts, LANES), BF16)], axis=1)
    kcol = lax.broadcasted_iota(jnp.int32, (tq, ts), 1)
    ahead = kcol - lax.broadcasted_iota(jnp.int32, (tq, ts), 0)
    halves = HEAD_PAIR // SB_HEAD_DIM
    in_head_q = [(lane_q >= h * SB_HEAD_DIM) & (lane_q < (h + 1) * SB_HEAD_DIM) for h in range(halves)]
    zero_q = jnp.zeros((tq, HEAD_PAIR), BF16)

    acc_ref[...] = jnp.zeros(acc_ref.shape, F32)
    c_ref[...] = jnp.zeros(c_ref.shape, F32)

    pairs = range(SB_WIDTH // HEAD_PAIR)
    cols = [slice(p * HEAD_PAIR, (p + 1) * HEAD_PAIR) for p in pairs]
    rows = [slice(s * tq, (s + 1) * tq) for s in subs]

    def sweep(w0):
        tiles = [(s, k) for s in subs for k in range(unroll)]
        start, bias = {}, {}
        for s, k in tiles:
            p0 = qi * tb + s * tq
            first = p0 + tq - ts * (w0 + k + 1)
            st = jnp.maximum(first, 0)
            start[s, k] = pl.multiple_of(st, tq)
            ok = (ahead < p0 - st) & (kcol < first + ts - st)
            bias[s, k] = jnp.where(ok, 0.0, MASKED_LOGIT)
        z, sp, split, inner, wb = {}, {}, {}, {}, {}
        groups = [(p, s) for p in pairs for s in subs]
        windows = range(unroll)

        def logits(g):
            p, s = groups[g]
            q = q_ref[rows[s], cols[p]]
            q2 = jnp.concatenate([jnp.where(in_head_q[h], q, zero_q) for h in range(halves)], axis=0)
            for k in windows:
                b2 = jnp.concatenate([bias[s, k]] * halves, axis=0)
                z[g, k] = _dot_nt(q2, k_ref[pl.ds(start[s, k], ts), cols[p]]) + b2

        def stay(g):
            for k in windows:
                zz = z[g, k]
                a = jnp.maximum(zz, 0.0) + jnp.log(1.0 + jnp.exp2(jnp.abs(zz) * (-LOG2E)))
                hi = a.astype(BF16)
                sp[g, k] = a
                split[g, k] = jnp.concatenate([hi, (a - hi.astype(F32)).astype(BF16)], axis=1)

        def tails(g):
            both = _dot(jnp.concatenate([split[g, k] for k in windows], axis=0), later2)
            for k in windows:
                inner[g, k] = both[k * halves * tq:(k + 1) * halves * tq, :]

        def weights(g):
            cvec = c_ref[g]
            for k in windows:
                c = (g, k)
                wb[c] = jnp.exp(z[c] - sp[c] - inner[c][:, 0:ts] - cvec).astype(BF16)
                cvec = cvec + inner[c][:, ts:ts + LANES]
            c_ref[g] = cvec

        def values(g):
            p, s = groups[g]
            out = None
            for k in windows:
                res = _dot(wb[g, k], v_ref[pl.ds(start[s, k], ts), cols[p]])
                pv = jnp.where(in_head_q[0], res[0:tq, :], res[tq:2 * tq, :])
                out = pv if out is None else out + pv
            acc_ref[rows[s], cols[p]] += out

        stages = (logits, stay, tails, weights, values)
        for step in range(len(groups) + lag * (len(stages) - 1)):
            for s in reversed(range(len(stages))):
                g = step - lag * s
                if 0 <= g < len(groups):
                    stages[s](g)

    def cond(carry):
        w0, cmin = carry
        return jnp.logical_and(w0 * ts < (qi + 1) * tb, cmin < TAIL_CUTOFF)

    def body(carry):
        w0, _ = carry
        sweep(w0)
        return w0 + unroll, jnp.min(c_ref[...])

    lax.while_loop(cond, body, (jnp.int32(0), jnp.float32(0.0)))
    o_ref[...] = acc_ref[...].astype(BF16)


def _attn_call(qkv3):
    b, s, _ = qkv3.shape
    tb = ATTN_TILE
    tq = ATTN_SUB
    n_groups = (SB_WIDTH // HEAD_PAIR) * (tb // tq)
    kern = functools.partial(_attn_kernel, tb=tb, ts=ATTN_WINDOW, tq=tq, unroll=ATTN_UNROLL, lag=ATTN_LAG)
    return pl.pallas_call(
        kern,
        grid=(b, s // tb),
        in_specs=[
            pl.BlockSpec((None, tb, SB_WIDTH), lambda bi, qi: (bi, qi, 0)),
            pl.BlockSpec((None, s, SB_WIDTH), lambda bi, qi: (bi, 0, 1)),
            pl.BlockSpec((None, s, SB_WIDTH), lambda bi, qi: (bi, 0, 2)),
        ],
        out_specs=pl.BlockSpec((None, tb, SB_WIDTH), lambda bi, qi: (bi, qi, 0)),
        out_shape=jax.ShapeDtypeStruct((b, s, SB_WIDTH), BF16),
        scratch_shapes=[pltpu.VMEM((tb, SB_WIDTH), F32),
                        pltpu.VMEM((n_groups, (HEAD_PAIR // SB_HEAD_DIM) * tq, LANES), F32)],
        compiler_params=pltpu.CompilerParams(
            dimension_semantics=("arbitrary", "arbitrary"), vmem_limit_bytes=VMEM_LIMIT),
        name="attn",
    )(qkv3, qkv3, qkv3)


def _mix_kernel(attn_ref, ga_ref, gbb_ref, x_ref, wa_ref, wo_ref, g1_ref, b1_ref, wrt_ref, brt_ref,
                h1_ref, meta_ref, xls_ref, tcnt_ref, *, tile, tm, alpha):
    chunks = range(tile // tm)
    reps = tm // LANES
    rows = [slice(h * tm, (h + 1) * tm) for h in chunks]
    branch_a = [_dot(attn_ref[rows[h], :], wa_ref[...]) for h in chunks]
    mixed = [_dot((ga_ref[rows[h], :] * branch_a[h] + gbb_ref[rows[h], :]).astype(BF16), wo_ref[...])
             for h in chunks]
    h1b = []
    for h in chunks:
        h1 = _layer_norm(alpha * x_ref[rows[h], :] + mixed[h], g1_ref[...], b1_ref[...])
        h1_ref[rows[h], :] = h1
        h1b.append(h1.astype(BF16))
    bias = jnp.concatenate([brt_ref[...]] * reps, axis=1)
    logits = [_dot_nt(wrt_ref[...], h1b[h]) + bias for h in chunks]
    route = [_route(logits[h], tm) for h in chunks]

    trow = lax.broadcasted_iota(jnp.int32, (tm, tm), 0)
    tcol = lax.broadcasted_iota(jnp.int32, (tm, tm), 1)
    earlier = (trow < tcol).astype(BF16)
    e32r = lax.broadcasted_iota(jnp.int32, (N_EXPERTS, N_EXPERTS), 0)
    e32c = lax.broadcasted_iota(jnp.int32, (N_EXPERTS, N_EXPERTS), 1)
    lower = (e32c < e32r).astype(BF16)
    ones_b = jnp.ones((tm, LANES), BF16)
    before_tile = [_dot(route[h].onehot, earlier) for h in chunks]
    tile_cnt = [_dot(route[h].onehot, ones_b) for h in chunks]
    smaller = [_dot(lower, route[h].onehot).astype(BF16) for h in chunks]
    seg_off = [_dot(smaller[h], ones_b) for h in chunks]

    zrow = jnp.zeros((1, tm), F32)
    perm = []
    for h in chunks:
        rt = route[h]
        tcnt_ref[h] = tile_cnt[h]
        local = before_tile[h] + jnp.concatenate([seg_off[h]] * reps, axis=1)
        pos1 = jnp.sum(jnp.where(rt.is1, local, 0.0), axis=0, keepdims=True)
        pos2 = jnp.sum(jnp.where(rt.is2, local, 0.0), axis=0, keepdims=True)
        meta_ref[:, rows[h]] = jnp.concatenate(
            [rt.e1.astype(F32), rt.e2.astype(F32), rt.w1, rt.w2, pos1, pos2, zrow, zrow], axis=0)
        srow = lax.broadcasted_iota(jnp.int32, (2 * tm, tm), 0)
        perm.append(jnp.where((srow == pos1.astype(jnp.int32)) | (srow == pos2.astype(jnp.int32)), 1.0, 0.0)
                    .astype(BF16))
    xs = [_dot(perm[h], h1b[h]) for h in chunks]
    for h in chunks:
        base = h * 2 * tm * SUBLANES
        for c in range(xs[h].shape[1] // LANES):
            xls_ref[pl.ds(base + c, 2 * tm, stride=SUBLANES), :] = xs[h][:, c * LANES:(c + 1) * LANES]


class _Route(NamedTuple):
    e1: jax.Array
    e2: jax.Array
    w1: jax.Array
    w2: jax.Array
    is1: jax.Array
    is2: jax.Array
    onehot: jax.Array


def _route(lt, tm):
    r = [lt[k:k + 1, :] for k in range(N_GROUPS)]
    gmax = jnp.maximum(jnp.maximum(r[0], r[1]), jnp.maximum(r[2], r[3]))
    gidx = jnp.where(r[0] == gmax, 0, jnp.where(r[1] == gmax, 1, jnp.where(r[2] == gmax, 2, 3)))
    gsum = (jnp.exp(r[0] - gmax) + jnp.exp(r[1] - gmax)) + (jnp.exp(r[2] - gmax) + jnp.exp(r[3] - gmax))
    gprob = 1.0 / gsum
    epg = EXPERTS_PER_GROUP
    slabs = [lt[8 + g * epg:8 + (g + 1) * epg, :] for g in range(N_GROUPS)]
    el = jnp.where(gidx == 0, slabs[0], jnp.where(gidx == 1, slabs[1], jnp.where(gidx == 2, slabs[2], slabs[3])))
    r8 = lax.broadcasted_iota(jnp.int32, (epg, tm), 0)
    m1 = jnp.max(el, axis=0, keepdims=True)
    i1 = jnp.min(jnp.where(el == m1, r8, epg), axis=0, keepdims=True)
    el2 = jnp.where(r8 == i1, -jnp.inf, el)
    m2 = jnp.max(el2, axis=0, keepdims=True)
    i2 = jnp.min(jnp.where(el2 == m2, r8, epg), axis=0, keepdims=True)
    dlt = jnp.exp(m2 - m1)
    w1 = gprob / (1.0 + dlt)
    w2 = gprob * dlt / (1.0 + dlt)
    e1 = gidx * epg + i1
    e2 = gidx * epg + i2
    r32 = lax.broadcasted_iota(jnp.int32, (N_EXPERTS, tm), 0)
    is1 = r32 == e1
    is2 = r32 == e2
    onehot = jnp.where(is1 | is2, 1.0, 0.0).astype(BF16)
    return _Route(e1, e2, w1, w2, is1, is2, onehot)


def _mix_call(attn2, gate_a, gbb, x2, wa_b, wo_b, g1, b1, wrt_b, brt, alpha):
    t, d = x2.shape
    tile = PROJ_TILE
    kern = functools.partial(_mix_kernel, tile=tile, tm=SUB_TILE, alpha=alpha)
    const = lambda i: (0, 0)
    rowblk = lambda i: (i, 0)
    tm = tile
    return pl.pallas_call(
        kern,
        grid=(t // tm,),
        in_specs=[
            pl.BlockSpec((tm, SB_WIDTH), rowblk),
            pl.BlockSpec((tm, d), rowblk),
            pl.BlockSpec((tm, d), rowblk),
            pl.BlockSpec((tm, d), rowblk),
            pl.BlockSpec((SB_WIDTH, d), const),
            pl.BlockSpec((d, d), const),
            pl.BlockSpec((1, d), const),
            pl.BlockSpec((1, d), const),
            pl.BlockSpec((ROUTER_ROWS, d), const),
            pl.BlockSpec((ROUTER_ROWS, LANES), const),
        ],
        out_specs=[
            pl.BlockSpec((tm, d), rowblk),
            pl.BlockSpec((8, tm), lambda i: (0, i)),
            pl.BlockSpec((2 * tm * SUBLANES, LANES), rowblk),
            pl.BlockSpec((tile // SUB_TILE, N_EXPERTS, LANES), lambda i: (i, 0, 0)),
        ],
        out_shape=[
            jax.ShapeDtypeStruct((t, d), F32),
            jax.ShapeDtypeStruct((8, t), F32),
            jax.ShapeDtypeStruct((2 * t * SUBLANES, LANES), F32),
            jax.ShapeDtypeStruct((t // SUB_TILE, N_EXPERTS, LANES), F32),
        ],
        compiler_params=pltpu.CompilerParams(
            dimension_semantics=("arbitrary",), vmem_limit_bytes=VMEM_LIMIT),
        name="mix",
    )(attn2, gate_a, gbb, x2, wa_b, wo_b, g1, b1, wrt_b, brt)


def _tok_rows(tok, n_tok):
    return pl.ds(pl.multiple_of(tok * SUBLANES, SUBLANES), pl.multiple_of(n_tok * SUBLANES, SUBLANES))


def _moe_kernel(be_ref, r0_ref, t0_ref, t1_ref, nv_ref, nu_ref, nxt_ref, cumt_ref, soff_ref,
                xls_ref, wg_ref, wu_ref, wd_ref, y_ref,
                xbuf_ref, zero_ref, sem, wgf_ref, wuf_ref, wdf_ref, wsem, wgb_ref, wub_ref, wdb_ref, state_ref,
                *, blk, sub, tile_rows):
    i = pl.program_id(0)
    n_used = nu_ref[0]
    e = be_ref[i]
    cur = i % 2
    ne = N_EXPERTS
    w_hbm = (wg_ref, wu_ref, wd_ref)
    wf_refs = (wgf_ref, wuf_ref, wdf_ref)

    def fetch_weights(expert, slot):
        for src, dst in zip(w_hbm, wf_refs):
            pltpu.make_async_copy(src.at[expert], dst.at[slot], wsem.at[slot]).start()

    def wait_weights(slot):
        for src, dst in zip(w_hbm, wf_refs):
            pltpu.make_async_copy(src.at[0], dst.at[slot], wsem.at[slot]).wait()

    def gather(b, slot):
        eb = be_ref[b]
        r0 = r0_ref[b]
        nv = nv_ref[b]

        def seg(t, c):
            first = cumt_ref[t * ne + eb]
            lo = jnp.maximum(first, r0)
            n = jnp.minimum(cumt_ref[(t + 1) * ne + eb], r0 + nv) - lo

            @pl.when(n > 0)
            def _():
                src = t * tile_rows + soff_ref[t * ne + eb] + (lo - first)
                pltpu.make_async_copy(xls_ref.at[_tok_rows(src, n), :],
                                      xbuf_ref.at[slot, _tok_rows(lo - r0, n), :], sem.at[slot]).start()
            return c

        lax.fori_loop(t0_ref[b], t1_ref[b] + 1, seg, 0)

        @pl.when(nv < blk)
        def _():
            pltpu.make_async_copy(zero_ref.at[_tok_rows(0, blk - nv), :],
                                  xbuf_ref.at[slot, _tok_rows(nv, blk - nv), :], sem.at[slot]).start()

    def drain(slot):
        pltpu.make_async_copy(xls_ref.at[pl.ds(0, blk * SUBLANES), :], xbuf_ref.at[slot], sem.at[slot]).wait()

    @pl.when(i == 0)
    def _():
        state_ref[0] = -1
        state_ref[1] = 0
        zero_ref[...] = jnp.zeros(zero_ref.shape, F32)

    @pl.when(jnp.logical_and(i == 0, n_used > 0))
    def _():
        gather(0, 0)
        fetch_weights(e, 0)

    @pl.when(i < n_used)
    def _():
        @pl.when(e != state_ref[0])
        def _():
            slot = state_ref[1]
            wait_weights(slot)
            wgb_ref[...] = wf_refs[0][slot].astype(BF16)
            wub_ref[...] = wf_refs[1][slot].astype(BF16)
            wdb_ref[...] = wf_refs[2][slot].astype(BF16)
            state_ref[0] = e
            state_ref[1] = 1 - slot

            @pl.when(nxt_ref[i] >= 0)
            def _():
                fetch_weights(nxt_ref[i], 1 - slot)

        gather(jnp.minimum(i + 1, n_used - 1), 1 - cur)
        drain(cur)

        def ffn(chunks):
            xbs = [jnp.concatenate([xbuf_ref[cur, pl.ds(h * sub * SUBLANES + c, sub, stride=SUBLANES), :]
                                    for c in range(SUBLANES)], axis=1).astype(BF16) for h in chunks]
            gate = [_dot(xb, wgb_ref[...]) for xb in xbs]
            up = [_dot(xb, wub_ref[...]) for xb in xbs]
            hidden = [(g * jax.nn.sigmoid(g) * u).astype(BF16) for g, u in zip(gate, up)]
            for h, hid in zip(chunks, hidden):
                y = _dot(hid, wdb_ref[...])
                for c in range(SUBLANES):
                    y_ref[pl.ds(h * sub * SUBLANES + c, sub, stride=SUBLANES), :] = y[:, c * LANES:(c + 1) * LANES]

        n_chunks = blk // sub
        live = (nv_ref[i] + sub - 1) // sub
        for n in range(1, n_chunks + 1):
            @pl.when(live == n)
            def _(n=n):
                ffn(list(range(n)))
                if n < n_chunks:
                    y_ref[n * sub * SUBLANES:blk * SUBLANES, :] = jnp.zeros(((blk - n * sub) * SUBLANES, LANES), F32)

    @pl.when(i == n_used - 1)
    def _():
        drain(1 - cur)

    @pl.when(i >= n_used)
    def _():
        y_ref[...] = jnp.zeros(y_ref.shape, F32)


def _moe_call(tables, xls, tile_rows, w_gate, w_up, w_down):
    n_blocks = tables[0].shape[0]
    blk = EXPERT_BLOCK
    _, d, de = w_gate.shape
    grid_spec = pltpu.PrefetchScalarGridSpec(
        num_scalar_prefetch=len(tables),
        grid=(n_blocks,),
        in_specs=[pl.BlockSpec(memory_space=pl.ANY)] * 4,
        out_specs=pl.BlockSpec((blk * SUBLANES, LANES), lambda i, *_: (i, 0)),
        scratch_shapes=[
            pltpu.VMEM((2, blk * SUBLANES, LANES), F32), pltpu.VMEM((blk * SUBLANES, LANES), F32),
            pltpu.SemaphoreType.DMA((2,)),
            pltpu.VMEM((2, d, de), F32), pltpu.VMEM((2, d, de), F32), pltpu.VMEM((2, de, d), F32),
            pltpu.SemaphoreType.DMA((2,)),
            pltpu.VMEM((d, de), BF16), pltpu.VMEM((d, de), BF16), pltpu.VMEM((de, d), BF16),
            pltpu.SMEM((2,), jnp.int32),
        ],
    )
    return pl.pallas_call(
        functools.partial(_moe_kernel, blk=blk, sub=SUB_TILE, tile_rows=tile_rows),
        grid_spec=grid_spec,
        out_shape=jax.ShapeDtypeStruct((n_blocks * blk * SUBLANES, LANES), F32),
        compiler_params=pltpu.CompilerParams(
            dimension_semantics=("arbitrary",), vmem_limit_bytes=VMEM_LIMIT),
        name="moe",
    )(*tables, xls, w_gate, w_up, w_down)


def _combine_kernel(pstart_ref, cumt_ref, soff_ref, tcnt_ref,
                    h1_ref, meta_ref, y_ref, g2_ref, b2_ref, o_ref, ybuf_ref, sem, *, tile, tm, alpha):
    i = pl.program_id(0)
    last = pl.num_programs(0) - 1
    cur = i % 2
    ne = N_EXPERTS
    chunks = range(tile // tm)
    seg_rows = 2 * tm * SUBLANES

    def gather(step, slot):
        for h in chunks:
            t = step * len(chunks) + h

            def seg(e, c, t=t, h=h):
                n = tcnt_ref[t * ne + e]

                @pl.when(n > 0)
                def _():
                    src = pstart_ref[e] + cumt_ref[t * ne + e]
                    dst = h * 2 * tm + soff_ref[t * ne + e]
                    pltpu.make_async_copy(y_ref.at[_tok_rows(src, n), :],
                                          ybuf_ref.at[slot, _tok_rows(dst, n), :], sem.at[slot]).start()
                return c

            lax.fori_loop(0, ne, seg, 0)

    def drain(slot):
        pltpu.make_async_copy(y_ref.at[pl.ds(0, len(chunks) * seg_rows), :], ybuf_ref.at[slot],
                              sem.at[slot]).wait()

    @pl.when(i == 0)
    def _():
        gather(0, 0)

    gather(jnp.minimum(i + 1, last), 1 - cur)
    drain(cur)
    srow = lax.broadcasted_iota(jnp.int32, (2 * tm, tm), 0)
    ys = [jnp.concatenate([ybuf_ref[cur, pl.ds(h * seg_rows + c, 2 * tm, stride=SUBLANES), :]
                           for c in range(SUBLANES)], axis=1).astype(BF16) for h in chunks]
    picked = {}
    for h in chunks:
        cols = slice(h * tm, (h + 1) * tm)
        for k in range(2):
            pos = meta_ref[4 + k:5 + k, cols].astype(jnp.int32)
            picked[h, k] = _dot_tn(jnp.where(srow == pos, 1.0, 0.0).astype(BF16), ys[h])
    for h in chunks:
        rows = slice(h * tm, (h + 1) * tm)
        rw = meta_ref[:, rows].T
        ffn = rw[:, 2:3] * picked[h, 0] + rw[:, 3:4] * picked[h, 1]
        o_ref[rows, :] = _layer_norm(alpha * h1_ref[rows, :] + ffn, g2_ref[...], b2_ref[...])

    @pl.when(i == last)
    def _():
        drain(1 - cur)


def _combine_call(tables, h1, meta, y_slots, g2, b2, alpha):
    t, d = h1.shape
    tm = PROJ_TILE
    kern = functools.partial(_combine_kernel, tile=tm, tm=SUB_TILE, alpha=alpha)
    const = lambda i, *_: (0, 0)
    grid_spec = pltpu.PrefetchScalarGridSpec(
        num_scalar_prefetch=len(tables),
        grid=(t // tm,),
        in_specs=[
            pl.BlockSpec((tm, d), lambda i, *_: (i, 0)),
            pl.BlockSpec((8, tm), lambda i, *_: (0, i)),
            pl.BlockSpec(memory_space=pl.ANY),
            pl.BlockSpec((1, d), const),
            pl.BlockSpec((1, d), const),
        ],
        out_specs=pl.BlockSpec((tm, d), lambda i, *_: (i, 0)),
        scratch_shapes=[pltpu.VMEM((2, 2 * tm * SUBLANES, LANES), F32), pltpu.SemaphoreType.DMA((2,))],
    )
    return pl.pallas_call(
        kern,
        grid_spec=grid_spec,
        out_shape=jax.ShapeDtypeStruct((t, d), F32),
        compiler_params=pltpu.CompilerParams(
            dimension_semantics=("arbitrary",), vmem_limit_bytes=VMEM_LIMIT),
        name="combine",
    )(*tables, h1, meta, y_slots, g2, b2)


def _layer(h, w_in, gate_bias, conv_w, w_branch_a, w_branch_b, w_out, ln1_g, ln1_b,
           w_router_g, b_router_g, w_router_e, b_router_e, w_gate, w_up, w_down, ln2_g, ln2_b, alpha):
    bsz, seq, d = h.shape
    t = bsz * seq
    x2 = h.reshape(t, d)

    qkv, gate_a, gbb = _proj_call(x2, w_in.astype(BF16), gate_bias.reshape(1, 2 * d), conv_w,
                                  w_branch_b.astype(BF16), seq)
    attn = _attn_call(qkv.reshape(bsz, seq, 3 * SB_WIDTH)).reshape(t, SB_WIDTH)

    wrt = jnp.zeros((ROUTER_ROWS, d), F32).at[0:N_GROUPS].set(w_router_g.T).at[8:].set(w_router_e.T)
    brt = jnp.zeros((ROUTER_ROWS,), F32).at[0:N_GROUPS].set(b_router_g).at[8:].set(b_router_e.reshape(-1))
    brt = jnp.broadcast_to(brt[:, None], (ROUTER_ROWS, LANES))
    h1, meta, xls, tcnt = _mix_call(attn, gate_a, gbb, x2, w_branch_a.astype(BF16), w_out.astype(BF16),
                                    ln1_g.reshape(1, d), ln1_b.reshape(1, d), wrt.astype(BF16), brt, alpha)

    blk = EXPERT_BLOCK
    tile_cnt = tcnt[:, :, 0].astype(jnp.int32)
    n_tiles = tile_cnt.shape[0]
    cum_incl = jnp.cumsum(tile_cnt, axis=0)
    cumt = jnp.concatenate([jnp.zeros((1, N_EXPERTS), jnp.int32), cum_incl], axis=0)
    counts = cumt[-1]
    soff = jnp.cumsum(tile_cnt, axis=1) - tile_cnt
    padded = (counts + blk - 1) // blk * blk
    pad_end = jnp.cumsum(padded)
    pad_start = pad_end - padded
    n_blocks = (2 * t) // blk + N_EXPERTS
    block_start = jnp.arange(n_blocks, dtype=jnp.int32) * blk
    block_expert = jnp.minimum(jnp.sum(pad_end[None, :] <= block_start[:, None], axis=1), N_EXPERTS - 1)
    n_used = (pad_end[-1] // blk).reshape(1)

    be = block_expert.astype(jnp.int32)
    mine = be[:, None] == jnp.arange(N_EXPERTS, dtype=jnp.int32)[None, :]

    def of_block(table):
        return jnp.sum(jnp.where(mine, table[..., None, :], 0), axis=-1)

    r0 = block_start - of_block(pad_start)
    nv = jnp.clip(of_block(counts) - r0, 0, blk)
    t0 = jnp.minimum(jnp.sum(of_block(cum_incl) <= r0[None, :], axis=0), n_tiles - 1)
    t1 = jnp.sum(of_block(cumt[:-1]) < (r0 + nv)[None, :], axis=0) - 1
    after = of_block(pad_end) // blk
    be_after = jnp.sum(jnp.where(after[:, None] == jnp.arange(n_blocks, dtype=jnp.int32)[None, :], be[None, :], 0),
                       axis=1)
    nxt = jnp.where(after < n_used[0], be_after, -1)
    tables = (be, r0, t0, t1, nv, n_used, nxt, cumt.reshape(-1), soff.reshape(-1))
    y_slots = _moe_call(tuple(a.astype(jnp.int32) for a in tables), xls, 2 * SUB_TILE, w_gate, w_up, w_down)
    ctables = (pad_start, cumt.reshape(-1), soff.reshape(-1), tile_cnt.reshape(-1))
    out = _combine_call(tuple(a.astype(jnp.int32) for a in ctables), h1, meta, y_slots,
                        ln2_g.reshape(1, d), ln2_b.reshape(1, d), alpha)
    return out.reshape(bsz, seq, d)


def kernel(x, w_in, gate_bias, conv_w, w_branch_a, w_branch_b, w_out, ln1_g, ln1_b, w_router_g, b_router_g,
           w_router_e, b_router_e, w_gate, w_up, w_down, ln2_g, ln2_b):
    depth = w_in.shape[0]
    alpha = (2.0 * depth) ** 0.25
    h = x
    for l in range(depth):
        h = _layer(h, w_in[l], gate_bias[l], conv_w[l], w_branch_a[l], w_branch_b[l], w_out[l], ln1_g[l],
                   ln1_b[l], w_router_g[l], b_router_g[l], w_router_e[l], b_router_e[l], w_gate[l], w_up[l],
                   w_down[l], ln2_g[l], ln2_b[l], alpha)
    return h
```

```python
import functools
from typing import NamedTuple

import jax
import jax.numpy as jnp
from jax import lax
from jax.experimental import pallas as pl
from jax.experimental.pallas import tpu as pltpu

F32 = jnp.float32
BF16 = jnp.bfloat16

SB_HEADS = 8
SB_HEAD_DIM = 64
SB_WIDTH = SB_HEADS * SB_HEAD_DIM
N_GROUPS = 4
EXPERTS_PER_GROUP = 8
N_EXPERTS = N_GROUPS * EXPERTS_PER_GROUP
LN_EPS = 1e-5

LANES = 128
SUBLANES = 8
HEAD_PAIR = LANES
PROJ_TILE = 1024
SUB_TILE = 256
ATTN_TILE = 512
ATTN_WINDOW = 128
ATTN_SUB = 64
ATTN_UNROLL = 2
ATTN_LAG = 2
EXPERT_BLOCK = 1024
ROUTER_ROWS = 8 + N_EXPERTS
SEG_FIELDS = 4
TAIL_CUTOFF = 110.0
MASKED_LOGIT = -1e30
LOG2E = 1.4426950408889634
VMEM_LIMIT = 56 * 1024 * 1024


def _dot(a, b):
    return jnp.dot(a, b, preferred_element_type=F32)


def _dot_nt(a, b):
    return lax.dot_general(a, b, (((1,), (1,)), ((), ())), preferred_element_type=F32)


def _dot_tn(a, b):
    return lax.dot_general(a, b, (((0,), (0,)), ((), ())), preferred_element_type=F32)


def _layer_norm(y, g, b):
    mu = jnp.mean(y, axis=-1, keepdims=True)
    d = y - mu
    var = jnp.mean(d * d, axis=-1, keepdims=True)
    return d * lax.rsqrt(var + LN_EPS) * g + b


def _proj_kernel(x_ref, w_ref, gbias_ref, convw_ref, wbb_ref, qkv_ref, ga_ref, gbb_ref, ubuf_ref,
                 *, tm, sub, seq, d_model):
    i = pl.program_id(0)
    cw = SB_WIDTH
    c0 = 3 * SB_WIDTH
    g0 = c0 + 3 * cw
    cwt = convw_ref[...]

    @pl.when(i == 0)
    def _():
        ubuf_ref[0:8, :] = jnp.zeros((8, cw), F32)

    chunks = range(tm // sub)
    rows = [slice(h * sub, (h + 1) * sub) for h in chunks]
    xb = [x_ref[rows[h], :].astype(BF16) for h in chunks]
    conv_in = [_dot(xb[h], w_ref[:, c0:g0]) for h in chunks]
    qkv = [_dot(xb[h], w_ref[:, 0:c0]) for h in chunks]
    gated = []
    for h in chunks:
        cb = conv_in[h][:, 0:cw]
        u = conv_in[h][:, cw:2 * cw] * conv_in[h][:, 2 * cw:3 * cw]
        ubuf_ref[0:8, :] = jnp.where((i * tm + h * sub) % seq == 0, 0.0, ubuf_ref[0:8, :])
        ubuf_ref[8:sub + 8, :] = u
        y = (cwt[0:1, :] * ubuf_ref[pl.ds(6, sub), :] + cwt[1:2, :] * ubuf_ref[pl.ds(7, sub), :]
             + cwt[2:3, :] * u)
        ubuf_ref[0:8, :] = u[sub - 8:sub, :]
        gated.append((cb * y).astype(BF16))
        qkv_ref[rows[h], 0:SB_WIDTH] = (qkv[h][:, 0:SB_WIDTH] * (SB_HEAD_DIM ** -0.5)).astype(BF16)
        qkv_ref[rows[h], SB_WIDTH:c0] = qkv[h][:, SB_WIDTH:c0].astype(BF16)
    gate_logits = [_dot(xb[h], w_ref[:, g0:g0 + 2 * d_model]) for h in chunks]
    branch_b = [_dot(gated[h], wbb_ref[...]) for h in chunks]
    for h in chunks:
        gates = jax.nn.sigmoid(gate_logits[h] + gbias_ref[...])
        ga_ref[rows[h], :] = gates[:, 0:d_model].astype(BF16)
        gbb_ref[rows[h], :] = (gates[:, d_model:2 * d_model] * branch_b[h]).astype(BF16)


def _proj_call(x2, w_in_b, gbias, conv_w, wbb_b, seq):
    t, d = x2.shape
    pw = w_in_b.shape[1]
    tm = PROJ_TILE
    sub = SUB_TILE
    cw = conv_w.shape[1]
    assert seq % sub == 0 and t % tm == 0
    kern = functools.partial(_proj_kernel, tm=tm, sub=sub, seq=seq, d_model=d)
    const = lambda i: (0, 0)
    return pl.pallas_call(
        kern,
        grid=(t // tm,),
        in_specs=[
            pl.BlockSpec((tm, d), lambda i: (i, 0)),
            pl.BlockSpec((d, pw), const),
            pl.BlockSpec((1, 2 * d), const),
            pl.BlockSpec((3, cw), const),
            pl.BlockSpec((cw, d), const),
        ],
        out_specs=[
            pl.BlockSpec((tm, 3 * SB_WIDTH), lambda i: (i, 0)),
            pl.BlockSpec((tm, d), lambda i: (i, 0)),
            pl.BlockSpec((tm, d), lambda i: (i, 0)),
        ],
        out_shape=[
            jax.ShapeDtypeStruct((t, 3 * SB_WIDTH), BF16),
            jax.ShapeDtypeStruct((t, d), BF16),
            jax.ShapeDtypeStruct((t, d), BF16),
        ],
        scratch_shapes=[pltpu.VMEM((sub + 8, cw), F32)],
        compiler_params=pltpu.CompilerParams(
            dimension_semantics=("arbitrary",), vmem_limit_bytes=VMEM_LIMIT),
        name="proj",
    )(x2, w_in_b, gbias, conv_w, wbb_b)


def _attn_kernel(q_ref, k_ref, v_ref, o_ref, acc_ref, c_ref, *, tb, ts, tq, unroll, lag):
    qi = pl.program_id(1)
    assert ts == LANES
    subs = range(tb // tq)
    lane_q = lax.broadcasted_iota(jnp.int32, (tq, HEAD_PAIR), 1)
    row = lax.broadcasted_iota(jnp.int32, (ts, ts), 0)
    col = lax.broadcasted_iota(jnp.int32, (ts, ts), 1)
    later = (row > col).astype(BF16)
    later2 = jnp.concatenate([later, later], axis=0)
    later2 = jnp.concatenate([later2, jnp.ones((2 * ts, LANES), BF16)], axis=1)
    kcol = lax.broadcasted_iota(jnp.int32, (tq, ts), 1)
    ahead = kcol - lax.broadcasted_iota(jnp.int32, (tq, ts), 0)
    halves = HEAD_PAIR // SB_HEAD_DIM
    in_head_q = [(lane_q >= h * SB_HEAD_DIM) & (lane_q < (h + 1) * SB_HEAD_DIM) for h in range(halves)]
    zero_q = jnp.zeros((tq, HEAD_PAIR), BF16)

    acc_ref[...] = jnp.zeros(acc_ref.shape, F32)
    c_ref[...] = jnp.zeros(c_ref.shape, F32)

    pairs = range(SB_WIDTH // HEAD_PAIR)
    cols = [slice(p * HEAD_PAIR, (p + 1) * HEAD_PAIR) for p in pairs]
    rows = [slice(s * tq, (s + 1) * tq) for s in subs]

    def sweep(w0):
        tiles = [(s, k) for s in subs for k in range(unroll)]
        start, bias = {}, {}
        for s, k in tiles:
            p0 = qi * tb + s * tq
            first = p0 + tq - ts * (w0 + k + 1)
            st = jnp.maximum(first, 0)
            start[s, k] = pl.multiple_of(st, tq)
            ok = (ahead < p0 - st) & (kcol < first + ts - st)
            bias[s, k] = jnp.where(ok, 0.0, MASKED_LOGIT)
        z, sp, split, inner, wb = {}, {}, {}, {}, {}
        groups = [(p, s) for p in pairs for s in subs]
        windows = range(unroll)

        def logits(g):
            p, s = groups[g]
            q = q_ref[rows[s], cols[p]]
            q2 = jnp.concatenate([jnp.where(in_head_q[h], q, zero_q) for h in range(halves)], axis=0)
            for k in windows:
                b2 = jnp.concatenate([bias[s, k]] * halves, axis=0)
                z[g, k] = _dot_nt(q2, k_ref[pl.ds(start[s, k], ts), cols[p]]) + b2

        def stay(g):
            for k in windows:
                zz = z[g, k]
                a = jnp.maximum(zz, 0.0) + jnp.log(1.0 + jnp.exp2(jnp.abs(zz) * (-LOG2E)))
                hi = a.astype(BF16)
                sp[g, k] = a
                split[g, k] = jnp.concatenate([hi, (a - hi.astype(F32)).astype(BF16)], axis=1)

        def tails(g):
            both = _dot(jnp.concatenate([split[g, k] for k in windows], axis=0), later2)
            for k in windows:
                inner[g, k] = both[k * halves * tq:(k + 1) * halves * tq, :]

        def weights(g):
            cvec = c_ref[g]
            for k in windows:
                c = (g, k)
                wb[c] = jnp.exp(z[c] - sp[c] - inner[c][:, 0:ts] - cvec).astype(BF16)
                cvec = cvec + inner[c][:, ts:ts + LANES]
            c_ref[g] = cvec

        def values(g):
            p, s = groups[g]
            out = None
            for k in windows:
                res = _dot(wb[g, k], v_ref[pl.ds(start[s, k], ts), cols[p]])
                pv = jnp.where(in_head_q[0], res[0:tq, :], res[tq:2 * tq, :])
                out = pv if out is None else out + pv
            acc_ref[rows[s], cols[p]] += out

        stages = (logits, stay, tails, weights, values)
        for step in range(len(groups) + lag * (len(stages) - 1)):
            for s in reversed(range(len(stages))):
                g = step - lag * s
                if 0 <= g < len(groups):
                    stages[s](g)

    def cond(carry):
        w0, cmin = carry
        return jnp.logical_and(w0 * ts < (qi + 1) * tb, cmin < TAIL_CUTOFF)

    def body(carry):
        w0, _ = carry
        sweep(w0)
        return w0 + unroll, jnp.min(c_ref[...])

    lax.while_loop(cond, body, (jnp.int32(0), jnp.float32(0.0)))
    o_ref[...] = acc_ref[...].astype(BF16)


def _attn_call(qkv3):
    b, s, _ = qkv3.shape
    tb = ATTN_TILE
    tq = ATTN_SUB
    n_groups = (SB_WIDTH // HEAD_PAIR) * (tb // tq)
    kern = functools.partial(_attn_kernel, tb=tb, ts=ATTN_WINDOW, tq=tq, unroll=ATTN_UNROLL, lag=ATTN_LAG)
    return pl.pallas_call(
        kern,
        grid=(b, s // tb),
        in_specs=[
            pl.BlockSpec((None, tb, SB_WIDTH), lambda bi, qi: (bi, qi, 0)),
            pl.BlockSpec((None, s, SB_WIDTH), lambda bi, qi: (bi, 0, 1)),
            pl.BlockSpec((None, s, SB_WIDTH), lambda bi, qi: (bi, 0, 2)),
        ],
        out_specs=pl.BlockSpec((None, tb, SB_WIDTH), lambda bi, qi: (bi, qi, 0)),
        out_shape=jax.ShapeDtypeStruct((b, s, SB_WIDTH), BF16),
        scratch_shapes=[pltpu.VMEM((tb, SB_WIDTH), F32),
                        pltpu.VMEM((n_groups, (HEAD_PAIR // SB_HEAD_DIM) * tq, LANES), F32)],
        compiler_params=pltpu.CompilerParams(
            dimension_semantics=("arbitrary", "arbitrary"), vmem_limit_bytes=VMEM_LIMIT),
        name="attn",
    )(qkv3, qkv3, qkv3)


def _mix_kernel(attn_ref, ga_ref, gbb_ref, x_ref, wa_ref, wo_ref, g1_ref, b1_ref, wrt_ref, brt_ref,
                h1_ref, meta_ref, xls_ref, tcnt_ref, *, tile, tm, alpha):
    chunks = range(tile // tm)
    reps = tm // LANES
    rows = [slice(h * tm, (h + 1) * tm) for h in chunks]
    branch_a = [_dot(attn_ref[rows[h], :], wa_ref[...]) for h in chunks]
    mixed = [_dot((ga_ref[rows[h], :] * branch_a[h] + gbb_ref[rows[h], :]).astype(BF16), wo_ref[...])
             for h in chunks]
    h1b = []
    for h in chunks:
        h1 = _layer_norm(alpha * x_ref[rows[h], :] + mixed[h], g1_ref[...], b1_ref[...])
        h1_ref[rows[h], :] = h1
        h1b.append(h1.astype(BF16))
    bias = jnp.concatenate([brt_ref[...]] * reps, axis=1)
    logits = [_dot_nt(wrt_ref[...], h1b[h]) + bias for h in chunks]
    route = [_route(logits[h], tm) for h in chunks]

    trow = lax.broadcasted_iota(jnp.int32, (tm, tm), 0)
    tcol = lax.broadcasted_iota(jnp.int32, (tm, tm), 1)
    earlier = (trow < tcol).astype(BF16)
    e32r = lax.broadcasted_iota(jnp.int32, (N_EXPERTS, N_EXPERTS), 0)
    e32c = lax.broadcasted_iota(jnp.int32, (N_EXPERTS, N_EXPERTS), 1)
    lower = (e32c < e32r).astype(BF16)
    ones_b = jnp.ones((tm, LANES), BF16)
    before_tile = [_dot(route[h].onehot, earlier) for h in chunks]
    tile_cnt = [_dot(route[h].onehot, ones_b) for h in chunks]
    smaller = [_dot(lower, route[h].onehot).astype(BF16) for h in chunks]
    seg_off = [_dot(smaller[h], ones_b) for h in chunks]

    zrow = jnp.zeros((1, tm), F32)
    perm = []
    for h in chunks:
        rt = route[h]
        tcnt_ref[h] = tile_cnt[h]
        local = before_tile[h] + jnp.concatenate([seg_off[h]] * reps, axis=1)
        pos1 = jnp.sum(jnp.where(rt.is1, local, 0.0), axis=0, keepdims=True)
        pos2 = jnp.sum(jnp.where(rt.is2, local, 0.0), axis=0, keepdims=True)
        meta_ref[:, rows[h]] = jnp.concatenate(
            [rt.e1.astype(F32), rt.e2.astype(F32), rt.w1, rt.w2, pos1, pos2, zrow, zrow], axis=0)
        srow = lax.broadcasted_iota(jnp.int32, (2 * tm, tm), 0)
        perm.append(jnp.where((srow == pos1.astype(jnp.int32)) | (srow == pos2.astype(jnp.int32)), 1.0, 0.0)
                    .astype(BF16))
    xs = [_dot(perm[h], h1b[h]) for h in chunks]
    for h in chunks:
        base = h * 2 * tm * SUBLANES
        for c in range(xs[h].shape[1] // LANES):
            xls_ref[pl.ds(base + c, 2 * tm, stride=SUBLANES), :] = xs[h][:, c * LANES:(c + 1) * LANES]


class _Route(NamedTuple):
    e1: jax.Array
    e2: jax.Array
    w1: jax.Array
    w2: jax.Array
    is1: jax.Array
    is2: jax.Array
    onehot: jax.Array


def _route(lt, tm):
    r = [lt[k:k + 1, :] for k in range(N_GROUPS)]
    gmax = jnp.maximum(jnp.maximum(r[0], r[1]), jnp.maximum(r[2], r[3]))
    gidx = jnp.where(r[0] == gmax, 0, jnp.where(r[1] == gmax, 1, jnp.where(r[2] == gmax, 2, 3)))
    gsum = (jnp.exp(r[0] - gmax) + jnp.exp(r[1] - gmax)) + (jnp.exp(r[2] - gmax) + jnp.exp(r[3] - gmax))
    gprob = 1.0 / gsum
    epg = EXPERTS_PER_GROUP
    slabs = [lt[8 + g * epg:8 + (g + 1) * epg, :] for g in range(N_GROUPS)]
    el = jnp.where(gidx == 0, slabs[0], jnp.where(gidx == 1, slabs[1], jnp.where(gidx == 2, slabs[2], slabs[3])))
    r8 = lax.broadcasted_iota(jnp.int32, (epg, tm), 0)
    m1 = jnp.max(el, axis=0, keepdims=True)
    i1 = jnp.min(jnp.where(el == m1, r8, epg), axis=0, keepdims=True)
    el2 = jnp.where(r8 == i1, -jnp.inf, el)
    m2 = jnp.max(el2, axis=0, keepdims=True)
    i2 = jnp.min(jnp.where(el2 == m2, r8, epg), axis=0, keepdims=True)
    dlt = jnp.exp(m2 - m1)
    w1 = gprob / (1.0 + dlt)
    w2 = gprob * dlt / (1.0 + dlt)
    e1 = gidx * epg + i1
    e2 = gidx * epg + i2
    r32 = lax.broadcasted_iota(jnp.int32, (N_EXPERTS, tm), 0)
    is1 = r32 == e1
    is2 = r32 == e2
    onehot = jnp.where(is1 | is2, 1.0, 0.0).astype(BF16)
    return _Route(e1, e2, w1, w2, is1, is2, onehot)


def _mix_call(attn2, gate_a, gbb, x2, wa_b, wo_b, g1, b1, wrt_b, brt, alpha):
    t, d = x2.shape
    tile = PROJ_TILE
    kern = functools.partial(_mix_kernel, tile=tile, tm=SUB_TILE, alpha=alpha)
    const = lambda i: (0, 0)
    rowblk = lambda i: (i, 0)
    tm = tile
    return pl.pallas_call(
        kern,
        grid=(t // tm,),
        in_specs=[
            pl.BlockSpec((tm, SB_WIDTH), rowblk),
            pl.BlockSpec((tm, d), rowblk),
            pl.BlockSpec((tm, d), rowblk),
            pl.BlockSpec((tm, d), rowblk),
            pl.BlockSpec((SB_WIDTH, d), const),
            pl.BlockSpec((d, d), const),
            pl.BlockSpec((1, d), const),
            pl.BlockSpec((1, d), const),
            pl.BlockSpec((ROUTER_ROWS, d), const),
            pl.BlockSpec((ROUTER_ROWS, LANES), const),
        ],
        out_specs=[
            pl.BlockSpec((tm, d), rowblk),
            pl.BlockSpec((8, tm), lambda i: (0, i)),
            pl.BlockSpec((2 * tm * SUBLANES, LANES), rowblk),
            pl.BlockSpec((tile // SUB_TILE, N_EXPERTS, LANES), lambda i: (i, 0, 0)),
        ],
        out_shape=[
            jax.ShapeDtypeStruct((t, d), F32),
            jax.ShapeDtypeStruct((8, t), F32),
            jax.ShapeDtypeStruct((2 * t * SUBLANES, LANES), F32),
            jax.ShapeDtypeStruct((t // SUB_TILE, N_EXPERTS, LANES), F32),
        ],
        compiler_params=pltpu.CompilerParams(
            dimension_semantics=("arbitrary",), vmem_limit_bytes=VMEM_LIMIT),
        name="mix",
    )(attn2, gate_a, gbb, x2, wa_b, wo_b, g1, b1, wrt_b, brt)


def _tok_rows(tok, n_tok):
    return pl.ds(pl.multiple_of(tok * SUBLANES, SUBLANES), pl.multiple_of(n_tok * SUBLANES, SUBLANES))


def _moe_kernel(be_ref, r0_ref, t0_ref, t1_ref, nv_ref, nu_ref, nxt_ref, seg_ref,
                xls_ref, wg_ref, wu_ref, wd_ref, y_ref,
                xbuf_ref, zero_ref, sem, wgf_ref, wuf_ref, wdf_ref, wsem, wgb_ref, wub_ref, wdb_ref, state_ref,
                *, blk, sub):
    i = pl.program_id(0)
    n_used = nu_ref[0]
    e = be_ref[i]
    cur = i % 2
    ne = N_EXPERTS
    w_hbm = (wg_ref, wu_ref, wd_ref)
    wf_refs = (wgf_ref, wuf_ref, wdf_ref)

    def fetch_weights(expert, slot):
        for src, dst in zip(w_hbm, wf_refs):
            pltpu.make_async_copy(src.at[expert], dst.at[slot], wsem.at[slot]).start()

    def wait_weights(slot):
        for src, dst in zip(w_hbm, wf_refs):
            pltpu.make_async_copy(src.at[0], dst.at[slot], wsem.at[slot]).wait()

    def gather(b, slot):
        eb = be_ref[b]
        r0 = r0_ref[b]
        nv = nv_ref[b]

        def seg(t, c):
            at = (t * ne + eb) * SEG_FIELDS
            lo = jnp.maximum(seg_ref[at], r0)
            n = jnp.minimum(seg_ref[at + 1], r0 + nv) - lo

            @pl.when(n > 0)
            def _():
                pltpu.make_async_copy(xls_ref.at[_tok_rows(seg_ref[at + 2] + lo, n), :],
                                      xbuf_ref.at[slot, _tok_rows(lo - r0, n), :], sem.at[slot]).start()
            return c

        lax.fori_loop(t0_ref[b], t1_ref[b] + 1, seg, 0)

        @pl.when(nv < blk)
        def _():
            pltpu.make_async_copy(zero_ref.at[_tok_rows(0, blk - nv), :],
                                  xbuf_ref.at[slot, _tok_rows(nv, blk - nv), :], sem.at[slot]).start()

    def drain(slot):
        pltpu.make_async_copy(xls_ref.at[pl.ds(0, blk * SUBLANES), :], xbuf_ref.at[slot], sem.at[slot]).wait()

    @pl.when(i == 0)
    def _():
        state_ref[0] = -1
        state_ref[1] = 0
        zero_ref[...] = jnp.zeros(zero_ref.shape, F32)

    @pl.when(jnp.logical_and(i == 0, n_used > 0))
    def _():
        gather(0, 0)
        fetch_weights(e, 0)

    @pl.when(i < n_used)
    def _():
        @pl.when(e != state_ref[0])
        def _():
            slot = state_ref[1]
            wait_weights(slot)
            wgb_ref[...] = wf_refs[0][slot].astype(BF16)
            wub_ref[...] = wf_refs[1][slot].astype(BF16)
            wdb_ref[...] = wf_refs[2][slot].astype(BF16)
            state_ref[0] = e
            state_ref[1] = 1 - slot

            @pl.when(nxt_ref[i] >= 0)
            def _():
                fetch_weights(nxt_ref[i], 1 - slot)

        gather(jnp.minimum(i + 1, n_used - 1), 1 - cur)
        drain(cur)

        def ffn(chunks):
            xbs = [jnp.concatenate([xbuf_ref[cur, pl.ds(h * sub * SUBLANES + c, sub, stride=SUBLANES), :]
                                    for c in range(SUBLANES)], axis=1).astype(BF16) for h in chunks]
            gate = [_dot(xb, wgb_ref[...]) for xb in xbs]
            up = [_dot(xb, wub_ref[...]) for xb in xbs]
            hidden = [(g * jax.nn.sigmoid(g) * u).astype(BF16) for g, u in zip(gate, up)]
            for h, hid in zip(chunks, hidden):
                y = _dot(hid, wdb_ref[...])
                for c in range(SUBLANES):
                    y_ref[pl.ds(h * sub * SUBLANES + c, sub, stride=SUBLANES), :] = y[:, c * LANES:(c + 1) * LANES]

        n_chunks = blk // sub
        live = (nv_ref[i] + sub - 1) // sub
        for n in range(1, n_chunks + 1):
            @pl.when(live == n)
            def _(n=n):
                ffn(list(range(n)))
                if n < n_chunks:
                    y_ref[n * sub * SUBLANES:blk * SUBLANES, :] = jnp.zeros(((blk - n * sub) * SUBLANES, LANES), F32)

    @pl.when(i == n_used - 1)
    def _():
        drain(1 - cur)

    @pl.when(i >= n_used)
    def _():
        y_ref[...] = jnp.zeros(y_ref.shape, F32)


def _moe_call(tables, xls, w_gate, w_up, w_down):
    n_blocks = tables[0].shape[0]
    blk = EXPERT_BLOCK
    _, d, de = w_gate.shape
    grid_spec = pltpu.PrefetchScalarGridSpec(
        num_scalar_prefetch=len(tables),
        grid=(n_blocks,),
        in_specs=[pl.BlockSpec(memory_space=pl.ANY)] * 4,
        out_specs=pl.BlockSpec((blk * SUBLANES, LANES), lambda i, *_: (i, 0)),
        scratch_shapes=[
            pltpu.VMEM((2, blk * SUBLANES, LANES), F32), pltpu.VMEM((blk * SUBLANES, LANES), F32),
            pltpu.SemaphoreType.DMA((2,)),
            pltpu.VMEM((2, d, de), F32), pltpu.VMEM((2, d, de), F32), pltpu.VMEM((2, de, d), F32),
            pltpu.SemaphoreType.DMA((2,)),
            pltpu.VMEM((d, de), BF16), pltpu.VMEM((d, de), BF16), pltpu.VMEM((de, d), BF16),
            pltpu.SMEM((2,), jnp.int32),
        ],
    )
    return pl.pallas_call(
        functools.partial(_moe_kernel, blk=blk, sub=SUB_TILE),
        grid_spec=grid_spec,
        out_shape=jax.ShapeDtypeStruct((n_blocks * blk * SUBLANES, LANES), F32),
        compiler_params=pltpu.CompilerParams(
            dimension_semantics=("arbitrary",), vmem_limit_bytes=VMEM_LIMIT),
        name="moe",
    )(*tables, xls, w_gate, w_up, w_down)


def _combine_kernel(seg_ref,
                    h1_ref, meta_ref, y_ref, g2_ref, b2_ref, o_ref, ybuf_ref, sem, *, tile, tm, alpha):
    i = pl.program_id(0)
    last = pl.num_programs(0) - 1
    cur = i % 2
    ne = N_EXPERTS
    chunks = range(tile // tm)
    seg_rows = 2 * tm * SUBLANES

    def gather(step, slot):
        for h in chunks:
            t = step * len(chunks) + h

            def seg(e, c, t=t, h=h):
                at = (t * ne + e) * SEG_FIELDS
                n = seg_ref[at]

                @pl.when(n > 0)
                def _():
                    pltpu.make_async_copy(y_ref.at[_tok_rows(seg_ref[at + 1], n), :],
                                          ybuf_ref.at[slot, _tok_rows(h * 2 * tm + seg_ref[at + 2], n), :],
                                          sem.at[slot]).start()
                return c

            lax.fori_loop(0, ne, seg, 0)

    def drain(slot):
        pltpu.make_async_copy(y_ref.at[pl.ds(0, len(chunks) * seg_rows), :], ybuf_ref.at[slot],
                              sem.at[slot]).wait()

    @pl.when(i == 0)
    def _():
        gather(0, 0)

    gather(jnp.minimum(i + 1, last), 1 - cur)
    drain(cur)
    srow = lax.broadcasted_iota(jnp.int32, (2 * tm, tm), 0)
    ys = [jnp.concatenate([ybuf_ref[cur, pl.ds(h * seg_rows + c, 2 * tm, stride=SUBLANES), :]
                           for c in range(SUBLANES)], axis=1).astype(BF16) for h in chunks]
    picked = {}
    for h in chunks:
        cols = slice(h * tm, (h + 1) * tm)
        for k in range(2):
            pos = meta_ref[4 + k:5 + k, cols].astype(jnp.int32)
            picked[h, k] = _dot_tn(jnp.where(srow == pos, 1.0, 0.0).astype(BF16), ys[h])
    for h in chunks:
        rows = slice(h * tm, (h + 1) * tm)
        rw = meta_ref[:, rows].T
        ffn = rw[:, 2:3] * picked[h, 0] + rw[:, 3:4] * picked[h, 1]
        o_ref[rows, :] = _layer_norm(alpha * h1_ref[rows, :] + ffn, g2_ref[...], b2_ref[...])

    @pl.when(i == last)
    def _():
        drain(1 - cur)


def _combine_call(tables, h1, meta, y_slots, g2, b2, alpha):
    t, d = h1.shape
    tm = PROJ_TILE
    kern = functools.partial(_combine_kernel, tile=tm, tm=SUB_TILE, alpha=alpha)
    const = lambda i, *_: (0, 0)
    grid_spec = pltpu.PrefetchScalarGridSpec(
        num_scalar_prefetch=len(tables),
        grid=(t // tm,),
        in_specs=[
            pl.BlockSpec((tm, d), lambda i, *_: (i, 0)),
            pl.BlockSpec((8, tm), lambda i, *_: (0, i)),
            pl.BlockSpec(memory_space=pl.ANY),
            pl.BlockSpec((1, d), const),
            pl.BlockSpec((1, d), const),
        ],
        out_specs=pl.BlockSpec((tm, d), lambda i, *_: (i, 0)),
        scratch_shapes=[pltpu.VMEM((2, 2 * tm * SUBLANES, LANES), F32), pltpu.SemaphoreType.DMA((2,))],
    )
    return pl.pallas_call(
        kern,
        grid_spec=grid_spec,
        out_shape=jax.ShapeDtypeStruct((t, d), F32),
        compiler_params=pltpu.CompilerParams(
            dimension_semantics=("arbitrary",), vmem_limit_bytes=VMEM_LIMIT),
        name="combine",
    )(*tables, h1, meta, y_slots, g2, b2)


def _layer(h, w_in, gate_bias, conv_w, w_branch_a, w_branch_b, w_out, ln1_g, ln1_b,
           w_router_g, b_router_g, w_router_e, b_router_e, w_gate, w_up, w_down, ln2_g, ln2_b, alpha):
    bsz, seq, d = h.shape
    t = bsz * seq
    x2 = h.reshape(t, d)

    qkv, gate_a, gbb = _proj_call(x2, w_in.astype(BF16), gate_bias.reshape(1, 2 * d), conv_w,
                                  w_branch_b.astype(BF16), seq)
    attn = _attn_call(qkv.reshape(bsz, seq, 3 * SB_WIDTH)).reshape(t, SB_WIDTH)

    wrt = jnp.zeros((ROUTER_ROWS, d), F32).at[0:N_GROUPS].set(w_router_g.T).at[8:].set(w_router_e.T)
    brt = jnp.zeros((ROUTER_ROWS,), F32).at[0:N_GROUPS].set(b_router_g).at[8:].set(b_router_e.reshape(-1))
    brt = jnp.broadcast_to(brt[:, None], (ROUTER_ROWS, LANES))
    h1, meta, xls, tcnt = _mix_call(attn, gate_a, gbb, x2, w_branch_a.astype(BF16), w_out.astype(BF16),
                                    ln1_g.reshape(1, d), ln1_b.reshape(1, d), wrt.astype(BF16), brt, alpha)

    blk = EXPERT_BLOCK
    tile_cnt = tcnt[:, :, 0].astype(jnp.int32)
    n_tiles = tile_cnt.shape[0]
    cum_incl = jnp.cumsum(tile_cnt, axis=0)
    cumt = jnp.concatenate([jnp.zeros((1, N_EXPERTS), jnp.int32), cum_incl], axis=0)
    counts = cumt[-1]
    soff = jnp.cumsum(tile_cnt, axis=1) - tile_cnt
    padded = (counts + blk - 1) // blk * blk
    pad_end = jnp.cumsum(padded)
    pad_start = pad_end - padded
    n_blocks = (2 * t) // blk + N_EXPERTS
    block_start = jnp.arange(n_blocks, dtype=jnp.int32) * blk
    block_expert = jnp.minimum(jnp.sum(pad_end[None, :] <= block_start[:, None], axis=1), N_EXPERTS - 1)
    n_used = (pad_end[-1] // blk).reshape(1)

    be = block_expert.astype(jnp.int32)
    mine = be[:, None] == jnp.arange(N_EXPERTS, dtype=jnp.int32)[None, :]

    def of_block(table):
        return jnp.sum(jnp.where(mine, table[..., None, :], 0), axis=-1)

    r0 = block_start - of_block(pad_start)
    nv = jnp.clip(of_block(counts) - r0, 0, blk)
    t0 = jnp.minimum(jnp.sum(of_block(cum_incl) <= r0[None, :], axis=0), n_tiles - 1)
    t1 = jnp.sum(of_block(cumt[:-1]) < (r0 + nv)[None, :], axis=0) - 1
    after = of_block(pad_end) // blk
    be_after = jnp.sum(jnp.where(after[:, None] == jnp.arange(n_blocks, dtype=jnp.int32)[None, :], be[None, :], 0),
                       axis=1)
    nxt = jnp.where(after < n_used[0], be_after, -1)
    tile_base = (jnp.arange(n_tiles, dtype=jnp.int32) * (2 * SUB_TILE))[:, None]
    pad_field = jnp.zeros_like(tile_cnt)
    seg_in = jnp.stack([cumt[:-1], cumt[1:], tile_base + soff - cumt[:-1], pad_field], axis=-1)
    seg_out = jnp.stack([tile_cnt, pad_start[None, :] + cumt[:-1], soff, pad_field], axis=-1)
    tables = (be, r0, t0, t1, nv, n_used, nxt, seg_in.reshape(-1))
    y_slots = _moe_call(tuple(a.astype(jnp.int32) for a in tables), xls, w_gate, w_up, w_down)
    out = _combine_call((seg_out.reshape(-1).astype(jnp.int32),), h1, meta, y_slots,
                        ln2_g.reshape(1, d), ln2_b.reshape(1, d), alpha)
    return out.reshape(bsz, seq, d)


def kernel(x, w_in, gate_bias, conv_w, w_branch_a, w_branch_b, w_out, ln1_g, ln1_b, w_router_g, b_router_g,
           w_router_e, b_router_e, w_gate, w_up, w_down, ln2_g, ln2_b):
    depth = w_in.shape[0]
    alpha = (2.0 * depth) ** 0.25
    h = x
    for l in range(depth):
        h = _layer(h, w_in[l], gate_bias[l], conv_w[l], w_branch_a[l], w_branch_b[l], w_out[l], ln1_g[l],
                   ln1_b[l], w_router_g[l], b_router_g[l], w_router_e[l], b_router_e[l], w_gate[l], w_up[l],
                   w_down[l], ln2_g[l], ln2_b[l], alpha)
    return h
```

```python
import functools
from typing import NamedTuple

import jax
import jax.numpy as jnp
from jax import lax
from jax.experimental import pallas as pl
from jax.experimental.pallas import tpu as pltpu

F32 = jnp.float32
BF16 = jnp.bfloat16

SB_HEADS = 8
SB_HEAD_DIM = 64
SB_WIDTH = SB_HEADS * SB_HEAD_DIM
N_GROUPS = 4
EXPERTS_PER_GROUP = 8
N_EXPERTS = N_GROUPS * EXPERTS_PER_GROUP
LN_EPS = 1e-5

LANES = 128
SUBLANES = 8
HEAD_PAIR = LANES
PROJ_TILE = 1024
SUB_TILE = 256
ATTN_TILE = 512
ATTN_WINDOW = 128
ATTN_SUB = 64
ATTN_UNROLL = 2
ATTN_LAG = 2
EXPERT_BLOCK = 1024
ROUTER_ROWS = 8 + N_EXPERTS
SEG_FIELDS = 3
TAIL_CUTOFF = 110.0
MASKED_LOGIT = -1e30
LOG2E = 1.4426950408889634
VMEM_LIMIT = 56 * 1024 * 1024


def _dot(a, b):
    return jnp.dot(a, b, preferred_element_type=F32)


def _dot_nt(a, b):
    return lax.dot_general(a, b, (((1,), (1,)), ((), ())), preferred_element_type=F32)


def _dot_tn(a, b):
    return lax.dot_general(a, b, (((0,), (0,)), ((), ())), preferred_element_type=F32)


def _layer_norm(y, g, b):
    mu = jnp.mean(y, axis=-1, keepdims=True)
    d = y - mu
    var = jnp.mean(d * d, axis=-1, keepdims=True)
    return d * lax.rsqrt(var + LN_EPS) * g + b


def _proj_kernel(x_ref, w_ref, gbias_ref, convw_ref, wbb_ref, qkv_ref, ga_ref, gbb_ref, ubuf_ref,
                 *, tm, sub, seq, d_model):
    i = pl.program_id(0)
    cw = SB_WIDTH
    c0 = 3 * SB_WIDTH
    g0 = c0 + 3 * cw
    cwt = convw_ref[...]

    @pl.when(i == 0)
    def _():
        ubuf_ref[0:8, :] = jnp.zeros((8, cw), F32)

    chunks = range(tm // sub)
    rows = [slice(h * sub, (h + 1) * sub) for h in chunks]
    xb = [x_ref[rows[h], :].astype(BF16) for h in chunks]
    conv_in = [_dot(xb[h], w_ref[:, c0:g0]) for h in chunks]
    qkv = [_dot(xb[h], w_ref[:, 0:c0]) for h in chunks]
    gated = []
    for h in chunks:
        cb = conv_in[h][:, 0:cw]
        u = conv_in[h][:, cw:2 * cw] * conv_in[h][:, 2 * cw:3 * cw]
        ubuf_ref[0:8, :] = jnp.where((i * tm + h * sub) % seq == 0, 0.0, ubuf_ref[0:8, :])
        ubuf_ref[8:sub + 8, :] = u
        y = (cwt[0:1, :] * ubuf_ref[pl.ds(6, sub), :] + cwt[1:2, :] * ubuf_ref[pl.ds(7, sub), :]
             + cwt[2:3, :] * u)
        ubuf_ref[0:8, :] = u[sub - 8:sub, :]
        gated.append((cb * y).astype(BF16))
        qkv_ref[rows[h], 0:SB_WIDTH] = (qkv[h][:, 0:SB_WIDTH] * (SB_HEAD_DIM ** -0.5)).astype(BF16)
        qkv_ref[rows[h], SB_WIDTH:c0] = qkv[h][:, SB_WIDTH:c0].astype(BF16)
    gate_logits = [_dot(xb[h], w_ref[:, g0:g0 + 2 * d_model]) for h in chunks]
    branch_b = [_dot(gated[h], wbb_ref[...]) for h in chunks]
    for h in chunks:
        gates = jax.nn.sigmoid(gate_logits[h] + gbias_ref[...])
        ga_ref[rows[h], :] = gates[:, 0:d_model].astype(BF16)
        gbb_ref[rows[h], :] = (gates[:, d_model:2 * d_model] * branch_b[h]).astype(BF16)


def _proj_call(x2, w_in_b, gbias, conv_w, wbb_b, seq):
    t, d = x2.shape
    pw = w_in_b.shape[1]
    tm = PROJ_TILE
    sub = SUB_TILE
    cw = conv_w.shape[1]
    assert seq % sub == 0 and t % tm == 0
    kern = functools.partial(_proj_kernel, tm=tm, sub=sub, seq=seq, d_model=d)
    const = lambda i: (0, 0)
    return pl.pallas_call(
        kern,
        grid=(t // tm,),
        in_specs=[
            pl.BlockSpec((tm, d), lambda i: (i, 0)),
            pl.BlockSpec((d, pw), const),
            pl.BlockSpec((1, 2 * d), const),
            pl.BlockSpec((3, cw), const),
            pl.BlockSpec((cw, d), const),
        ],
        out_specs=[
            pl.BlockSpec((tm, 3 * SB_WIDTH), lambda i: (i, 0)),
            pl.BlockSpec((tm, d), lambda i: (i, 0)),
            pl.BlockSpec((tm, d), lambda i: (i, 0)),
        ],
        out_shape=[
            jax.ShapeDtypeStruct((t, 3 * SB_WIDTH), BF16),
            jax.ShapeDtypeStruct((t, d), BF16),
            jax.ShapeDtypeStruct((t, d), BF16),
        ],
        scratch_shapes=[pltpu.VMEM((sub + 8, cw), F32)],
        compiler_params=pltpu.CompilerParams(
            dimension_semantics=("arbitrary",), vmem_limit_bytes=VMEM_LIMIT),
        name="proj",
    )(x2, w_in_b, gbias, conv_w, wbb_b)


def _attn_kernel(q_ref, k_ref, v_ref, o_ref, acc_ref, c_ref, *, tb, ts, tq, unroll, lag):
    qi = pl.program_id(1)
    assert ts == LANES
    subs = range(tb // tq)
    lane_q = lax.broadcasted_iota(jnp.int32, (tq, HEAD_PAIR), 1)
    row = lax.broadcasted_iota(jnp.int32, (ts, ts), 0)
    col = lax.broadcasted_iota(jnp.int32, (ts, ts), 1)
    later = (row > col).astype(BF16)
    later2 = jnp.concatenate([later, later], axis=0)
    later2 = jnp.concatenate([later2, jnp.ones((2 * ts, LANES), BF16)], axis=1)
    kcol = lax.broadcasted_iota(jnp.int32, (tq, ts), 1)
    ahead = kcol - lax.broadcasted_iota(jnp.int32, (tq, ts), 0)
    halves = HEAD_PAIR // SB_HEAD_DIM
    in_head_q = [(lane_q >= h * SB_HEAD_DIM) & (lane_q < (h + 1) * SB_HEAD_DIM) for h in range(halves)]
    zero_q = jnp.zeros((tq, HEAD_PAIR), BF16)

    acc_ref[...] = jnp.zeros(acc_ref.shape, F32)
    c_ref[...] = jnp.zeros(c_ref.shape, F32)

    pairs = range(SB_WIDTH // HEAD_PAIR)
    cols = [slice(p * HEAD_PAIR, (p + 1) * HEAD_PAIR) for p in pairs]
    rows = [slice(s * tq, (s + 1) * tq) for s in subs]

    def sweep(w0):
        tiles = [(s, k) for s in subs for k in range(unroll)]
        start, bias = {}, {}
        for s, k in tiles:
            p0 = qi * tb + s * tq
            first = p0 + tq - ts * (w0 + k + 1)
            st = jnp.maximum(first, 0)
            start[s, k] = pl.multiple_of(st, tq)
            ok = (ahead < p0 - st) & (kcol < first + ts - st)
            bias[s, k] = jnp.where(ok, 0.0, MASKED_LOGIT)
        z, sp, split, inner, wb = {}, {}, {}, {}, {}
        groups = [(p, s) for p in pairs for s in subs]
        windows = range(unroll)

        def logits(g):
            p, s = groups[g]
            q = q_ref[rows[s], cols[p]]
            q2 = jnp.concatenate([jnp.where(in_head_q[h], q, zero_q) for h in range(halves)], axis=0)
            for k in windows:
                b2 = jnp.concatenate([bias[s, k]] * halves, axis=0)
                z[g, k] = _dot_nt(q2, k_ref[pl.ds(start[s, k], ts), cols[p]]) + b2

        def stay(g):
            for k in windows:
                zz = z[g, k]
                a = jnp.maximum(zz, 0.0) + jnp.log(1.0 + jnp.exp2(jnp.abs(zz) * (-LOG2E)))
                hi = a.astype(BF16)
                sp[g, k] = a
                split[g, k] = jnp.concatenate([hi, (a - hi.astype(F32)).astype(BF16)], axis=1)

        def tails(g):
            both = _dot(jnp.concatenate([split[g, k] for k in windows], axis=0), later2)
            for k in windows:
                inner[g, k] = both[k * halves * tq:(k + 1) * halves * tq, :]

        def weights(g):
            cvec = c_ref[g]
            for k in windows:
                c = (g, k)
                wb[c] = jnp.exp(z[c] - sp[c] - inner[c][:, 0:ts] - cvec).astype(BF16)
                cvec = cvec + inner[c][:, ts:ts + LANES]
            c_ref[g] = cvec

        def values(g):
            p, s = groups[g]
            out = None
            for k in windows:
                res = _dot(wb[g, k], v_ref[pl.ds(start[s, k], ts), cols[p]])
                pv = jnp.where(in_head_q[0], res[0:tq, :], res[tq:2 * tq, :])
                out = pv if out is None else out + pv
            acc_ref[rows[s], cols[p]] += out

        stages = (logits, stay, tails, weights, values)
        for step in range(len(groups) + lag * (len(stages) - 1)):
            for s in reversed(range(len(stages))):
                g = step - lag * s
                if 0 <= g < len(groups):
                    stages[s](g)

    def cond(carry):
        w0, cmin = carry
        return jnp.logical_and(w0 * ts < (qi + 1) * tb, cmin < TAIL_CUTOFF)

    def body(carry):
        w0, _ = carry
        sweep(w0)
        return w0 + unroll, jnp.min(c_ref[...])

    lax.while_loop(cond, body, (jnp.int32(0), jnp.float32(0.0)))
    o_ref[...] = acc_ref[...].astype(BF16)


def _attn_call(qkv3):
    b, s, _ = qkv3.shape
    tb = ATTN_TILE
    tq = ATTN_SUB
    n_groups = (SB_WIDTH // HEAD_PAIR) * (tb // tq)
    kern = functools.partial(_attn_kernel, tb=tb, ts=ATTN_WINDOW, tq=tq, unroll=ATTN_UNROLL, lag=ATTN_LAG)
    return pl.pallas_call(
        kern,
        grid=(b, s // tb),
        in_specs=[
            pl.BlockSpec((None, tb, SB_WIDTH), lambda bi, qi: (bi, qi, 0)),
            pl.BlockSpec((None, s, SB_WIDTH), lambda bi, qi: (bi, 0, 1)),
            pl.BlockSpec((None, s, SB_WIDTH), lambda bi, qi: (bi, 0, 2)),
        ],
        out_specs=pl.BlockSpec((None, tb, SB_WIDTH), lambda bi, qi: (bi, qi, 0)),
        out_shape=jax.ShapeDtypeStruct((b, s, SB_WIDTH), BF16),
        scratch_shapes=[pltpu.VMEM((tb, SB_WIDTH), F32),
                        pltpu.VMEM((n_groups, (HEAD_PAIR // SB_HEAD_DIM) * tq, LANES), F32)],
        compiler_params=pltpu.CompilerParams(
            dimension_semantics=("arbitrary", "arbitrary"), vmem_limit_bytes=VMEM_LIMIT),
        name="attn",
    )(qkv3, qkv3, qkv3)


def _mix_kernel(attn_ref, ga_ref, gbb_ref, x_ref, wa_ref, wo_ref, g1_ref, b1_ref, wrt_ref, brt_ref,
                h1_ref, meta_ref, xls_ref, tcnt_ref, *, tile, tm, alpha):
    chunks = range(tile // tm)
    reps = tm // LANES
    rows = [slice(h * tm, (h + 1) * tm) for h in chunks]
    branch_a = [_dot(attn_ref[rows[h], :], wa_ref[...]) for h in chunks]
    mixed = [_dot((ga_ref[rows[h], :] * branch_a[h] + gbb_ref[rows[h], :]).astype(BF16), wo_ref[...])
             for h in chunks]
    h1b = []
    for h in chunks:
        h1 = _layer_norm(alpha * x_ref[rows[h], :] + mixed[h], g1_ref[...], b1_ref[...])
        h1_ref[rows[h], :] = h1
        h1b.append(h1.astype(BF16))
    bias = jnp.concatenate([brt_ref[...]] * reps, axis=1)
    logits = [_dot_nt(wrt_ref[...], h1b[h]) + bias for h in chunks]
    route = [_route(logits[h], tm) for h in chunks]

    trow = lax.broadcasted_iota(jnp.int32, (tm, tm), 0)
    tcol = lax.broadcasted_iota(jnp.int32, (tm, tm), 1)
    earlier = (trow < tcol).astype(BF16)
    e32r = lax.broadcasted_iota(jnp.int32, (N_EXPERTS, N_EXPERTS), 0)
    e32c = lax.broadcasted_iota(jnp.int32, (N_EXPERTS, N_EXPERTS), 1)
    lower = (e32c < e32r).astype(BF16)
    ones_b = jnp.ones((tm, LANES), BF16)
    before_tile = [_dot(route[h].onehot, earlier) for h in chunks]
    tile_cnt = [_dot(route[h].onehot, ones_b) for h in chunks]
    smaller = [_dot(lower, route[h].onehot).astype(BF16) for h in chunks]
    seg_off = [_dot(smaller[h], ones_b) for h in chunks]

    zrow = jnp.zeros((1, tm), F32)
    perm = []
    for h in chunks:
        rt = route[h]
        tcnt_ref[h] = tile_cnt[h]
        local = before_tile[h] + jnp.concatenate([seg_off[h]] * reps, axis=1)
        pos1 = jnp.sum(jnp.where(rt.is1, local, 0.0), axis=0, keepdims=True)
        pos2 = jnp.sum(jnp.where(rt.is2, local, 0.0), axis=0, keepdims=True)
        meta_ref[:, rows[h]] = jnp.concatenate(
            [rt.e1.astype(F32), rt.e2.astype(F32), rt.w1, rt.w2, pos1, pos2, zrow, zrow], axis=0)
        srow = lax.broadcasted_iota(jnp.int32, (2 * tm, tm), 0)
        perm.append(jnp.where((srow == pos1.astype(jnp.int32)) | (srow == pos2.astype(jnp.int32)), 1.0, 0.0)
                    .astype(BF16))
    xs = [_dot(perm[h], h1b[h]) for h in chunks]
    for h in chunks:
        base = h * 2 * tm * SUBLANES
        for c in range(xs[h].shape[1] // LANES):
            xls_ref[pl.ds(base + c, 2 * tm, stride=SUBLANES), :] = xs[h][:, c * LANES:(c + 1) * LANES]


class _Route(NamedTuple):
    e1: jax.Array
    e2: jax.Array
    w1: jax.Array
    w2: jax.Array
    is1: jax.Array
    is2: jax.Array
    onehot: jax.Array


def _route(lt, tm):
    r = [lt[k:k + 1, :] for k in range(N_GROUPS)]
    gmax = jnp.maximum(jnp.maximum(r[0], r[1]), jnp.maximum(r[2], r[3]))
    gidx = jnp.where(r[0] == gmax, 0, jnp.where(r[1] == gmax, 1, jnp.where(r[2] == gmax, 2, 3)))
    gsum = (jnp.exp(r[0] - gmax) + jnp.exp(r[1] - gmax)) + (jnp.exp(r[2] - gmax) + jnp.exp(r[3] - gmax))
    gprob = 1.0 / gsum
    epg = EXPERTS_PER_GROUP
    slabs = [lt[8 + g * epg:8 + (g + 1) * epg, :] for g in range(N_GROUPS)]
    el = jnp.where(gidx == 0, slabs[0], jnp.where(gidx == 1, slabs[1], jnp.where(gidx == 2, slabs[2], slabs[3])))
    r8 = lax.broadcasted_iota(jnp.int32, (epg, tm), 0)
    m1 = jnp.max(el, axis=0, keepdims=True)
    i1 = jnp.min(jnp.where(el == m1, r8, epg), axis=0, keepdims=True)
    el2 = jnp.where(r8 == i1, -jnp.inf, el)
    m2 = jnp.max(el2, axis=0, keepdims=True)
    i2 = jnp.min(jnp.where(el2 == m2, r8, epg), axis=0, keepdims=True)
    dlt = jnp.exp(m2 - m1)
    w1 = gprob / (1.0 + dlt)
    w2 = gprob * dlt / (1.0 + dlt)
    e1 = gidx * epg + i1
    e2 = gidx * epg + i2
    r32 = lax.broadcasted_iota(jnp.int32, (N_EXPERTS, tm), 0)
    is1 = r32 == e1
    is2 = r32 == e2
    onehot = jnp.where(is1 | is2, 1.0, 0.0).astype(BF16)
    return _Route(e1, e2, w1, w2, is1, is2, onehot)


def _mix_call(attn2, gate_a, gbb, x2, wa_b, wo_b, g1, b1, wrt_b, brt, alpha):
    t, d = x2.shape
    tile = PROJ_TILE
    kern = functools.partial(_mix_kernel, tile=tile, tm=SUB_TILE, alpha=alpha)
    const = lambda i: (0, 0)
    rowblk = lambda i: (i, 0)
    tm = tile
    return pl.pallas_call(
        kern,
        grid=(t // tm,),
        in_specs=[
            pl.BlockSpec((tm, SB_WIDTH), rowblk),
            pl.BlockSpec((tm, d), rowblk),
            pl.BlockSpec((tm, d), rowblk),
            pl.BlockSpec((tm, d), rowblk),
            pl.BlockSpec((SB_WIDTH, d), const),
            pl.BlockSpec((d, d), const),
            pl.BlockSpec((1, d), const),
            pl.BlockSpec((1, d), const),
            pl.BlockSpec((ROUTER_ROWS, d), const),
            pl.BlockSpec((ROUTER_ROWS, LANES), const),
        ],
        out_specs=[
            pl.BlockSpec((tm, d), rowblk),
            pl.BlockSpec((8, tm), lambda i: (0, i)),
            pl.BlockSpec((2 * tm * SUBLANES, LANES), rowblk),
            pl.BlockSpec((tile // SUB_TILE, N_EXPERTS, LANES), lambda i: (i, 0, 0)),
        ],
        out_shape=[
            jax.ShapeDtypeStruct((t, d), F32),
            jax.ShapeDtypeStruct((8, t), F32),
            jax.ShapeDtypeStruct((2 * t * SUBLANES, LANES), F32),
            jax.ShapeDtypeStruct((t // SUB_TILE, N_EXPERTS, LANES), F32),
        ],
        compiler_params=pltpu.CompilerParams(
            dimension_semantics=("arbitrary",), vmem_limit_bytes=VMEM_LIMIT),
        name="mix",
    )(attn2, gate_a, gbb, x2, wa_b, wo_b, g1, b1, wrt_b, brt)


def _tok_rows(tok, n_tok):
    return pl.ds(pl.multiple_of(tok * SUBLANES, SUBLANES), pl.multiple_of(n_tok * SUBLANES, SUBLANES))


def _moe_kernel(be_ref, r0_ref, t0_ref, t1_ref, nv_ref, nu_ref, nxt_ref, seg_ref,
                xls_ref, wg_ref, wu_ref, wd_ref, y_ref,
                xbuf_ref, zero_ref, sem, wgf_ref, wuf_ref, wdf_ref, wsem, wgb_ref, wub_ref, wdb_ref, state_ref,
                *, blk, sub):
    i = pl.program_id(0)
    n_used = nu_ref[0]
    e = be_ref[i]
    cur = i % 2
    ne = N_EXPERTS
    n_seg = seg_ref.shape[0] // SEG_FIELDS
    w_hbm = (wg_ref, wu_ref, wd_ref)
    wf_refs = (wgf_ref, wuf_ref, wdf_ref)

    def fetch_weights(expert, slot):
        for src, dst in zip(w_hbm, wf_refs):
            pltpu.make_async_copy(src.at[expert], dst.at[slot], wsem.at[slot]).start()

    def wait_weights(slot):
        for src, dst in zip(w_hbm, wf_refs):
            pltpu.make_async_copy(src.at[0], dst.at[slot], wsem.at[slot]).wait()

    def gather(b, slot):
        eb = be_ref[b]
        r0 = r0_ref[b]
        nv = nv_ref[b]

        def seg(t, c):
            at = t * ne + eb
            lo = jnp.maximum(seg_ref[at], r0)
            n = jnp.minimum(seg_ref[n_seg + at], r0 + nv) - lo

            @pl.when(n > 0)
            def _():
                pltpu.make_async_copy(xls_ref.at[_tok_rows(seg_ref[2 * n_seg + at] + lo, n), :],
                                      xbuf_ref.at[slot, _tok_rows(lo - r0, n), :], sem.at[slot]).start()
            return c

        lax.fori_loop(t0_ref[b], t1_ref[b] + 1, seg, 0)

        @pl.when(nv < blk)
        def _():
            pltpu.make_async_copy(zero_ref.at[_tok_rows(0, blk - nv), :],
                                  xbuf_ref.at[slot, _tok_rows(nv, blk - nv), :], sem.at[slot]).start()

    def drain(slot):
        pltpu.make_async_copy(xls_ref.at[pl.ds(0, blk * SUBLANES), :], xbuf_ref.at[slot], sem.at[slot]).wait()

    @pl.when(i == 0)
    def _():
        state_ref[0] = -1
        state_ref[1] = 0
        zero_ref[...] = jnp.zeros(zero_ref.shape, F32)

    @pl.when(jnp.logical_and(i == 0, n_used > 0))
    def _():
        gather(0, 0)
        fetch_weights(e, 0)

    @pl.when(i < n_used)
    def _():
        @pl.when(e != state_ref[0])
        def _():
            slot = state_ref[1]
            wait_weights(slot)
            wgb_ref[...] = wf_refs[0][slot].astype(BF16)
            wub_ref[...] = wf_refs[1][slot].astype(BF16)
            wdb_ref[...] = wf_refs[2][slot].astype(BF16)
            state_ref[0] = e
            state_ref[1] = 1 - slot

            @pl.when(nxt_ref[i] >= 0)
            def _():
                fetch_weights(nxt_ref[i], 1 - slot)

        gather(jnp.minimum(i + 1, n_used - 1), 1 - cur)
        drain(cur)

        def ffn(chunks):
            xbs = [jnp.concatenate([xbuf_ref[cur, pl.ds(h * sub * SUBLANES + c, sub, stride=SUBLANES), :]
                                    for c in range(SUBLANES)], axis=1).astype(BF16) for h in chunks]
            gate = [_dot(xb, wgb_ref[...]) for xb in xbs]
            up = [_dot(xb, wub_ref[...]) for xb in xbs]
            hidden = [(g * jax.nn.sigmoid(g) * u).astype(BF16) for g, u in zip(gate, up)]
            for h, hid in zip(chunks, hidden):
                y = _dot(hid, wdb_ref[...])
                for c in range(SUBLANES):
                    y_ref[pl.ds(h * sub * SUBLANES + c, sub, stride=SUBLANES), :] = y[:, c * LANES:(c + 1) * LANES]

        n_chunks = blk // sub
        live = (nv_ref[i] + sub - 1) // sub
        for n in range(1, n_chunks + 1):
            @pl.when(live == n)
            def _(n=n):
                ffn(list(range(n)))
                if n < n_chunks:
                    y_ref[n * sub * SUBLANES:blk * SUBLANES, :] = jnp.zeros(((blk - n * sub) * SUBLANES, LANES), F32)

    @pl.when(i == n_used - 1)
    def _():
        drain(1 - cur)

    @pl.when(i >= n_used)
    def _():
        y_ref[...] = jnp.zeros(y_ref.shape, F32)


def _moe_call(tables, xls, w_gate, w_up, w_down):
    n_blocks = tables[0].shape[0]
    blk = EXPERT_BLOCK
    _, d, de = w_gate.shape
    grid_spec = pltpu.PrefetchScalarGridSpec(
        num_scalar_prefetch=len(tables),
        grid=(n_blocks,),
        in_specs=[pl.BlockSpec(memory_space=pl.ANY)] * 4,
        out_specs=pl.BlockSpec((blk * SUBLANES, LANES), lambda i, *_: (i, 0)),
        scratch_shapes=[
            pltpu.VMEM((2, blk * SUBLANES, LANES), F32), pltpu.VMEM((blk * SUBLANES, LANES), F32),
            pltpu.SemaphoreType.DMA((2,)),
            pltpu.VMEM((2, d, de), F32), pltpu.VMEM((2, d, de), F32), pltpu.VMEM((2, de, d), F32),
            pltpu.SemaphoreType.DMA((2,)),
            pltpu.VMEM((d, de), BF16), pltpu.VMEM((d, de), BF16), pltpu.VMEM((de, d), BF16),
            pltpu.SMEM((2,), jnp.int32),
        ],
    )
    return pl.pallas_call(
        functools.partial(_moe_kernel, blk=blk, sub=SUB_TILE),
        grid_spec=grid_spec,
        out_shape=jax.ShapeDtypeStruct((n_blocks * blk * SUBLANES, LANES), F32),
        compiler_params=pltpu.CompilerParams(
            dimension_semantics=("arbitrary",), vmem_limit_bytes=VMEM_LIMIT),
        name="moe",
    )(*tables, xls, w_gate, w_up, w_down)


def _combine_kernel(seg_ref,
                    h1_ref, meta_ref, y_ref, g2_ref, b2_ref, o_ref, ybuf_ref, sem, *, tile, tm, alpha):
    i = pl.program_id(0)
    last = pl.num_programs(0) - 1
    cur = i % 2
    ne = N_EXPERTS
    chunks = range(tile // tm)
    seg_rows = 2 * tm * SUBLANES
    n_seg = seg_ref.shape[0] // SEG_FIELDS

    def gather(step, slot):
        for h in chunks:
            t = step * len(chunks) + h

            def seg(e, c, t=t, h=h):
                at = t * ne + e
                n = seg_ref[at]

                @pl.when(n > 0)
                def _():
                    pltpu.make_async_copy(y_ref.at[_tok_rows(seg_ref[n_seg + at], n), :],
                                          ybuf_ref.at[slot, _tok_rows(h * 2 * tm + seg_ref[2 * n_seg + at], n), :],
                                          sem.at[slot]).start()
                return c

            lax.fori_loop(0, ne, seg, 0)

    def drain(slot):
        pltpu.make_async_copy(y_ref.at[pl.ds(0, len(chunks) * seg_rows), :], ybuf_ref.at[slot],
                              sem.at[slot]).wait()

    @pl.when(i == 0)
    def _():
        gather(0, 0)

    gather(jnp.minimum(i + 1, last), 1 - cur)
    drain(cur)
    srow = lax.broadcasted_iota(jnp.int32, (2 * tm, tm), 0)
    ys = [jnp.concatenate([ybuf_ref[cur, pl.ds(h * seg_rows + c, 2 * tm, stride=SUBLANES), :]
                           for c in range(SUBLANES)], axis=1).astype(BF16) for h in chunks]
    picked = {}
    for h in chunks:
        cols = slice(h * tm, (h + 1) * tm)
        for k in range(2):
            pos = meta_ref[4 + k:5 + k, cols].astype(jnp.int32)
            picked[h, k] = _dot_tn(jnp.where(srow == pos, 1.0, 0.0).astype(BF16), ys[h])
    for h in chunks:
        rows = slice(h * tm, (h + 1) * tm)
        rw = meta_ref[:, rows].T
        ffn = rw[:, 2:3] * picked[h, 0] + rw[:, 3:4] * picked[h, 1]
        o_ref[rows, :] = _layer_norm(alpha * h1_ref[rows, :] + ffn, g2_ref[...], b2_ref[...])

    @pl.when(i == last)
    def _():
        drain(1 - cur)


def _combine_call(tables, h1, meta, y_slots, g2, b2, alpha):
    t, d = h1.shape
    tm = PROJ_TILE
    kern = functools.partial(_combine_kernel, tile=tm, tm=SUB_TILE, alpha=alpha)
    const = lambda i, *_: (0, 0)
    grid_spec = pltpu.PrefetchScalarGridSpec(
        num_scalar_prefetch=len(tables),
        grid=(t // tm,),
        in_specs=[
            pl.BlockSpec((tm, d), lambda i, *_: (i, 0)),
            pl.BlockSpec((8, tm), lambda i, *_: (0, i)),
            pl.BlockSpec(memory_space=pl.ANY),
            pl.BlockSpec((1, d), const),
            pl.BlockSpec((1, d), const),
        ],
        out_specs=pl.BlockSpec((tm, d), lambda i, *_: (i, 0)),
        scratch_shapes=[pltpu.VMEM((2, 2 * tm * SUBLANES, LANES), F32), pltpu.SemaphoreType.DMA((2,))],
    )
    return pl.pallas_call(
        kern,
        grid_spec=grid_spec,
        out_shape=jax.ShapeDtypeStruct((t, d), F32),
        compiler_params=pltpu.CompilerParams(
            dimension_semantics=("arbitrary",), vmem_limit_bytes=VMEM_LIMIT),
        name="combine",
    )(*tables, h1, meta, y_slots, g2, b2)


def _layer(h, w_in, gate_bias, conv_w, w_branch_a, w_branch_b, w_out, ln1_g, ln1_b,
           w_router_g, b_router_g, w_router_e, b_router_e, w_gate, w_up, w_down, ln2_g, ln2_b, alpha):
    bsz, seq, d = h.shape
    t = bsz * seq
    x2 = h.reshape(t, d)

    qkv, gate_a, gbb = _proj_call(x2, w_in.astype(BF16), gate_bias.reshape(1, 2 * d), conv_w,
                                  w_branch_b.astype(BF16), seq)
    attn = _attn_call(qkv.reshape(bsz, seq, 3 * SB_WIDTH)).reshape(t, SB_WIDTH)

    wrt = jnp.zeros((ROUTER_ROWS, d), F32).at[0:N_GROUPS].set(w_router_g.T).at[8:].set(w_router_e.T)
    brt = jnp.zeros((ROUTER_ROWS,), F32).at[0:N_GROUPS].set(b_router_g).at[8:].set(b_router_e.reshape(-1))
    brt = jnp.broadcast_to(brt[:, None], (ROUTER_ROWS, LANES))
    h1, meta, xls, tcnt = _mix_call(attn, gate_a, gbb, x2, w_branch_a.astype(BF16), w_out.astype(BF16),
                                    ln1_g.reshape(1, d), ln1_b.reshape(1, d), wrt.astype(BF16), brt, alpha)

    blk = EXPERT_BLOCK
    tile_cnt = tcnt[:, :, 0].astype(jnp.int32)
    n_tiles = tile_cnt.shape[0]
    cum_incl = jnp.cumsum(tile_cnt, axis=0)
    cumt = jnp.concatenate([jnp.zeros((1, N_EXPERTS), jnp.int32), cum_incl], axis=0)
    counts = cumt[-1]
    soff = jnp.cumsum(tile_cnt, axis=1) - tile_cnt
    padded = (counts + blk - 1) // blk * blk
    pad_end = jnp.cumsum(padded)
    pad_start = pad_end - padded
    n_blocks = (2 * t) // blk + N_EXPERTS
    block_start = jnp.arange(n_blocks, dtype=jnp.int32) * blk
    block_expert = jnp.minimum(jnp.sum(pad_end[None, :] <= block_start[:, None], axis=1), N_EXPERTS - 1)
    n_used = (pad_end[-1] // blk).reshape(1)

    be = block_expert.astype(jnp.int32)
    mine = be[:, None] == jnp.arange(N_EXPERTS, dtype=jnp.int32)[None, :]

    def of_block(table):
        return jnp.sum(jnp.where(mine, table[..., None, :], 0), axis=-1)

    r0 = block_start - of_block(pad_start)
    nv = jnp.clip(of_block(counts) - r0, 0, blk)
    t0 = jnp.minimum(jnp.sum(of_block(cum_incl) <= r0[None, :], axis=0), n_tiles - 1)
    t1 = jnp.sum(of_block(cumt[:-1]) < (r0 + nv)[None, :], axis=0) - 1
    after = of_block(pad_end) // blk
    be_after = jnp.sum(jnp.where(after[:, None] == jnp.arange(n_blocks, dtype=jnp.int32)[None, :], be[None, :], 0),
                       axis=1)
    nxt = jnp.where(after < n_used[0], be_after, -1)
    tile_base = (jnp.arange(n_tiles, dtype=jnp.int32) * (2 * SUB_TILE))[:, None]
    seg_in = jnp.concatenate([a.reshape(-1) for a in (cumt[:-1], cumt[1:], tile_base + soff - cumt[:-1])])
    seg_out = jnp.concatenate([a.reshape(-1) for a in (tile_cnt, pad_start[None, :] + cumt[:-1], soff)])
    tables = (be, r0, t0, t1, nv, n_used, nxt, seg_in)
    y_slots = _moe_call(tuple(a.astype(jnp.int32) for a in tables), xls, w_gate, w_up, w_down)
    out = _combine_call((seg_out.astype(jnp.int32),), h1, meta, y_slots,
                        ln2_g.reshape(1, d), ln2_b.reshape(1, d), alpha)
    return out.reshape(bsz, seq, d)


def kernel(x, w_in, gate_bias, conv_w, w_branch_a, w_branch_b, w_out, ln1_g, ln1_b, w_router_g, b_router_g,
           w_router_e, b_router_e, w_gate, w_up, w_down, ln2_g, ln2_b):
    depth = w_in.shape[0]
    alpha = (2.0 * depth) ** 0.25
    h = x
    for l in range(depth):
        h = _layer(h, w_in[l], gate_bias[l], conv_w[l], w_branch_a[l], w_branch_b[l], w_out[l], ln1_g[l],
                   ln1_b[l], w_router_g[l], b_router_g[l], w_router_e[l], b_router_e[l], w_gate[l], w_up[l],
                   w_down[l], ln2_g[l], ln2_b[l], alpha)
    return h
```
